```python
import jax, jax.numpy as jnp
from jax import lax
import numpy as np

D_MODEL = 2048
BATCH = 2
SEQ = 8192
DEPTH = 1
DEC_BATCH = 16
DEC_SEQ = 16
PAST_LEN = 4096

CHUNK = 64
HEAD_DIM = 64
MIX_WIDTH = D_MODEL
RWKV_WIDTH = MIX_WIDTH // 2
FOX_WIDTH = MIX_WIDTH - RWKV_WIDTH
RWKV_HEADS = RWKV_WIDTH // HEAD_DIM
FOX_HEADS = FOX_WIDTH // HEAD_DIM
DECAY_LORA = 64
AAA_LORA = 64
GATE_LORA = 160
O_R = 0
O_K = RWKV_WIDTH
O_V = 2 * RWKV_WIDTH
O_W = 3 * RWKV_WIDTH
O_A = O_W + DECAY_LORA
O_G = O_A + AAA_LORA
A_COLS = O_G + GATE_LORA
B_COLS = 4 * FOX_WIDTH + FOX_HEADS
IN_COLS = A_COLS + B_COLS
QBLK = 128
N_MEM = 256
MEM_HEADS = 4
MEM_HEAD_DIM = 128
MEM_WIDTH = MEM_HEADS * MEM_HEAD_DIM
N_EXPERTS = 32
TOP_K = 4
D_FF = D_MODEL
SWIGLU_LIMIT = 7.0
SWIGLU_ALPHA = 1.702
RMS_EPS = 1e-6
GN_EPS = 64e-5

kernel_name = 'hybrid_rwkv7_fox_stream_step'


def rmsnorm(x, g):
    xf = x.astype(jnp.float32)
    y = xf * lax.rsqrt(jnp.mean(xf * xf, -1, keepdims=True) + RMS_EPS)
    return (y * g.astype(jnp.float32)).astype(x.dtype)


def rwkv_mix(z, z_prev, s0, mu, w0, w2, a0, a2, g2, k_k, k_a, r_k, lnx_w, lnx_b):
    B, T, _ = z.shape
    zm = z + mu * (jnp.concatenate([z_prev.astype(z.dtype), z[:, :-1]], axis=1) - z)
    r = zm[..., O_R:O_K]
    k = zm[..., O_K:O_V]
    v = zm[..., O_V:O_W]
    zw = zm[..., O_W:O_A]
    za = zm[..., O_A:O_G]
    zg = zm[..., O_G:A_COLS]
    w_log = -jax.nn.softplus(-(w0 + jnp.tanh(zw) @ w2)) - 0.5
    a = jax.nn.sigmoid(a0 + za @ a2)
    g = jax.nn.sigmoid(zg) @ g2
    heads = lambda t: t.reshape(B, T, RWKV_HEADS, HEAD_DIM).astype(jnp.float32)
    kk = heads(k * k_k)
    kk = kk * lax.rsqrt(jnp.maximum(jnp.sum(kk * kk, -1, keepdims=True), 1e-24))
    kh = heads(k * (1.0 + (a - 1.0) * k_a))
    rh, vh, ah = heads(r), heads(v), heads(a)
    decay = jnp.exp(-jnp.exp(heads(w_log)))

    def step(S, inp):
        r_t, d_t, k_t, v_t, kk_t, a_t = inp
        s_kk = jnp.einsum('bhvk,bhk->bhv', S, kk_t)
        S = (S * d_t[:, :, None, :] - s_kk[..., None] * (kk_t * a_t)[:, :, None, :]
             + v_t[..., None] * k_t[:, :, None, :])
        return S, jnp.einsum('bhvk,bhk->bhv', S, r_t)

    tmaj = lambda t: jnp.swapaxes(t, 0, 1)
    s_fin, o = lax.scan(step, s0.astype(jnp.float32),
                        (tmaj(rh), tmaj(decay), tmaj(kh), tmaj(vh), tmaj(kk), tmaj(ah)))
    o = tmaj(o)
    mean = jnp.mean(o, -1, keepdims=True)
    var = jnp.mean(jnp.square(o - mean), -1, keepdims=True)
    on = ((o - mean) * lax.rsqrt(var + GN_EPS)).reshape(B, T, RWKV_WIDTH) * lnx_w + lnx_b
    bonus = (jnp.sum(rh * kh * r_k, -1, keepdims=True) * vh).reshape(B, T, RWKV_WIDTH)
    y = (on + bonus) * g
    return y.astype(z.dtype), s_fin.astype(s0.dtype), z[:, -1:]


def fox_project(zb, b_f, q_gain, k_gain):
    B, T, _ = zb.shape
    hd = lambda t: t.reshape(B, T, FOX_HEADS, HEAD_DIM)
    q = rmsnorm(hd(zb[..., :FOX_WIDTH]), q_gain)
    k = rmsnorm(hd(zb[..., FOX_WIDTH:2 * FOX_WIDTH]), k_gain)
    v = hd(zb[..., 2 * FOX_WIDTH:3 * FOX_WIDTH])
    gate = jax.nn.sigmoid(zb[..., 3 * FOX_WIDTH:4 * FOX_WIDTH])
    logf = jax.nn.log_sigmoid((zb[..., 4 * FOX_WIDTH:] + b_f).astype(jnp.float32))
    return q, k, v, gate, logf


def fox_attend(q, k, v, Fq, Fk, q_pos, k_pos):
    s = jnp.einsum('bqhd,bkhd->bhqk', q, k).astype(jnp.float32) * (HEAD_DIM ** -0.5)
    s = s + Fq[..., :, None] - Fk[..., None, :]
    s = jnp.where(k_pos[None, :] <= q_pos[:, None], s, -jnp.inf)
    p = jax.nn.softmax(s, axis=-1)
    return jnp.einsum('bhqk,bkhd->bqhd', p.astype(v.dtype), v)


def fox_prompt(q, k, v, logf):
    B, S = q.shape[:2]
    F = jnp.swapaxes(jnp.cumsum(logf, axis=1), 1, 2)
    nb = S // QBLK
    qb = jnp.moveaxis(q.reshape(B, nb, QBLK, FOX_HEADS, HEAD_DIM), 1, 0)
    Fb = jnp.moveaxis(F.reshape(B, FOX_HEADS, nb, QBLK), 2, 0)
    pos = jnp.arange(S)

    def blk(args):
        qi, Fi, i = args
        return fox_attend(qi, k, v, Fi, F, i * QBLK + jnp.arange(QBLK), pos)

    out = lax.map(blk, (qb, Fb, jnp.arange(nb)))
    return jnp.moveaxis(out, 0, 1).reshape(B, S, FOX_WIDTH)


def fox_sample(q, k, v, logf, ck, cv, clogf):
    B, T = q.shape[:2]
    P = ck.shape[1]
    k_all = jnp.concatenate([ck.astype(k.dtype), k], axis=1)
    v_all = jnp.concatenate([cv.astype(v.dtype), v], axis=1)
    F = jnp.swapaxes(jnp.cumsum(jnp.concatenate([clogf.astype(jnp.float32), logf], axis=1), axis=1), 1, 2)
    o = fox_attend(q, k_all, v_all, F[..., P:], F, P + jnp.arange(T), jnp.arange(P + T))
    return o.reshape(B, T, FOX_WIDTH)


def mem_kv(mem, g_mem, w_xk, w_xv, k_gain):
    B, M, _ = mem.shape
    mn = rmsnorm(mem, g_mem)
    mk = rmsnorm((mn @ w_xk).reshape(B, M, MEM_HEADS, MEM_HEAD_DIM), k_gain)
    mv = (mn @ w_xv).reshape(B, M, MEM_HEADS, MEM_HEAD_DIM)
    return mk, mv


def cross_attend(h, mk, mv, w_xq, q_gain, w_xo):
    B, T, _ = h.shape
    q = rmsnorm((h @ w_xq).reshape(B, T, MEM_HEADS, MEM_HEAD_DIM), q_gain)
    s = jnp.einsum('bthd,bmhd->bhtm', q, mk.astype(q.dtype)).astype(jnp.float32) * (MEM_HEAD_DIM ** -0.5)
    p = jax.nn.softmax(s, axis=-1)
    o = jnp.einsum('bhtm,bmhd->bthd', p.astype(h.dtype), mv.astype(h.dtype)).reshape(B, T, MEM_WIDTH)
    return o @ w_xo


def moe(h, w_router, b_router, w_gu, b_gu, w_down, b_down):
    logits = (h @ w_router).astype(jnp.float32) + b_router.astype(jnp.float32)
    top_v, top_i = lax.top_k(logits, TOP_K)
    gates = jax.nn.softmax(top_v, axis=-1)
    dense = jnp.einsum('nk,nke->ne', gates,
                       jax.nn.one_hot(top_i, N_EXPERTS, dtype=jnp.float32)).astype(h.dtype)
    y = jnp.zeros_like(h)
    for e in range(N_EXPERTS):
        gu = h @ w_gu[e] + b_gu[e]
        glu = jnp.minimum(gu[:, 0::2], SWIGLU_LIMIT)
        lin = jnp.clip(gu[:, 1::2], -SWIGLU_LIMIT, SWIGLU_LIMIT)
        act = glu * jax.nn.sigmoid(SWIGLU_ALPHA * glu) * (lin + 1.0)
        y = y + dense[:, e:e + 1] * (act @ w_down[e] + b_down[e])
    return y


def setup_inputs(seed: int = 0) -> dict:
    key = jax.random.key(seed)
    keys = list(jax.random.split(key, 64))
    L = DEPTH

    def nrm(shape, scale=1.0):
        return jax.random.normal(keys.pop(), shape, jnp.float32) * scale

    def gain(shape):
        return 1.0 + nrm(shape, 0.02)

    return {
        'x_prompt': nrm((BATCH, SEQ, D_MODEL)),
        'x_sample': nrm((DEC_BATCH, DEC_SEQ, D_MODEL)),
        'mem_prompt': nrm((BATCH, N_MEM, D_MODEL)),
        'cache_fox_k': nrm((L, DEC_BATCH, PAST_LEN, FOX_HEADS, HEAD_DIM)),
        'cache_fox_v': nrm((L, DEC_BATCH, PAST_LEN, FOX_HEADS, HEAD_DIM)),
        'cache_fox_logf': jax.nn.log_sigmoid(3.0 + nrm((L, DEC_BATCH, PAST_LEN, FOX_HEADS))),
        'state_rwkv': nrm((L, DEC_BATCH, RWKV_HEADS, HEAD_DIM, HEAD_DIM), 0.3),
        'state_shift': nrm((L, DEC_BATCH, 1, A_COLS)),
        'cache_mem_k': nrm((L, DEC_BATCH, N_MEM, MEM_HEADS, MEM_HEAD_DIM)),
        'cache_mem_v': nrm((L, DEC_BATCH, N_MEM, MEM_HEADS, MEM_HEAD_DIM)),
        'norm_mix': gain((L, D_MODEL)),
        'w_in': nrm((L, D_MODEL, IN_COLS), D_MODEL ** -0.5),
        'rwkv_mu': jax.random.uniform(keys.pop(), (L, A_COLS), jnp.float32),
        'rwkv_w0': nrm((L, RWKV_WIDTH), 0.5),
        'rwkv_w2': nrm((L, DECAY_LORA, RWKV_WIDTH), 0.5 * DECAY_LORA ** -0.5),
        'rwkv_a0': nrm((L, RWKV_WIDTH), 0.1),
        'rwkv_a2': nrm((L, AAA_LORA, RWKV_WIDTH), AAA_LORA ** -0.5),
        'rwkv_g2': nrm((L, GATE_LORA, RWKV_WIDTH), GATE_LORA ** -0.5),
        'rwkv_k_k': 0.85 + nrm((L, RWKV_WIDTH), 0.05),
        'rwkv_k_a': 1.0 + nrm((L, RWKV_WIDTH), 0.05),
        'rwkv_r_k': nrm((L, RWKV_HEADS, HEAD_DIM), 0.1),
        'rwkv_lnx_w': gain((L, RWKV_WIDTH)),
        'rwkv_lnx_b': nrm((L, RWKV_WIDTH), 0.02),
        'fox_b_f': 3.0 + nrm((L, FOX_HEADS), 0.5),
        'fox_q_norm': gain((L, HEAD_DIM)),
        'fox_k_norm': gain((L, HEAD_DIM)),
        'w_out': nrm((L, MIX_WIDTH, D_MODEL), MIX_WIDTH ** -0.5),
        'norm_cross': gain((L, D_MODEL)),
        'norm_mem': gain((L, D_MODEL)),
        'w_xq': nrm((L, D_MODEL, MEM_WIDTH), D_MODEL ** -0.5),
        'w_xk': nrm((L, D_MODEL, MEM_WIDTH), D_MODEL ** -0.5),
        'w_xv': nrm((L, D_MODEL, MEM_WIDTH), D_MODEL ** -0.5),
        'w_xo': nrm((L, MEM_WIDTH, D_MODEL), MEM_WIDTH ** -0.5),
        'xq_norm': gain((L, MEM_HEAD_DIM)),
        'xk_norm': gain((L, MEM_HEAD_DIM)),
        'norm_ffn': gain((L, D_MODEL)),
        'w_router': nrm((L, D_MODEL, N_EXPERTS), D_MODEL ** -0.5),
        'b_router': nrm((L, N_EXPERTS), 0.01),
        'w_gu': nrm((L, N_EXPERTS, D_MODEL, 2 * D_FF), D_MODEL ** -0.5),
        'b_gu': nrm((L, N_EXPERTS, 2 * D_FF), 0.02),
        'w_down': nrm((L, N_EXPERTS, D_FF, D_MODEL), D_FF ** -0.5),
        'b_down': nrm((L, N_EXPERTS, D_MODEL), 0.02),
    }


def reference(x_prompt, x_sample, mem_prompt, cache_fox_k, cache_fox_v, cache_fox_logf,
              state_rwkv, state_shift, cache_mem_k, cache_mem_v,
              norm_mix, w_in, rwkv_mu, rwkv_w0, rwkv_w2, rwkv_a0, rwkv_a2, rwkv_g2,
              rwkv_k_k, rwkv_k_a, rwkv_r_k, rwkv_lnx_w, rwkv_lnx_b,
              fox_b_f, fox_q_norm, fox_k_norm, w_out,
              norm_cross, norm_mem, w_xq, w_xk, w_xv, w_xo, xq_norm, xk_norm,
              norm_ffn, w_router, b_router, w_gu, b_gu, w_down, b_down):
    Bp = x_prompt.shape[0]
    xp, xs = x_prompt, x_sample
    p_k, p_v, p_lf, p_st, p_sh, p_mk, p_mv = [], [], [], [], [], [], []
    s_k, s_v, s_lf, s_st, s_sh = [], [], [], [], []
    for l in range(DEPTH):
        rw = (rwkv_mu[l], rwkv_w0[l], rwkv_w2[l], rwkv_a0[l], rwkv_a2[l], rwkv_g2[l],
              rwkv_k_k[l], rwkv_k_a[l], rwkv_r_k[l], rwkv_lnx_w[l], rwkv_lnx_b[l])
        fx = (fox_b_f[l], fox_q_norm[l], fox_k_norm[l])
        zp = rmsnorm(xp, norm_mix[l]) @ w_in[l]
        zs = rmsnorm(xs, norm_mix[l]) @ w_in[l]
        ya_p, st_p, sh_p = rwkv_mix(zp[..., :A_COLS], jnp.zeros((Bp, 1, A_COLS), zp.dtype),
                                    jnp.zeros((Bp, RWKV_HEADS, HEAD_DIM, HEAD_DIM), zp.dtype), *rw)
        ya_s, st_s, sh_s = rwkv_mix(zs[..., :A_COLS], state_shift[l], state_rwkv[l], *rw)
        q_p, k_p, v_p, g_p, lf_p = fox_project(zp[..., A_COLS:], *fx)
        yb_p = fox_prompt(q_p, k_p, v_p, lf_p) * g_p
        q_s, k_s, v_s, g_s, lf_s = fox_project(zs[..., A_COLS:], *fx)
        yb_s = fox_sample(q_s, k_s, v_s, lf_s, cache_fox_k[l], cache_fox_v[l], cache_fox_logf[l]) * g_s
        xp = xp + jnp.concatenate([ya_p, yb_p], axis=-1) @ w_out[l]
        xs = xs + jnp.concatenate([ya_s, yb_s], axis=-1) @ w_out[l]
        mk_p, mv_p = mem_kv(mem_prompt, norm_mem[l], w_xk[l], w_xv[l], xk_norm[l])
        xp = xp + cross_attend(rmsnorm(xp, norm_cross[l]), mk_p, mv_p, w_xq[l], xq_norm[l], w_xo[l])
        xs = xs + cross_attend(rmsnorm(xs, norm_cross[l]), cache_mem_k[l], cache_mem_v[l],
                               w_xq[l], xq_norm[l], w_xo[l])
        n_p = xp.shape[0] * xp.shape[1]
        flat = jnp.concatenate([rmsnorm(xp, norm_ffn[l]).reshape(-1, D_MODEL),
                                rmsnorm(xs, norm_ffn[l]).reshape(-1, D_MODEL)], axis=0)
        f = moe(flat, w_router[l], b_router[l], w_gu[l], b_gu[l], w_down[l], b_down[l])
        xp = xp + f[:n_p].reshape(xp.shape)
        xs = xs + f[n_p:].reshape(xs.shape)
        p_k.append(k_p); p_v.append(v_p); p_lf.append(lf_p); p_st.append(st_p); p_sh.append(sh_p)
        p_mk.append(mk_p); p_mv.append(mv_p)
        s_k.append(k_s); s_v.append(v_s); s_lf.append(lf_s); s_st.append(st_s); s_sh.append(sh_s)
    return (xp, xs,
            jnp.stack(p_k), jnp.stack(p_v), jnp.stack(p_lf), jnp.stack(p_st), jnp.stack(p_sh),
            jnp.stack(p_mk), jnp.stack(p_mv),
            jnp.stack(s_k), jnp.stack(s_v), jnp.stack(s_lf), jnp.stack(s_st), jnp.stack(s_sh))
```

```python
import functools

import jax
import jax.numpy as jnp
from jax import lax
from jax.experimental import pallas as pl
from jax.experimental.pallas import tpu as pltpu

F32 = jnp.float32
BF16 = jnp.bfloat16

D_MODEL = 2048
HEAD_DIM = 64
RWKV_WIDTH = 1024
FOX_WIDTH = 1024
RWKV_HEADS = 16
FOX_HEADS = 16
DECAY_LORA = 64
AAA_LORA = 64
GATE_LORA = 160
A_COLS = 3 * RWKV_WIDTH + DECAY_LORA + AAA_LORA + GATE_LORA
N_MEM = 256
MEM_HEADS = 4
MEM_HEAD_DIM = 128
MEM_WIDTH = 512
N_EXPERTS = 32
TOP_K = 4
D_FF = 2048
SWIGLU_LIMIT = 7.0
SWIGLU_ALPHA = 1.702
RMS_EPS = 1e-6
GN_EPS = 64e-5

LANES = 128
HEAD_PAIRS = 8

C_R, C_K, C_V = 0, 1024, 2048
C_FQ, C_FK, C_FV, C_FG = 3072, 4096, 5120, 6144
C_GL = 7168
C_WA = 7424
C_FL = 7552
Z_COLS = 7680

VMEM_LIMIT = 56 * 1024 * 1024
NEG_BIG = -1e30


def _cparams(sem):
    return pltpu.CompilerParams(dimension_semantics=sem, vmem_limit_bytes=VMEM_LIMIT)


def _block_ones():
    r = lax.broadcasted_iota(jnp.int32, (LANES, LANES), 0) // HEAD_DIM
    c = lax.broadcasted_iota(jnp.int32, (LANES, LANES), 1) // HEAD_DIM
    return (r == c).astype(BF16)


def _seg_sum64(x):
    ones = _block_ones()
    outs = []
    for s in range(x.shape[-1] // LANES):
        xs = x[:, s * LANES:(s + 1) * LANES]
        hi = xs.astype(BF16)
        lo = (xs - hi.astype(F32)).astype(BF16)
        outs.append(jnp.dot(hi, ones, preferred_element_type=F32)
                    + jnp.dot(lo, ones, preferred_element_type=F32))
    return outs[0] if len(outs) == 1 else jnp.concatenate(outs, axis=-1)


def _softplus(x):
    return jnp.maximum(x, 0.0) + jnp.log(1.0 + jnp.exp(-jnp.abs(x)))


def _sigmoid(x):
    return 1.0 / (1.0 + jnp.exp(-x))


def _norm_mm_kernel(x_ref, g_ref, w_ref, o_ref, xn_ref):
    @pl.when(pl.program_id(1) == 0)
    def _():
        x = x_ref[...]
        ms = jnp.mean(x * x, axis=-1, keepdims=True)
        xn_ref[...] = (x * lax.rsqrt(ms + RMS_EPS) * g_ref[...]).astype(BF16)

    o_ref[...] = jnp.dot(xn_ref[...], w_ref[...], preferred_element_type=F32)


def _norm_mm(x, g, w, tm, tn):
    m, k = x.shape
    n = w.shape[1]
    return pl.pallas_call(
        _norm_mm_kernel,
        grid=(m // tm, n // tn),
        in_specs=[pl.BlockSpec((tm, k), lambda i, j: (i, 0)),
                  pl.BlockSpec((1, k), lambda i, j: (0, 0)),
                  pl.BlockSpec((k, tn), lambda i, j: (0, j))],
        out_specs=pl.BlockSpec((tm, tn), lambda i, j: (i, j)),
        out_shape=jax.ShapeDtypeStruct((m, n), F32),
        scratch_shapes=[pltpu.VMEM((tm, k), BF16)],
        compiler_params=_cparams(("parallel", "arbitrary")),
        name="norm_mm",
    )(x, g, w)


def _rwkv_prep_kernel(zm_ref, zg_ref, zw_ref, pm_ref, pg_ref, pw_ref,
                      mum_ref, mug_ref, muw_ref, w0_ref, w2_ref, a0_ref, a2_ref, g2_ref,
                      kk_ref, ka_ref, rk_ref,
                      r_o, d_o, k_o, v_o, kk_o, b_o, g_o, bonus_o):
    def shifted(z_ref, p_ref, mu_ref):
        z = z_ref[...]
        rolled = pltpu.roll(z, 1, 0)
        row = lax.broadcasted_iota(jnp.int32, z.shape, 0)
        prev = jnp.where(row == 0, p_ref[0], rolled)
        return z + mu_ref[...] * (prev - z)

    zm = shifted(zm_ref, pm_ref, mum_ref)
    zg = shifted(zg_ref, pg_ref, mug_ref)
    zw = shifted(zw_ref, pw_ref, muw_ref)
    r = zm[:, C_R:C_R + RWKV_WIDTH]
    k = zm[:, C_K:C_K + RWKV_WIDTH]
    v = zm[:, C_V:C_V + RWKV_WIDTH]
    lw = jnp.dot(jnp.tanh(zw).astype(BF16), w2_ref[...], preferred_element_type=F32)
    la = jnp.dot(zw.astype(BF16), a2_ref[...], preferred_element_type=F32)
    g = jnp.dot(_sigmoid(zg).astype(BF16), g2_ref[...], preferred_element_type=F32)
    w_log = -_softplus(-(w0_ref[...] + lw)) - 0.5
    a = _sigmoid(a0_ref[...] + la)
    kk = k * kk_ref[...]
    kk = kk * lax.rsqrt(jnp.maximum(_seg_sum64(kk * kk), 1e-24))
    kh = k * (1.0 + (a - 1.0) * ka_ref[...])
    tiles = lambda x: x.reshape(x.shape[0], HEAD_PAIRS, LANES)
    r_o[...] = tiles(r)
    d_o[...] = tiles(jnp.exp(-jnp.exp(w_log)))
    k_o[...] = tiles(kh)
    v_o[...] = tiles(v)
    kk_o[...] = tiles(kk)
    b_o[...] = tiles(kk * a)
    g_o[...] = g
    bonus_o[...] = _seg_sum64(r * kh * rk_ref[...]) * v


def _rwkv_prep(z, prev_m, prev_g, prev_w, pr, tm):
    n = z.shape[0]
    row = lambda w, c: pl.BlockSpec((tm, w), lambda i, c=c: (i, c))
    prev = lambda w: pl.BlockSpec((1, 1, w), lambda i: (i, 0, 0))
    full = lambda a: pl.BlockSpec(a.shape, lambda i: (0,) * a.ndim)
    params = [pr["mu_m"], pr["mu_g"], pr["mu_w"], pr["w0"], pr["w2"], pr["a0"], pr["a2"], pr["g2"],
              pr["k_k"], pr["k_a"], pr["r_k"]]
    tiled = jax.ShapeDtypeStruct((n, HEAD_PAIRS, LANES), F32)
    flat = jax.ShapeDtypeStruct((n, RWKV_WIDTH), F32)
    return pl.pallas_call(
        _rwkv_prep_kernel,
        grid=(n // tm,),
        in_specs=[row(3072, 0), row(256, C_GL // 256), row(128, C_WA // 128),
                  prev(3072), prev(256), prev(128)] + [full(a) for a in params],
        out_specs=[pl.BlockSpec((tm, HEAD_PAIRS, LANES), lambda i: (i, 0, 0))] * 6
        + [pl.BlockSpec((tm, RWKV_WIDTH), lambda i: (i, 0))] * 2,
        out_shape=[tiled] * 6 + [flat] * 2,
        compiler_params=_cparams(("parallel",)),
        name="rwkv_prep",
    )(z, z, z, prev_m, prev_g, prev_w, *params)


SEQ_PER_STEP = 2
SCAN_BLOCK = 128


def _rwkv_scan_kernel(r_ref, d_ref, k_ref, v_ref, kk_ref, b_ref, g_ref, bonus_ref, s0_ref,
                      lw_ref, lb_ref,
                      y_ref, sf_ref,
                      s_ref, oa_ref, u_ref, *, tb):
    tblk = pl.program_id(1)

    @pl.when(tblk == 0)
    def _():
        s_ref[...] = s0_ref[...]

    oa_ref[...] = jnp.zeros_like(oa_ref)
    ones = _block_ones()
    lane = lax.broadcasted_iota(jnp.int32, (HEAD_DIM, LANES), 1)
    sub = lax.broadcasted_iota(jnp.int32, (HEAD_DIM, LANES), 0)
    diag = (lane % HEAD_DIM) == sub

    def bcast(ref, s, t, hp):
        return jnp.broadcast_to(ref[s, t, hp:hp + 1, :], (HEAD_DIM, LANES))

    def step(t, carry):
        tt = t % HEAD_DIM
        half = t // HEAD_DIM
        for s in range(SEQ_PER_STEP):
            for hp in range(HEAD_PAIRS):
                i = s * HEAD_PAIRS + hp
                st = s_ref[s, hp]
                u_ref[0, i * HEAD_DIM:(i + 1) * HEAD_DIM, :] = (st * bcast(kk_ref, s, t, hp)).astype(BF16)
                u_ref[1, i * HEAD_DIM:(i + 1) * HEAD_DIM, :] = jnp.where(
                    diag, bcast(v_ref, s, t, hp), 0.0).astype(BF16)
        skk = jnp.dot(u_ref[0], ones, preferred_element_type=F32)
        vb = jnp.dot(u_ref[1], ones, preferred_element_type=F32)
        for s in range(SEQ_PER_STEP):
            for hp in range(HEAD_PAIRS):
                i = s * HEAD_PAIRS + hp
                rows = slice(i * HEAD_DIM, (i + 1) * HEAD_DIM)
                st = (s_ref[s, hp] * bcast(d_ref, s, t, hp)
                      - skk[rows] * bcast(b_ref, s, t, hp)
                      + vb[rows] * bcast(k_ref, s, t, hp))
                s_ref[s, hp] = st
                u_ref[2, rows, :] = (st * bcast(r_ref, s, t, hp)).astype(BF16)
        ob = jnp.dot(u_ref[2], ones, preferred_element_type=F32)
        hit = (lane % HEAD_DIM) == tt
        for s in range(SEQ_PER_STEP):
            for hp in range(HEAD_PAIRS):
                i = s * HEAD_PAIRS + hp
                rows = slice(i * HEAD_DIM, (i + 1) * HEAD_DIM)
                cur = oa_ref[s, hp, pl.ds(half * HEAD_DIM, HEAD_DIM), :]
                oa_ref[s, hp, pl.ds(half * HEAD_DIM, HEAD_DIM), :] = jnp.where(hit, ob[rows], cur)
        return carry

    lax.fori_loop(0, tb, step, 0)

    lane_t = lax.broadcasted_iota(jnp.int32, (HEAD_DIM, LANES), 1)
    low = lane_t < HEAD_DIM
    for s in range(SEQ_PER_STEP):
        slabs = []
        for hp in range(HEAD_PAIRS):
            mt = oa_ref[s, hp].T
            ro = pltpu.roll(mt, HEAD_DIM, 1)
            top = jnp.where(low, mt[:HEAD_DIM], ro[HEAD_DIM:])
            bot = jnp.where(low, ro[:HEAD_DIM], mt[HEAD_DIM:])
            slabs.append(jnp.concatenate([top, bot], axis=0)[:tb])
        o = jnp.concatenate(slabs, axis=-1)
        mean = _seg_sum64(o) * (1.0 / HEAD_DIM)
        cen = o - mean
        var = _seg_sum64(cen * cen) * (1.0 / HEAD_DIM)
        on = cen * lax.rsqrt(var + GN_EPS) * lw_ref[...] + lb_ref[...]
        y_ref[s] = ((on + bonus_ref[s]) * g_ref[s]).astype(y_ref.dtype)

    @pl.when(tblk == pl.num_programs(1) - 1)
    def _():
        sf_ref[...] = s_ref[...]


def _rwkv_scan(steps, g, bonus, s0, lnw, lnb, tb):
    nseq, t = g.shape[:2]
    step_spec = pl.BlockSpec((SEQ_PER_STEP, tb, HEAD_PAIRS, LANES), lambda i, j: (i, j, 0, 0))
    seq_spec = pl.BlockSpec((SEQ_PER_STEP, tb, RWKV_WIDTH), lambda i, j: (i, j, 0))
    st_spec = pl.BlockSpec((SEQ_PER_STEP, HEAD_PAIRS, HEAD_DIM, LANES), lambda i, j: (i, 0, 0, 0))
    par_spec = pl.BlockSpec((1, RWKV_WIDTH), lambda i, j: (0, 0))
    nrows = SEQ_PER_STEP * HEAD_PAIRS * HEAD_DIM
    return pl.pallas_call(
        functools.partial(_rwkv_scan_kernel, tb=tb),
        grid=(nseq // SEQ_PER_STEP, t // tb),
        in_specs=[step_spec] * 6 + [seq_spec] * 2 + [st_spec] + [par_spec] * 2,
        out_specs=[seq_spec, st_spec],
        out_shape=[jax.ShapeDtypeStruct((nseq, t, RWKV_WIDTH), BF16),
                   jax.ShapeDtypeStruct(s0.shape, F32)],
        scratch_shapes=[pltpu.VMEM((SEQ_PER_STEP, HEAD_PAIRS, HEAD_DIM, LANES), F32),
                        pltpu.VMEM((SEQ_PER_STEP, HEAD_PAIRS, LANES, LANES), F32),
                        pltpu.VMEM((3, nrows, LANES), BF16)],
        compiler_params=_cparams(("parallel", "arbitrary")),
        name="rwkv_scan",
    )(*steps, g, bonus, s0, lnw, lnb)


def _state_to_pairs(s):
    b = s.shape[0]
    return s.reshape(b, HEAD_PAIRS, 2, HEAD_DIM, HEAD_DIM).transpose(0, 1, 3, 2, 4).reshape(
        b, HEAD_PAIRS, HEAD_DIM, LANES)


def _pairs_to_state(s):
    b = s.shape[0]
    return s.reshape(b, HEAD_PAIRS, HEAD_DIM, 2, HEAD_DIM).transpose(0, 1, 3, 2, 4).reshape(
        b, RWKV_HEADS, HEAD_DIM, HEAD_DIM)


def _fox_proj_kernel(q_ref, k_ref, fl_ref, qg_ref, kg_ref, bf_ref, qb_o, kb_o, kf_o, lf_o):
    def headnorm(x, g):
        ms = _seg_sum64(x * x) * (1.0 / HEAD_DIM)
        return x * lax.rsqrt(ms + RMS_EPS) * g

    q = headnorm(q_ref[...], qg_ref[...])
    k = headnorm(k_ref[...], kg_ref[...])
    qb_o[...] = (q * (HEAD_DIM ** -0.5)).astype(BF16)
    kb_o[...] = k.astype(BF16)
    kf_o[...] = k
    lf_o[...] = -_softplus(-(fl_ref[...] + bf_ref[...]))


def _fox_proj(z, qg, kg, bf, tm):
    n = z.shape[0]
    col = lambda w, c: pl.BlockSpec((tm, w), lambda i, c=c: (i, c))
    par = lambda w: pl.BlockSpec((1, w), lambda i: (0, 0))
    wide = pl.BlockSpec((tm, FOX_WIDTH), lambda i: (i, 0))
    return pl.pallas_call(
        _fox_proj_kernel,
        grid=(n // tm,),
        in_specs=[col(1024, C_FQ // 1024), col(1024, C_FK // 1024), col(128, C_FL // 128),
                  par(1024), par(1024), par(128)],
        out_specs=[wide, wide, wide, pl.BlockSpec((tm, LANES), lambda i: (i, 0))],
        out_shape=[jax.ShapeDtypeStruct((n, FOX_WIDTH), BF16), jax.ShapeDtypeStruct((n, FOX_WIDTH), BF16),
                   jax.ShapeDtypeStruct((n, FOX_WIDTH), F32), jax.ShapeDtypeStruct((n, LANES), F32)],
        compiler_params=_cparams(("parallel",)),
        name="fox_proj",
    )(z, z, z, qg, kg, bf)


def _attn_update(h, s, v, m_ref, l_ref, acc_ref):
    m_prev = m_ref[h]
    m_new = jnp.maximum(m_prev, jnp.max(s, axis=-1, keepdims=True))
    alpha = jnp.exp(m_prev - m_new)
    p = jnp.exp(s - m_new)
    l_ref[h] = alpha * l_ref[h] + jnp.sum(p, axis=-1, keepdims=True)
    acc_ref[h] = alpha * acc_ref[h] + jnp.dot(p.astype(BF16), v, preferred_element_type=F32)
    m_ref[h] = m_new


def _attn_init(q_ref, qm_ref, m_ref, l_ref, acc_ref):
    q = q_ref[...]
    lane = lax.broadcasted_iota(jnp.int32, q.shape, 1)
    zero = jnp.zeros_like(q)
    qm_ref[0] = jnp.where(lane < HEAD_DIM, q, zero)
    qm_ref[1] = jnp.where(lane >= HEAD_DIM, q, zero)
    m_ref[...] = jnp.full_like(m_ref, NEG_BIG)
    l_ref[...] = jnp.zeros_like(l_ref)
    acc_ref[...] = jnp.zeros_like(acc_ref)


def _attn_finish(gate_ref, y_ref, l_ref, acc_ref):
    lane = lax.broadcasted_iota(jnp.int32, acc_ref.shape[1:], 1)
    o = jnp.where(lane < HEAD_DIM, acc_ref[0] / l_ref[0], acc_ref[1] / l_ref[1])
    y_ref[...] = (o * _sigmoid(gate_ref[...])).astype(y_ref.dtype)


def _fox_prompt_kernel(q_ref, k_ref, v_ref, fq_ref, fk_ref, gate_ref, y_ref,
                       qm_ref, m_ref, l_ref, acc_ref, *, tq):
    qi = pl.program_id(2)
    kj = pl.program_id(3)

    @pl.when(kj == 0)
    def _():
        _attn_init(q_ref, qm_ref, m_ref, l_ref, acc_ref)

    def body(masked):
        k = k_ref[...]
        v = v_ref[...].astype(BF16)
        for h in range(2):
            s = lax.dot_general(qm_ref[h], k, (((1,), (1,)), ((), ())), preferred_element_type=F32)
            s = s + fq_ref[0, 0, :, h:h + 1] - fk_ref[0, 0, h:h + 1, :]
            if masked:
                row = lax.broadcasted_iota(jnp.int32, s.shape, 0)
                col = lax.broadcasted_iota(jnp.int32, s.shape, 1)
                s = jnp.where(col <= row, s, NEG_BIG)
            _attn_update(h, s, v, m_ref, l_ref, acc_ref)

    @pl.when(kj < qi)
    def _():
        body(False)

    @pl.when(kj == qi)
    def _():
        body(True)
        _attn_finish(gate_ref, y_ref, l_ref, acc_ref)


def _fox_prompt(qb, kb, vz, fq, fk, z, nb, seq, tq):
    nq = seq // tq
    qspec = pl.BlockSpec((tq, LANES), lambda b, hp, i, j: (b * nq + i, hp))
    kspec = pl.BlockSpec((tq, LANES), lambda b, hp, i, j: (b * nq + jnp.minimum(j, i), hp))
    vspec = pl.BlockSpec((tq, LANES), lambda b, hp, i, j: (b * nq + jnp.minimum(j, i), C_FV // LANES + hp))
    return pl.pallas_call(
        functools.partial(_fox_prompt_kernel, tq=tq),
        grid=(nb, HEAD_PAIRS, nq, nq),
        in_specs=[qspec, kspec, vspec,
                  pl.BlockSpec((1, 1, tq, 2), lambda b, hp, i, j: (b, hp, i, 0)),
                  pl.BlockSpec((1, 1, 2, tq), lambda b, hp, i, j: (b, hp, 0, jnp.minimum(j, i))),
                  pl.BlockSpec((tq, LANES), lambda b, hp, i, j: (b * nq + i, C_FG // LANES + hp))],
        out_specs=qspec,
        out_shape=jax.ShapeDtypeStruct((nb * seq, FOX_WIDTH), BF16),
        scratch_shapes=[pltpu.VMEM((2, tq, LANES), BF16), pltpu.VMEM((2, tq, 1), F32),
                        pltpu.VMEM((2, tq, 1), F32), pltpu.VMEM((2, tq, LANES), F32)],
        compiler_params=_cparams(("parallel", "parallel", "parallel", "arbitrary")),
        name="fox_prompt",
    )(qb, kb, vz, fq, fk, z)


def _fox_sample_kernel(q_ref, kc_ref, vc_ref, kn_ref, vn_ref, fq_ref, fkc_ref, fkn_ref, gate_ref, y_ref,
                       qm_ref, m_ref, l_ref, acc_ref, *, nk):
    kj = pl.program_id(2)

    @pl.when(kj == 0)
    def _():
        _attn_init(q_ref, qm_ref, m_ref, l_ref, acc_ref)

    @pl.when(kj < nk)
    def _():
        k = kc_ref[0].astype(BF16)
        v = vc_ref[0].astype(BF16)
        for h in range(2):
            s = lax.dot_general(qm_ref[h], k, (((1,), (1,)), ((), ())), preferred_element_type=F32)
            s = s + fq_ref[0, 0, :, h:h + 1] - fkc_ref[0, 0, h:h + 1, :]
            _attn_update(h, s, v, m_ref, l_ref, acc_ref)

    @pl.when(kj == nk)
    def _():
        k = kn_ref[...]
        v = vn_ref[...].astype(BF16)
        for h in range(2):
            s = lax.dot_general(qm_ref[h], k, (((1,), (1,)), ((), ())), preferred_element_type=F32)
            s = s + fq_ref[0, 0, :, h:h + 1] - fkn_ref[0, 0, h:h + 1, :]
            row = lax.broadcasted_iota(jnp.int32, s.shape, 0)
            col = lax.broadcasted_iota(jnp.int32, s.shape, 1)
            s = jnp.where(col <= row, s, NEG_BIG)
            _attn_update(h, s, v, m_ref, l_ref, acc_ref)
        _attn_finish(gate_ref, y_ref, l_ref, acc_ref)


def _fox_sample(qb, kc, vc, kb, z, fq, fkc, fkn, nb, t, tk):
    past = kc.shape[1]
    nk = past // tk
    last = nk - 1
    new = pl.BlockSpec((t, LANES), lambda b, hp, j: (b, hp))
    cache = pl.BlockSpec((1, tk, LANES), lambda b, hp, j: (b, jnp.minimum(j, last), hp))
    return pl.pallas_call(
        functools.partial(_fox_sample_kernel, nk=nk),
        grid=(nb, HEAD_PAIRS, nk + 1),
        in_specs=[new, cache, cache, new,
                  pl.BlockSpec((t, LANES), lambda b, hp, j: (b, C_FV // LANES + hp)),
                  pl.BlockSpec((1, 1, t, 2), lambda b, hp, j: (b, hp, 0, 0)),
                  pl.BlockSpec((1, 1, 2, tk), lambda b, hp, j: (b, hp, 0, jnp.minimum(j, last))),
                  pl.BlockSpec((1, 1, 2, t), lambda b, hp, j: (b, hp, 0, 0)),
                  pl.BlockSpec((t, LANES), lambda b, hp, j: (b, C_FG // LANES + hp))],
        out_specs=new,
        out_shape=jax.ShapeDtypeStruct((nb * t, FOX_WIDTH), BF16),
        scratch_shapes=[pltpu.VMEM((2, t, LANES), BF16), pltpu.VMEM((2, t, 1), F32),
                        pltpu.VMEM((2, t, 1), F32), pltpu.VMEM((2, t, LANES), F32)],
        compiler_params=_cparams(("parallel", "parallel", "arbitrary")),
        name="fox_sample",
    )(qb, kc, vc, kb, z, fq, fkc, fkn, z)


def _out_proj_kernel(x_ref, ya_ref, yb_ref, wa_ref, wb_ref, o_ref):
    o_ref[...] = (x_ref[...]
                  + jnp.dot(ya_ref[...], wa_ref[...], preferred_element_type=F32)
                  + jnp.dot(yb_ref[...], wb_ref[...], preferred_element_type=F32))


def _out_proj(x, ya, yb, w, tm):
    n = x.shape[0]
    return pl.pallas_call(
        _out_proj_kernel,
        grid=(n // tm,),
        in_specs=[pl.BlockSpec((tm, D_MODEL), lambda i: (i, 0)),
                  pl.BlockSpec((tm, RWKV_WIDTH), lambda i: (i, 0)),
                  pl.BlockSpec((tm, FOX_WIDTH), lambda i: (i, 0)),
                  pl.BlockSpec((RWKV_WIDTH, D_MODEL), lambda i: (0, 0)),
                  pl.BlockSpec((FOX_WIDTH, D_MODEL), lambda i: (1, 0))],
        out_specs=pl.BlockSpec((tm, D_MODEL), lambda i: (i, 0)),
        out_shape=jax.ShapeDtypeStruct((n, D_MODEL), F32),
        compiler_params=_cparams(("parallel",)),
        name="out_proj",
    )(x, ya, yb, w, w)


def _mem_kv_kernel(x_ref, g_ref, w_ref, kg_ref, o_ref):
    x = x_ref[...]
    ms = jnp.mean(x * x, axis=-1, keepdims=True)
    xn = (x * lax.rsqrt(ms + RMS_EPS) * g_ref[...]).astype(BF16)
    kv = jnp.dot(xn, w_ref[...], preferred_element_type=F32)
    for h in range(MEM_HEADS):
        kh = kv[:, h * MEM_HEAD_DIM:(h + 1) * MEM_HEAD_DIM]
        ms = jnp.mean(kh * kh, axis=-1, keepdims=True)
        o_ref[:, h * MEM_HEAD_DIM:(h + 1) * MEM_HEAD_DIM] = kh * lax.rsqrt(ms + RMS_EPS) * kg_ref[...]
    o_ref[:, MEM_WIDTH:] = kv[:, MEM_WIDTH:]


def _mem_kv(mem, g, w_kv, kg, tm):
    n = mem.shape[0]
    return pl.pallas_call(
        _mem_kv_kernel,
        grid=(n // tm,),
        in_specs=[pl.BlockSpec((tm, D_MODEL), lambda i: (i, 0)),
                  pl.BlockSpec((1, D_MODEL), lambda i: (0, 0)),
                  pl.BlockSpec((D_MODEL, 2 * MEM_WIDTH), lambda i: (0, 0)),
                  pl.BlockSpec((1, MEM_HEAD_DIM), lambda i: (0, 0))],
        out_specs=pl.BlockSpec((tm, 2 * MEM_WIDTH), lambda i: (i, 0)),
        out_shape=jax.ShapeDtypeStruct((n, 2 * MEM_WIDTH), F32),
        compiler_params=_cparams(("parallel",)),
        name="mem_kv",
    )(mem, g, w_kv, kg)


def _cross_router_kernel(x_ref, gc_ref, wq_ref, qg_ref, mk_ref, mv_ref, wo_ref,
                         gf_ref, wr_ref, br_ref,
                         x2_ref, h_ref, idx_ref, gate_ref):
    x = x_ref[...]
    ms = jnp.mean(x * x, axis=-1, keepdims=True)
    xn = (x * lax.rsqrt(ms + RMS_EPS) * gc_ref[...]).astype(BF16)
    q = jnp.dot(xn, wq_ref[...], preferred_element_type=F32)
    outs = []
    for h in range(MEM_HEADS):
        sl = slice(h * MEM_HEAD_DIM, (h + 1) * MEM_HEAD_DIM)
        qh = q[:, sl]
        qms = jnp.mean(qh * qh, axis=-1, keepdims=True)
        qh = (qh * lax.rsqrt(qms + RMS_EPS) * qg_ref[...]).astype(BF16)
        s = lax.dot_general(qh, mk_ref[0, :, sl], (((1,), (1,)), ((), ())), preferred_element_type=F32)
        s = s * (MEM_HEAD_DIM ** -0.5)
        p = jnp.exp(s - jnp.max(s, axis=-1, keepdims=True))
        p = p / jnp.sum(p, axis=-1, keepdims=True)
        outs.append(jnp.dot(p.astype(BF16), mv_ref[0, :, sl], preferred_element_type=F32))
    o = jnp.concatenate(outs, axis=-1).astype(BF16)
    x2 = x + jnp.dot(o, wo_ref[...], preferred_element_type=F32)
    x2_ref[...] = x2

    ms2 = jnp.mean(x2 * x2, axis=-1, keepdims=True)
    hn = x2 * lax.rsqrt(ms2 + RMS_EPS) * gf_ref[...]
    h_ref[...] = hn.astype(BF16)
    logits = jnp.dot(hn, wr_ref[...], preferred_element_type=F32, precision=lax.Precision.HIGHEST)
    logits = logits + br_ref[...]
    lane = lax.broadcasted_iota(jnp.int32, logits.shape, 1)
    idx_acc = jnp.zeros(logits.shape, jnp.int32)
    val_acc = jnp.zeros(logits.shape, F32)
    top = None
    for kk in range(TOP_K):
        m = jnp.max(logits, axis=-1, keepdims=True)
        sel = jnp.min(jnp.where(logits == m, lane, LANES), axis=-1, keepdims=True)
        if kk == 0:
            top = m
        idx_acc = jnp.where(lane == kk, sel, idx_acc)
        val_acc = jnp.where(lane == kk, jnp.exp(m - top), val_acc)
        logits = jnp.where(lane == sel, -jnp.inf, logits)
    idx_ref[...] = idx_acc
    gate_ref[...] = val_acc / jnp.sum(val_acc, axis=-1, keepdims=True)


def _cross_router(x, mk, mv, pc, nb, tm):
    n = x.shape[0]
    per = n // nb // tm
    full = lambda a: pl.BlockSpec(a.shape, lambda i: (0,) * a.ndim)
    mem = pl.BlockSpec((1, N_MEM, MEM_WIDTH), lambda i: (i // per, 0, 0))
    row = lambda w: pl.BlockSpec((tm, w), lambda i: (i, 0))
    params1 = [pc["g_cross"], pc["w_xq"], pc["xq_gain"]]
    params2 = [pc["w_xo"], pc["g_ffn"], pc["w_router"], pc["b_router"]]
    return pl.pallas_call(
        _cross_router_kernel,
        grid=(n // tm,),
        in_specs=[row(D_MODEL)] + [full(a) for a in params1] + [mem, mem] + [full(a) for a in params2],
        out_specs=[row(D_MODEL), row(D_MODEL), row(LANES), row(LANES)],
        out_shape=[jax.ShapeDtypeStruct((n, D_MODEL), F32), jax.ShapeDtypeStruct((n, D_MODEL), BF16),
                   jax.ShapeDtypeStruct((n, LANES), jnp.int32), jax.ShapeDtypeStruct((n, LANES), F32)],
        compiler_params=_cparams(("parallel",)),
        name="cross_router",
    )(x, *params1, mk, mv, *params2)


MOE_TM = 512
MOE_TF = 512


def _moe_kernel(te_ref, nu_ref, xs_ref, wg_ref, wl_ref, bg_ref, bl_ref, wd_ref, bd_ref, gate_ref,
                o_ref, acc_ref):
    i = pl.program_id(0)
    f = pl.program_id(1)

    @pl.when(i < nu_ref[0])
    def _():
        xs = xs_ref[...]
        glu = jnp.dot(xs, wg_ref[0], preferred_element_type=F32) + bg_ref[0]
        lin = jnp.dot(xs, wl_ref[0], preferred_element_type=F32) + bl_ref[0]
        glu = jnp.minimum(glu, SWIGLU_LIMIT)
        lin = jnp.clip(lin, -SWIGLU_LIMIT, SWIGLU_LIMIT)
        act = glu * _sigmoid(SWIGLU_ALPHA * glu) * (lin + 1.0)
        part = jnp.dot(act.astype(BF16), wd_ref[0].astype(BF16), preferred_element_type=F32)

        @pl.when(f == 0)
        def _():
            acc_ref[...] = part

        @pl.when(f > 0)
        def _():
            acc_ref[...] += part

        @pl.when(f == pl.num_programs(1) - 1)
        def _():
            o_ref[...] = (acc_ref[...] + bd_ref[0]) * gate_ref[...]

    @pl.when(jnp.logical_and(i >= nu_ref[0], f == pl.num_programs(1) - 1))
    def _():
        o_ref[...] = jnp.zeros_like(o_ref)


def _moe_experts(tile_expert, n_used, xs, wg, wl, bg, bl, wd, bd, gate_rows):
    p = xs.shape[0]
    nt = p // MOE_TM
    nf = D_FF // MOE_TF
    fi = lambda i, f, nu: jnp.where(i < nu[0], f, nf - 1)
    grid_spec = pltpu.PrefetchScalarGridSpec(
        num_scalar_prefetch=2,
        grid=(nt, nf),
        in_specs=[pl.BlockSpec((MOE_TM, D_MODEL), lambda i, f, te, nu: (i, 0)),
                  pl.BlockSpec((1, D_MODEL, MOE_TF), lambda i, f, te, nu: (te[i], 0, fi(i, f, nu))),
                  pl.BlockSpec((1, D_MODEL, MOE_TF), lambda i, f, te, nu: (te[i], 0, fi(i, f, nu))),
                  pl.BlockSpec((1, 1, MOE_TF), lambda i, f, te, nu: (te[i], 0, fi(i, f, nu))),
                  pl.BlockSpec((1, 1, MOE_TF), lambda i, f, te, nu: (te[i], 0, fi(i, f, nu))),
                  pl.BlockSpec((1, MOE_TF, D_MODEL), lambda i, f, te, nu: (te[i], fi(i, f, nu), 0)),
                  pl.BlockSpec((1, 1, D_MODEL), lambda i, f, te, nu: (te[i], 0, 0)),
                  pl.BlockSpec((MOE_TM, 1), lambda i, f, te, nu: (i, 0))],
        out_specs=pl.BlockSpec((MOE_TM, D_MODEL), lambda i, f, te, nu: (i, 0)),
        scratch_shapes=[pltpu.VMEM((MOE_TM, D_MODEL), F32)],
    )
    return pl.pallas_call(
        _moe_kernel,
        grid_spec=grid_spec,
        out_shape=jax.ShapeDtypeStruct((p, D_MODEL), F32),
        compiler_params=_cparams(("arbitrary", "arbitrary")),
        name="moe_experts",
    )(tile_expert, n_used, xs, wg, wl, bg, bl, wd, bd, gate_rows)


def _route(top_i, gates):
    n = top_i.shape[0]
    na = n * TOP_K
    flat_e = top_i.reshape(na)
    order = jnp.argsort(flat_e, stable=True)
    counts = jnp.zeros((N_EXPERTS,), jnp.int32).at[flat_e].add(1)
    padded = ((counts + MOE_TM - 1) // MOE_TM) * MOE_TM
    pad_start = jnp.cumsum(padded) - padded
    start = jnp.cumsum(counts) - counts
    sorted_e = flat_e[order]
    rank = jnp.arange(na, dtype=jnp.int32) - start[sorted_e]
    dest = pad_start[sorted_e] + rank
    p_rows = ((na + N_EXPERTS * (MOE_TM - 1)) // MOE_TM + 1) * MOE_TM
    row_token = jnp.zeros((p_rows,), jnp.int32).at[dest].set((order // TOP_K).astype(jnp.int32))
    row_gate = jnp.zeros((p_rows,), F32).at[dest].set(gates.reshape(na)[order])
    pos = jnp.zeros((na,), jnp.int32).at[order].set(dest.astype(jnp.int32)).reshape(n, TOP_K)
    tile_start = jnp.arange(p_rows // MOE_TM, dtype=jnp.int32) * MOE_TM
    pad_end = pad_start + padded
    tile_expert = jnp.minimum(jnp.sum(tile_start[:, None] >= pad_end[None, :], axis=1), N_EXPERTS - 1)
    n_used = (jnp.sum(padded) // MOE_TM).astype(jnp.int32).reshape(1)
    return row_token, row_gate, pos, tile_expert.astype(jnp.int32), n_used


def _pad_cols(a, width):
    return jnp.pad(a, ((0, 0), (0, width - a.shape[1])))


def _pack_cols(a):
    o_w = 3 * RWKV_WIDTH
    o_a = o_w + DECAY_LORA
    o_g = o_a + AAA_LORA
    fb = A_COLS
    return jnp.concatenate([
        a[:, 0:3 * RWKV_WIDTH],
        a[:, fb:fb + 4 * FOX_WIDTH],
        _pad_cols(a[:, o_g:A_COLS], 256),
        a[:, o_w:o_g],
        _pad_cols(a[:, fb + 4 * FOX_WIDTH:], LANES),
    ], axis=1)


def _unpack_a_cols(z):
    return jnp.concatenate([z[..., 0:3 * RWKV_WIDTH], z[..., C_WA:C_WA + 128], z[..., C_GL:C_GL + GATE_LORA]],
                           axis=-1)


def _prev_rows(z, tm, first):
    nseq, t, _ = z.shape
    if t > tm:
        inner = z[:, tm - 1:t - 1:tm]
        rows = jnp.concatenate([first, inner], axis=1)
    else:
        rows = first
    rows = rows.reshape(-1, 1, Z_COLS)
    return rows[..., 0:3072], rows[..., C_GL:C_GL + 256], rows[..., C_WA:C_WA + 128]


def kernel(x_prompt, x_sample, mem_prompt, cache_fox_k, cache_fox_v, cache_fox_logf, state_rwkv, state_shift,
           cache_mem_k, cache_mem_v, norm_mix, w_in, rwkv_mu, rwkv_w0, rwkv_w2, rwkv_a0, rwkv_a2, rwkv_g2,
           rwkv_k_k, rwkv_k_a, rwkv_r_k, rwkv_lnx_w, rwkv_lnx_b, fox_b_f, fox_q_norm, fox_k_norm, w_out,
           norm_cross, norm_mem, w_xq, w_xk, w_xv, w_xo, xq_norm, xk_norm, norm_ffn, w_router, b_router,
           w_gu, b_gu, w_down, b_down):
    bp, sp, _ = x_prompt.shape
    bs, ts, _ = x_sample.shape
    past = cache_fox_k.shape[2]
    n_p, n_s = bp * sp, bs * ts
    l = 0

    w_in_p = _pack_cols(w_in[l]).astype(BF16)
    mu_p = _pack_cols(jnp.pad(rwkv_mu[l][None, :], ((0, 0), (0, w_in.shape[2] - A_COLS))))
    zero_lora = jnp.zeros((DECAY_LORA, RWKV_WIDTH), F32)
    prep = {
        "mu_m": mu_p[:, 0:3072], "mu_g": mu_p[:, C_GL:C_GL + 256], "mu_w": mu_p[:, C_WA:C_WA + 128],
        "w0": rwkv_w0[l][None], "a0": rwkv_a0[l][None],
        "w2": jnp.concatenate([rwkv_w2[l], zero_lora], axis=0).astype(BF16),
        "a2": jnp.concatenate([zero_lora, rwkv_a2[l]], axis=0).astype(BF16),
        "g2": jnp.pad(rwkv_g2[l], ((0, 256 - GATE_LORA), (0, 0))).astype(BF16),
        "k_k": rwkv_k_k[l][None], "k_a": rwkv_k_a[l][None], "r_k": rwkv_r_k[l].reshape(1, RWKV_WIDTH),
    }
    lnw, lnb = rwkv_lnx_w[l][None], rwkv_lnx_b[l][None]
    qg = jnp.tile(fox_q_norm[l], FOX_HEADS)[None]
    kg = jnp.tile(fox_k_norm[l], FOX_HEADS)[None]
    bf = _pad_cols(fox_b_f[l][None], LANES)
    w_out_b = w_out[l].astype(BF16)
    w_kv = jnp.concatenate([w_xk[l], w_xv[l]], axis=1).astype(BF16)
    pc = {
        "g_cross": norm_cross[l][None], "w_xq": w_xq[l].astype(BF16), "xq_gain": xq_norm[l][None],
        "w_xo": w_xo[l].astype(BF16), "g_ffn": norm_ffn[l][None],
        "w_router": _pad_cols(w_router[l], LANES),
        "b_router": jnp.concatenate([b_router[l], jnp.full((LANES - N_EXPERTS,), NEG_BIG, F32)])[None],
    }
    wg = w_gu[l][:, :, 0::2].astype(BF16)
    wl = w_gu[l][:, :, 1::2].astype(BF16)
    bg = b_gu[l][:, None, 0::2]
    bl = b_gu[l][:, None, 1::2]
    bd = b_down[l][:, None, :]

    def mixers(x, nseq, t, first_shift, s0, tm_mm, tm_prep, tb):
        z = _norm_mm(x, norm_mix[l][None], w_in_p, tm_mm, 768)
        z3 = z.reshape(nseq, t, Z_COLS)
        pm, pg, pw = _prev_rows(z3, tm_prep, first_shift)
        *steps, g, bonus = _rwkv_prep(z, pm, pg, pw, prep, tm_prep)
        steps = [a.reshape(nseq, t, HEAD_PAIRS, LANES) for a in steps]
        ya, s_fin = _rwkv_scan(steps, g.reshape(nseq, t, RWKV_WIDTH), bonus.reshape(nseq, t, RWKV_WIDTH),
                               _state_to_pairs(s0), lnw, lnb, tb)
        qb, kb, kf, lf = _fox_proj(z, qg, kg, bf, tm_prep)
        return z, z3, ya.reshape(nseq * t, RWKV_WIDTH), _pairs_to_state(s_fin), qb, kb, kf, lf[:, :FOX_HEADS]

    zero_shift = jnp.zeros((bp, 1, Z_COLS), F32)
    zero_state = jnp.zeros((bp, RWKV_HEADS, HEAD_DIM, HEAD_DIM), F32)
    xp = x_prompt.reshape(n_p, D_MODEL)
    xs = x_sample.reshape(n_s, D_MODEL)
    zp, zp3, ya_p, st_p, qb_p, kb_p, kf_p, lf_p = mixers(xp, bp, sp, zero_shift, zero_state, 1024, 256,
                                                         SCAN_BLOCK)
    shift_s = _pack_cols(jnp.pad(state_shift[l].reshape(bs, A_COLS), ((0, 0), (0, w_in.shape[2] - A_COLS))))
    zs, zs3, ya_s, st_s, qb_s, kb_s, kf_s, lf_s = mixers(xs, bs, ts, shift_s.reshape(bs, 1, Z_COLS),
                                                         state_rwkv[l], 256, ts, ts)

    def f_layouts(fcum, nseq):
        f4 = fcum.reshape(nseq, -1, HEAD_PAIRS, 2)
        return f4.transpose(0, 2, 1, 3), f4.transpose(0, 2, 3, 1)

    fp = jnp.cumsum(lf_p.reshape(bp, sp, FOX_HEADS), axis=1)
    fq_p, fk_p = f_layouts(fp, bp)
    yb_p = _fox_prompt(qb_p, kb_p, zp, fq_p, fk_p, zp, bp, sp, 512)

    fs = jnp.cumsum(jnp.concatenate([cache_fox_logf[l].astype(F32), lf_s.reshape(bs, ts, FOX_HEADS)], axis=1),
                    axis=1)
    fq_s, fkn_s = f_layouts(fs[:, past:], bs)
    _, fkc_s = f_layouts(fs[:, :past], bs)
    kc = cache_fox_k[l].reshape(bs, past, FOX_WIDTH)
    vc = cache_fox_v[l].reshape(bs, past, FOX_WIDTH)
    yb_s = _fox_sample(qb_s, kc, vc, kb_s, zs, fq_s, fkc_s, fkn_s, bs, ts, 1024)

    x1_p = _out_proj(xp, ya_p, yb_p, w_out_b, 512)
    x1_s = _out_proj(xs, ya_s, yb_s, w_out_b, 256)

    kv_p = _mem_kv(mem_prompt.reshape(bp * N_MEM, D_MODEL), norm_mem[l][None], w_kv, xk_norm[l][None], 256)
    mk_p = kv_p[:, :MEM_WIDTH].reshape(bp, N_MEM, MEM_WIDTH)
    mv_p = kv_p[:, MEM_WIDTH:].reshape(bp, N_MEM, MEM_WIDTH)
    x2_p, h_p, ti_p, tg_p = _cross_router(x1_p, mk_p.astype(BF16), mv_p.astype(BF16), pc, bp, 256)
    mk_s = cache_mem_k[l].reshape(bs, N_MEM, MEM_WIDTH).astype(BF16)
    mv_s = cache_mem_v[l].reshape(bs, N_MEM, MEM_WIDTH).astype(BF16)
    x2_s, h_s, ti_s, tg_s = _cross_router(x1_s, mk_s, mv_s, pc, bs, ts)

    h_all = jnp.concatenate([h_p, h_s], axis=0)
    top_i = jnp.concatenate([ti_p[:, :TOP_K], ti_s[:, :TOP_K]], axis=0)
    gates = jnp.concatenate([tg_p[:, :TOP_K], tg_s[:, :TOP_K]], axis=0)
    row_token, row_gate, pos, tile_expert, n_used = _route(top_i, gates)
    xs_rows = h_all[row_token]
    out_rows = _moe_experts(tile_expert, n_used, xs_rows, wg, wl, bg, bl, w_down[l], bd, row_gate[:, None])
    f = jnp.sum(out_rows[pos], axis=1)
    y_p = (x2_p + f[:n_p]).reshape(bp, sp, D_MODEL)
    y_s = (x2_s + f[n_p:]).reshape(bs, ts, D_MODEL)

    def caches(z3, kf, lf, nseq, t):
        return (kf.reshape(1, nseq, t, FOX_HEADS, HEAD_DIM),
                z3[..., C_FV:C_FV + FOX_WIDTH].reshape(1, nseq, t, FOX_HEADS, HEAD_DIM),
                lf.reshape(1, nseq, t, FOX_HEADS))

    pk, pv, plf = caches(zp3, kf_p, lf_p, bp, sp)
    sk, sv, slf = caches(zs3, kf_s, lf_s, bs, ts)
    return (y_p, y_s, pk, pv, plf, st_p[None], _unpack_a_cols(zp3[:, -1:])[None],
            mk_p.reshape(1, bp, N_MEM, MEM_HEADS, MEM_HEAD_DIM), mv_p.reshape(1, bp, N_MEM, MEM_HEADS, MEM_HEAD_DIM),
            sk, sv, slf, st_s[None], _unpack_a_cols(zs3[:, -1:])[None])
```

```python
import functools

import jax
import jax.numpy as jnp
from jax import lax
from jax.experimental import pallas as pl
from jax.experimental.pallas import tpu as pltpu

F32 = jnp.float32
BF16 = jnp.bfloat16

D_MODEL = 2048
HEAD_DIM = 64
RWKV_WIDTH = 1024
FOX_WIDTH = 1024
RWKV_HEADS = 16
FOX_HEADS = 16
DECAY_LORA = 64
AAA_LORA = 64
GATE_LORA = 160
A_COLS = 3 * RWKV_WIDTH + DECAY_LORA + AAA_LORA + GATE_LORA
N_MEM = 256
MEM_HEADS = 4
MEM_HEAD_DIM = 128
MEM_WIDTH = 512
N_EXPERTS = 32
TOP_K = 4
D_FF = 2048
SWIGLU_LIMIT = 7.0
SWIGLU_ALPHA = 1.702
RMS_EPS = 1e-6
GN_EPS = 64e-5

LANES = 128
HEAD_PAIRS = 8

C_R, C_K, C_V = 0, 1024, 2048
C_FQ, C_FK, C_FV, C_FG = 3072, 4096, 5120, 6144
C_GL = 7168
C_WA = 7424
C_FL = 7552
Z_COLS = 7680

VMEM_LIMIT = 56 * 1024 * 1024
NEG_BIG = -1e30
LOG2E = 1.4426950408889634


def _cparams(sem):
    return pltpu.CompilerParams(dimension_semantics=sem, vmem_limit_bytes=VMEM_LIMIT)


def _block_ones():
    r = lax.broadcasted_iota(jnp.int32, (LANES, LANES), 0) // HEAD_DIM
    c = lax.broadcasted_iota(jnp.int32, (LANES, LANES), 1) // HEAD_DIM
    return (r == c).astype(BF16)


def _seg_sum64(x):
    ones = _block_ones()
    outs = []
    for s in range(x.shape[-1] // LANES):
        xs = x[:, s * LANES:(s + 1) * LANES]
        hi = xs.astype(BF16)
        lo = (xs - hi.astype(F32)).astype(BF16)
        outs.append(jnp.dot(hi, ones, preferred_element_type=F32)
                    + jnp.dot(lo, ones, preferred_element_type=F32))
    return outs[0] if len(outs) == 1 else jnp.concatenate(outs, axis=-1)


def _softplus(x):
    return jnp.maximum(x, 0.0) + jnp.log(1.0 + jnp.exp(-jnp.abs(x)))


def _sigmoid(x):
    return 1.0 / (1.0 + jnp.exp(-x))


def _norm_mm_kernel(x_ref, g_ref, w_ref, o_ref, xn_ref):
    @pl.when(pl.program_id(1) == 0)
    def _():
        x = x_ref[...]
        ms = jnp.mean(x * x, axis=-1, keepdims=True)
        xn_ref[...] = (x * lax.rsqrt(ms + RMS_EPS) * g_ref[...]).astype(BF16)

    o_ref[...] = jnp.dot(xn_ref[...], w_ref[...], preferred_element_type=F32)


def _norm_mm(x, g, w, tm, tn):
    m, k = x.shape
    n = w.shape[1]
    return pl.pallas_call(
        _norm_mm_kernel,
        grid=(m // tm, n // tn),
        in_specs=[pl.BlockSpec((tm, k), lambda i, j: (i, 0)),
                  pl.BlockSpec((1, k), lambda i, j: (0, 0)),
                  pl.BlockSpec((k, tn), lambda i, j: (0, j))],
        out_specs=pl.BlockSpec((tm, tn), lambda i, j: (i, j)),
        out_shape=jax.ShapeDtypeStruct((m, n), F32),
        scratch_shapes=[pltpu.VMEM((tm, k), BF16)],
        compiler_params=_cparams(("parallel", "arbitrary")),
        name="norm_mm",
    )(x, g, w)


def _rwkv_prep_kernel(zm_ref, zg_ref, zw_ref, pm_ref, pg_ref, pw_ref,
                      mum_ref, mug_ref, muw_ref, w0_ref, w2_ref, a0_ref, a2_ref, g2_ref,
                      kk_ref, ka_ref, rk_ref,
                      r_o, d_o, k_o, v_o, kk_o, b_o, g_o, bonus_o):
    def shifted(z_ref, p_ref, mu_ref):
        z = z_ref[...]
        rolled = pltpu.roll(z, 1, 0)
        row = lax.broadcasted_iota(jnp.int32, z.shape, 0)
        prev = jnp.where(row == 0, p_ref[0], rolled)
        return z + mu_ref[...] * (prev - z)

    zm = shifted(zm_ref, pm_ref, mum_ref)
    zg = shifted(zg_ref, pg_ref, mug_ref)
    zw = shifted(zw_ref, pw_ref, muw_ref)
    r = zm[:, C_R:C_R + RWKV_WIDTH]
    k = zm[:, C_K:C_K + RWKV_WIDTH]
    v = zm[:, C_V:C_V + RWKV_WIDTH]
    lw = jnp.dot(jnp.tanh(zw).astype(BF16), w2_ref[...], preferred_element_type=F32)
    la = jnp.dot(zw.astype(BF16), a2_ref[...], preferred_element_type=F32)
    g = jnp.dot(_sigmoid(zg).astype(BF16), g2_ref[...], preferred_element_type=F32)
    w_log = -_softplus(-(w0_ref[...] + lw)) - 0.5
    a = _sigmoid(a0_ref[...] + la)
    kk = k * kk_ref[...]
    kk = kk * lax.rsqrt(jnp.maximum(_seg_sum64(kk * kk), 1e-24))
    kh = k * (1.0 + (a - 1.0) * ka_ref[...])
    tiles = lambda x: x.reshape(x.shape[0], HEAD_PAIRS, LANES)
    r_o[...] = tiles(r)
    d_o[...] = tiles(jnp.exp(-jnp.exp(w_log)))
    k_o[...] = tiles(kh)
    v_o[...] = tiles(v)
    kk_o[...] = tiles(kk)
    b_o[...] = tiles(kk * a)
    g_o[...] = g
    bonus_o[...] = _seg_sum64(r * kh * rk_ref[...]) * v


def _rwkv_prep(z, prev_m, prev_g, prev_w, pr, tm):
    n = z.shape[0]
    row = lambda w, c: pl.BlockSpec((tm, w), lambda i, c=c: (i, c))
    prev = lambda w: pl.BlockSpec((1, 1, w), lambda i: (i, 0, 0))
    full = lambda a: pl.BlockSpec(a.shape, lambda i: (0,) * a.ndim)
    params = [pr["mu_m"], pr["mu_g"], pr["mu_w"], pr["w0"], pr["w2"], pr["a0"], pr["a2"], pr["g2"],
              pr["k_k"], pr["k_a"], pr["r_k"]]
    tiled = jax.ShapeDtypeStruct((n, HEAD_PAIRS, LANES), F32)
    flat = jax.ShapeDtypeStruct((n, RWKV_WIDTH), F32)
    return pl.pallas_call(
        _rwkv_prep_kernel,
        grid=(n // tm,),
        in_specs=[row(3072, 0), row(256, C_GL // 256), row(128, C_WA // 128),
                  prev(3072), prev(256), prev(128)] + [full(a) for a in params],
        out_specs=[pl.BlockSpec((tm, HEAD_PAIRS, LANES), lambda i: (i, 0, 0))] * 6
        + [pl.BlockSpec((tm, RWKV_WIDTH), lambda i: (i, 0))] * 2,
        out_shape=[tiled] * 6 + [flat] * 2,
        compiler_params=_cparams(("parallel",)),
        name="rwkv_prep",
    )(z, z, z, prev_m, prev_g, prev_w, *params)


SEQ_PER_STEP = 2
SCAN_BLOCK = 128


def _rwkv_scan_kernel(r_ref, d_ref, k_ref, v_ref, kk_ref, b_ref, g_ref, bonus_ref, s0_ref,
                      lw_ref, lb_ref,
                      y_ref, sf_ref,
                      s_ref, oa_ref, u_ref, *, tb):
    tblk = pl.program_id(1)

    @pl.when(tblk == 0)
    def _():
        s_ref[...] = s0_ref[...]

    oa_ref[...] = jnp.zeros_like(oa_ref)
    ones = _block_ones()
    lane = lax.broadcasted_iota(jnp.int32, (HEAD_DIM, LANES), 1)
    sub = lax.broadcasted_iota(jnp.int32, (HEAD_DIM, LANES), 0)
    diag = (lane % HEAD_DIM) == sub

    def bcast(ref, s, t, hp):
        return jnp.broadcast_to(ref[s, t, hp:hp + 1, :], (HEAD_DIM, LANES))

    def step(t, carry):
        tt = t % HEAD_DIM
        half = t // HEAD_DIM
        for s in range(SEQ_PER_STEP):
            for hp in range(HEAD_PAIRS):
                i = s * HEAD_PAIRS + hp
                st = s_ref[s, hp]
                u_ref[0, i * HEAD_DIM:(i + 1) * HEAD_DIM, :] = (st * bcast(kk_ref, s, t, hp)).astype(BF16)
                u_ref[1, i * HEAD_DIM:(i + 1) * HEAD_DIM, :] = jnp.where(
                    diag, bcast(v_ref, s, t, hp), 0.0).astype(BF16)
        skk = jnp.dot(u_ref[0], ones, preferred_element_type=F32)
        vb = jnp.dot(u_ref[1], ones, preferred_element_type=F32)
        for s in range(SEQ_PER_STEP):
            for hp in range(HEAD_PAIRS):
                i = s * HEAD_PAIRS + hp
                rows = slice(i * HEAD_DIM, (i + 1) * HEAD_DIM)
                st = (s_ref[s, hp] * bcast(d_ref, s, t, hp)
                      - skk[rows] * bcast(b_ref, s, t, hp)
                      + vb[rows] * bcast(k_ref, s, t, hp))
                s_ref[s, hp] = st
                u_ref[2, rows, :] = (st * bcast(r_ref, s, t, hp)).astype(BF16)
        ob = jnp.dot(u_ref[2], ones, preferred_element_type=F32)
        hit = (lane % HEAD_DIM) == tt
        for s in range(SEQ_PER_STEP):
            for hp in range(HEAD_PAIRS):
                i = s * HEAD_PAIRS + hp
                rows = slice(i * HEAD_DIM, (i + 1) * HEAD_DIM)
                cur = oa_ref[s, hp, pl.ds(half * HEAD_DIM, HEAD_DIM), :]
                oa_ref[s, hp, pl.ds(half * HEAD_DIM, HEAD_DIM), :] = jnp.where(hit, ob[rows], cur)
        return carry

    lax.fori_loop(0, tb, step, 0)

    lane_t = lax.broadcasted_iota(jnp.int32, (HEAD_DIM, LANES), 1)
    low = lane_t < HEAD_DIM
    for s in range(SEQ_PER_STEP):
        slabs = []
        for hp in range(HEAD_PAIRS):
            mt = oa_ref[s, hp].T
            ro = pltpu.roll(mt, HEAD_DIM, 1)
            top = jnp.where(low, mt[:HEAD_DIM], ro[HEAD_DIM:])
            bot = jnp.where(low, ro[:HEAD_DIM], mt[HEAD_DIM:])
            slabs.append(jnp.concatenate([top, bot], axis=0)[:tb])
        o = jnp.concatenate(slabs, axis=-1)
        mean = _seg_sum64(o) * (1.0 / HEAD_DIM)
        cen = o - mean
        var = _seg_sum64(cen * cen) * (1.0 / HEAD_DIM)
        on = cen * lax.rsqrt(var + GN_EPS) * lw_ref[...] + lb_ref[...]
        y_ref[s] = ((on + bonus_ref[s]) * g_ref[s]).astype(y_ref.dtype)

    @pl.when(tblk == pl.num_programs(1) - 1)
    def _():
        sf_ref[...] = s_ref[...]


def _rwkv_scan(steps, g, bonus, s0, lnw, lnb, tb):
    nseq, t = g.shape[:2]
    step_spec = pl.BlockSpec((SEQ_PER_STEP, tb, HEAD_PAIRS, LANES), lambda i, j: (i, j, 0, 0))
    seq_spec = pl.BlockSpec((SEQ_PER_STEP, tb, RWKV_WIDTH), lambda i, j: (i, j, 0))
    st_spec = pl.BlockSpec((SEQ_PER_STEP, HEAD_PAIRS, HEAD_DIM, LANES), lambda i, j: (i, 0, 0, 0))
    par_spec = pl.BlockSpec((1, RWKV_WIDTH), lambda i, j: (0, 0))
    nrows = SEQ_PER_STEP * HEAD_PAIRS * HEAD_DIM
    return pl.pallas_call(
        functools.partial(_rwkv_scan_kernel, tb=tb),
        grid=(nseq // SEQ_PER_STEP, t // tb),
        in_specs=[step_spec] * 6 + [seq_spec] * 2 + [st_spec] + [par_spec] * 2,
        out_specs=[seq_spec, st_spec],
        out_shape=[jax.ShapeDtypeStruct((nseq, t, RWKV_WIDTH), BF16),
                   jax.ShapeDtypeStruct(s0.shape, F32)],
        scratch_shapes=[pltpu.VMEM((SEQ_PER_STEP, HEAD_PAIRS, HEAD_DIM, LANES), F32),
                        pltpu.VMEM((SEQ_PER_STEP, HEAD_PAIRS, LANES, LANES), F32),
                        pltpu.VMEM((3, nrows, LANES), BF16)],
        compiler_params=_cparams(("parallel", "arbitrary")),
        name="rwkv_scan",
    )(*steps, g, bonus, s0, lnw, lnb)


def _state_to_pairs(s):
    b = s.shape[0]
    return s.reshape(b, HEAD_PAIRS, 2, HEAD_DIM, HEAD_DIM).transpose(0, 1, 3, 2, 4).reshape(
        b, HEAD_PAIRS, HEAD_DIM, LANES)


def _pairs_to_state(s):
    b = s.shape[0]
    return s.reshape(b, HEAD_PAIRS, HEAD_DIM, 2, HEAD_DIM).transpose(0, 1, 3, 2, 4).reshape(
        b, RWKV_HEADS, HEAD_DIM, HEAD_DIM)


def _fox_proj_kernel(q_ref, k_ref, fl_ref, qg_ref, kg_ref, bf_ref, qb_o, kb_o, kf_o, lf_o):
    def headnorm(x, g):
        ms = _seg_sum64(x * x) * (1.0 / HEAD_DIM)
        return x * lax.rsqrt(ms + RMS_EPS) * g

    q = headnorm(q_ref[...], qg_ref[...])
    k = headnorm(k_ref[...], kg_ref[...])
    qb_o[...] = (q * (HEAD_DIM ** -0.5 * LOG2E)).astype(BF16)
    kb_o[...] = k.astype(BF16)
    kf_o[...] = k
    lf_o[...] = -_softplus(-(fl_ref[...] + bf_ref[...]))


def _fox_proj(z, qg, kg, bf, tm):
    n = z.shape[0]
    col = lambda w, c: pl.BlockSpec((tm, w), lambda i, c=c: (i, c))
    par = lambda w: pl.BlockSpec((1, w), lambda i: (0, 0))
    wide = pl.BlockSpec((tm, FOX_WIDTH), lambda i: (i, 0))
    return pl.pallas_call(
        _fox_proj_kernel,
        grid=(n // tm,),
        in_specs=[col(1024, C_FQ // 1024), col(1024, C_FK // 1024), col(128, C_FL // 128),
                  par(1024), par(1024), par(128)],
        out_specs=[wide, wide, wide, pl.BlockSpec((tm, LANES), lambda i: (i, 0))],
        out_shape=[jax.ShapeDtypeStruct((n, FOX_WIDTH), BF16), jax.ShapeDtypeStruct((n, FOX_WIDTH), BF16),
                   jax.ShapeDtypeStruct((n, FOX_WIDTH), F32), jax.ShapeDtypeStruct((n, LANES), F32)],
        compiler_params=_cparams(("parallel",)),
        name="fox_proj",
    )(z, z, z, qg, kg, bf)


def _attn_update(h, s, v, m_ref, l_ref, acc_ref):
    m_prev = m_ref[h]
    m_new = jnp.maximum(m_prev, jnp.max(s, axis=-1, keepdims=True))
    alpha = jnp.exp2(m_prev - m_new)
    p = jnp.exp2(s - m_new)
    l_ref[h] = alpha * l_ref[h] + jnp.sum(p, axis=-1, keepdims=True)
    acc_ref[h] = alpha * acc_ref[h] + jnp.dot(p.astype(BF16), v, preferred_element_type=F32)
    m_ref[h] = m_new


def _attn_init(q_ref, qm_ref, m_ref, l_ref, acc_ref):
    q = q_ref[...]
    lane = lax.broadcasted_iota(jnp.int32, q.shape, 1)
    zero = jnp.zeros_like(q)
    qm_ref[0] = jnp.where(lane < HEAD_DIM, q, zero)
    qm_ref[1] = jnp.where(lane >= HEAD_DIM, q, zero)
    m_ref[...] = jnp.full_like(m_ref, NEG_BIG)
    l_ref[...] = jnp.zeros_like(l_ref)
    acc_ref[...] = jnp.zeros_like(acc_ref)


def _attn_finish(gate_ref, y_ref, l_ref, acc_ref):
    lane = lax.broadcasted_iota(jnp.int32, acc_ref.shape[1:], 1)
    o = jnp.where(lane < HEAD_DIM, acc_ref[0] / l_ref[0], acc_ref[1] / l_ref[1])
    y_ref[...] = (o * _sigmoid(gate_ref[...])).astype(y_ref.dtype)


ATTN_STRIP = 256


def _fox_prompt_kernel(qi_ref, kj_ref, q_ref, k_ref, vt_ref, gate_ref, y_ref, m_ref, l_ref, acc_ref, *, tq):
    pair = pl.program_id(2)
    qi = qi_ref[pair]
    kj = kj_ref[pair]

    @pl.when(kj == 0)
    def _():
        m_ref[...] = jnp.full_like(m_ref, NEG_BIG)
        l_ref[...] = jnp.zeros_like(l_ref)
        acc_ref[...] = jnp.zeros_like(acc_ref)

    def body(masked):
        for h in range(2):
            k = k_ref[0, h]
            vt = vt_ref[0, h]
            for c in range(tq // ATTN_STRIP):
                cols = slice(c * ATTN_STRIP, (c + 1) * ATTN_STRIP)
                st = lax.dot_general(k, q_ref[0, h, cols, :], (((1,), (1,)), ((), ())),
                                     preferred_element_type=F32)
                if masked:
                    key = lax.broadcasted_iota(jnp.int32, st.shape, 0)
                    qry = lax.broadcasted_iota(jnp.int32, st.shape, 1) + c * ATTN_STRIP
                    st = jnp.where(key <= qry, st, NEG_BIG)
                m_prev = m_ref[h, :, cols]
                m_new = jnp.maximum(m_prev, jnp.max(st, axis=0, keepdims=True))
                alpha = jnp.exp2(m_prev - m_new)
                p = jnp.exp2(st - m_new)
                l_ref[h, :, cols] = alpha * l_ref[h, :, cols] + jnp.sum(p, axis=0, keepdims=True)
                acc_ref[h, :, cols] = alpha * acc_ref[h, :, cols] + jnp.dot(
                    vt, p.astype(BF16), preferred_element_type=F32)
                m_ref[h, :, cols] = m_new

    @pl.when(kj < qi)
    def _():
        body(False)

    @pl.when(kj == qi)
    def _():
        body(True)
        ot = jnp.concatenate([acc_ref[0] / l_ref[0], acc_ref[1] / l_ref[1]], axis=0)
        y_ref[...] = (ot.T * _sigmoid(gate_ref[...])).astype(y_ref.dtype)


def _fox_prompt(q_aug, k_aug, vt, z, nb, seq, tq):
    nq = seq // tq
    pairs = [(i, j) for i in range(nq) for j in range(i + 1)]
    qi_tab = jnp.asarray([p[0] for p in pairs], jnp.int32)
    kj_tab = jnp.asarray([p[1] for p in pairs], jnp.int32)
    grid_spec = pltpu.PrefetchScalarGridSpec(
        num_scalar_prefetch=2,
        grid=(nb, HEAD_PAIRS, len(pairs)),
        in_specs=[pl.BlockSpec((1, 2, tq, LANES), lambda b, hp, p, qi, kj: (b, hp, qi[p], 0)),
                  pl.BlockSpec((1, 2, tq, LANES), lambda b, hp, p, qi, kj: (b, hp, kj[p], 0)),
                  pl.BlockSpec((1, 2, HEAD_DIM, tq), lambda b, hp, p, qi, kj: (b, hp, 0, kj[p])),
                  pl.BlockSpec((tq, LANES), lambda b, hp, p, qi, kj: (b * nq + qi[p], C_FG // LANES + hp))],
        out_specs=pl.BlockSpec((tq, LANES), lambda b, hp, p, qi, kj: (b * nq + qi[p], hp)),
        scratch_shapes=[pltpu.VMEM((2, 1, tq), F32), pltpu.VMEM((2, 1, tq), F32),
                        pltpu.VMEM((2, HEAD_DIM, tq), F32)],
    )
    return pl.pallas_call(
        functools.partial(_fox_prompt_kernel, tq=tq),
        grid_spec=grid_spec,
        out_shape=jax.ShapeDtypeStruct((nb * seq, FOX_WIDTH), BF16),
        compiler_params=_cparams(("parallel", "parallel", "arbitrary")),
        name="fox_prompt",
    )(qi_tab, kj_tab, q_aug, k_aug, vt, z)


def _split3(x):
    def top(v):
        bits = lax.bitcast_convert_type(v, jnp.uint32) & jnp.uint32(0xFFFF0000)
        return lax.bitcast_convert_type(bits, F32)

    hi = top(x)
    mid = top(x - hi)
    lo = top(x - hi - mid)
    return hi.astype(BF16), mid.astype(BF16), lo.astype(BF16)


def _augment(qb, kb, vz, f2, nb, seq):
    heads = lambda a: a.reshape(nb, seq, FOX_HEADS, HEAD_DIM).transpose(0, 2, 1, 3)
    fh = f2.transpose(0, 2, 1)[..., None]
    one = jnp.ones_like(fh, BF16)
    pad = jnp.zeros((nb, FOX_HEADS, seq, LANES - HEAD_DIM - 6), BF16)
    fq3 = _split3(fh)
    fk3 = _split3(-fh)
    q_aug = jnp.concatenate([heads(qb), *fq3, one, one, one, pad], axis=-1)
    k_aug = jnp.concatenate([heads(kb), one, one, one, *fk3, pad], axis=-1)
    vt = vz.astype(BF16).reshape(nb, seq, FOX_HEADS, HEAD_DIM).transpose(0, 2, 3, 1)
    return q_aug, k_aug, vt


def _fox_sample_kernel(q_ref, kc_ref, vc_ref, kn_ref, vn_ref, fq_ref, fkc_ref, fkn_ref, gate_ref, y_ref,
                       qm_ref, m_ref, l_ref, acc_ref, *, nk):
    kj = pl.program_id(2)

    @pl.when(kj == 0)
    def _():
        _attn_init(q_ref, qm_ref, m_ref, l_ref, acc_ref)

    @pl.when(kj < nk)
    def _():
        k = kc_ref[0].astype(BF16)
        v = vc_ref[0].astype(BF16)
        for h in range(2):
            s = lax.dot_general(qm_ref[h], k, (((1,), (1,)), ((), ())), preferred_element_type=F32)
            s = s + fq_ref[0, 0, :, h:h + 1] - fkc_ref[0, 0, h:h + 1, :]
            _attn_update(h, s, v, m_ref, l_ref, acc_ref)

    @pl.when(kj == nk)
    def _():
        k = kn_ref[...]
        v = vn_ref[...].astype(BF16)
        for h in range(2):
            s = lax.dot_general(qm_ref[h], k, (((1,), (1,)), ((), ())), preferred_element_type=F32)
            s = s + fq_ref[0, 0, :, h:h + 1] - fkn_ref[0, 0, h:h + 1, :]
            row = lax.broadcasted_iota(jnp.int32, s.shape, 0)
            col = lax.broadcasted_iota(jnp.int32, s.shape, 1)
            s = jnp.where(col <= row, s, NEG_BIG)
            _attn_update(h, s, v, m_ref, l_ref, acc_ref)
        _attn_finish(gate_ref, y_ref, l_ref, acc_ref)


def _fox_sample(qb, kc, vc, kb, z, fq, fkc, fkn, nb, t, tk):
    past = kc.shape[1]
    nk = past // tk
    last = nk - 1
    new = pl.BlockSpec((t, LANES), lambda b, hp, j: (b, hp))
    cache = pl.BlockSpec((1, tk, LANES), lambda b, hp, j: (b, jnp.minimum(j, last), hp))
    return pl.pallas_call(
        functools.partial(_fox_sample_kernel, nk=nk),
        grid=(nb, HEAD_PAIRS, nk + 1),
        in_specs=[new, cache, cache, new,
                  pl.BlockSpec((t, LANES), lambda b, hp, j: (b, C_FV // LANES + hp)),
                  pl.BlockSpec((1, 1, t, 2), lambda b, hp, j: (b, hp, 0, 0)),
                  pl.BlockSpec((1, 1, 2, tk), lambda b, hp, j: (b, hp, 0, jnp.minimum(j, last))),
                  pl.BlockSpec((1, 1, 2, t), lambda b, hp, j: (b, hp, 0, 0)),
                  pl.BlockSpec((t, LANES), lambda b, hp, j: (b, C_FG // LANES + hp))],
        out_specs=new,
        out_shape=jax.ShapeDtypeStruct((nb * t, FOX_WIDTH), BF16),
        scratch_shapes=[pltpu.VMEM((2, t, LANES), BF16), pltpu.VMEM((2, t, 1), F32),
                        pltpu.VMEM((2, t, 1), F32), pltpu.VMEM((2, t, LANES), F32)],
        compiler_params=_cparams(("parallel", "parallel", "arbitrary")),
        name="fox_sample",
    )(qb, kc, vc, kb, z, fq, fkc, fkn, z)


def _out_proj_kernel(x_ref, ya_ref, yb_ref, wa_ref, wb_ref, o_ref):
    o_ref[...] = (x_ref[...]
                  + jnp.dot(ya_ref[...], wa_ref[...], preferred_element_type=F32)
                  + jnp.dot(yb_ref[...], wb_ref[...], preferred_element_type=F32))


def _out_proj(x, ya, yb, w, tm):
    n = x.shape[0]
    return pl.pallas_call(
        _out_proj_kernel,
        grid=(n // tm,),
        in_specs=[pl.BlockSpec((tm, D_MODEL), lambda i: (i, 0)),
                  pl.BlockSpec((tm, RWKV_WIDTH), lambda i: (i, 0)),
                  pl.BlockSpec((tm, FOX_WIDTH), lambda i: (i, 0)),
                  pl.BlockSpec((RWKV_WIDTH, D_MODEL), lambda i: (0, 0)),
                  pl.BlockSpec((FOX_WIDTH, D_MODEL), lambda i: (1, 0))],
        out_specs=pl.BlockSpec((tm, D_MODEL), lambda i: (i, 0)),
        out_shape=jax.ShapeDtypeStruct((n, D_MODEL), F32),
        compiler_params=_cparams(("parallel",)),
        name="out_proj",
    )(x, ya, yb, w, w)


def _mem_kv_kernel(x_ref, g_ref, w_ref, kg_ref, o_ref):
    x = x_ref[...]
    ms = jnp.mean(x * x, axis=-1, keepdims=True)
    xn = (x * lax.rsqrt(ms + RMS_EPS) * g_ref[...]).astype(BF16)
    kv = jnp.dot(xn, w_ref[...], preferred_element_type=F32)
    for h in range(MEM_HEADS):
        kh = kv[:, h * MEM_HEAD_DIM:(h + 1) * MEM_HEAD_DIM]
        ms = jnp.mean(kh * kh, axis=-1, keepdims=True)
        o_ref[:, h * MEM_HEAD_DIM:(h + 1) * MEM_HEAD_DIM] = kh * lax.rsqrt(ms + RMS_EPS) * kg_ref[...]
    o_ref[:, MEM_WIDTH:] = kv[:, MEM_WIDTH:]


def _mem_kv(mem, g, w_kv, kg, tm):
    n = mem.shape[0]
    return pl.pallas_call(
        _mem_kv_kernel,
        grid=(n // tm,),
        in_specs=[pl.BlockSpec((tm, D_MODEL), lambda i: (i, 0)),
                  pl.BlockSpec((1, D_MODEL), lambda i: (0, 0)),
                  pl.BlockSpec((D_MODEL, 2 * MEM_WIDTH), lambda i: (0, 0)),
                  pl.BlockSpec((1, MEM_HEAD_DIM), lambda i: (0, 0))],
        out_specs=pl.BlockSpec((tm, 2 * MEM_WIDTH), lambda i: (i, 0)),
        out_shape=jax.ShapeDtypeStruct((n, 2 * MEM_WIDTH), F32),
        compiler_params=_cparams(("parallel",)),
        name="mem_kv",
    )(mem, g, w_kv, kg)


def _cross_router_kernel(x_ref, gc_ref, wq_ref, qg_ref, mk_ref, mv_ref, wo_ref,
                         gf_ref, wr_ref, br_ref,
                         x2_ref, h_ref, idx_ref, gate_ref):
    x = x_ref[...]
    ms = jnp.mean(x * x, axis=-1, keepdims=True)
    xn = (x * lax.rsqrt(ms + RMS_EPS) * gc_ref[...]).astype(BF16)
    q = jnp.dot(xn, wq_ref[...], preferred_element_type=F32)
    outs = []
    for h in range(MEM_HEADS):
        sl = slice(h * MEM_HEAD_DIM, (h + 1) * MEM_HEAD_DIM)
        qh = q[:, sl]
        qms = jnp.mean(qh * qh, axis=-1, keepdims=True)
        qh = (qh * lax.rsqrt(qms + RMS_EPS) * qg_ref[...]).astype(BF16)
        s = lax.dot_general(qh, mk_ref[0, :, sl], (((1,), (1,)), ((), ())), preferred_element_type=F32)
        s = s * (MEM_HEAD_DIM ** -0.5)
        p = jnp.exp(s - jnp.max(s, axis=-1, keepdims=True))
        p = p / jnp.sum(p, axis=-1, keepdims=True)
        outs.append(jnp.dot(p.astype(BF16), mv_ref[0, :, sl], preferred_element_type=F32))
    o = jnp.concatenate(outs, axis=-1).astype(BF16)
    x2 = x + jnp.dot(o, wo_ref[...], preferred_element_type=F32)
    x2_ref[...] = x2

    ms2 = jnp.mean(x2 * x2, axis=-1, keepdims=True)
    hn = x2 * lax.rsqrt(ms2 + RMS_EPS) * gf_ref[...]
    h_ref[...] = hn.astype(BF16)
    logits = jnp.dot(hn, wr_ref[...], preferred_element_type=F32, precision=lax.Precision.HIGHEST)
    logits = logits + br_ref[...]
    lane = lax.broadcasted_iota(jnp.int32, logits.shape, 1)
    idx_acc = jnp.zeros(logits.shape, jnp.int32)
    val_acc = jnp.zeros(logits.shape, F32)
    top = None
    for kk in range(TOP_K):
        m = jnp.max(logits, axis=-1, keepdims=True)
        sel = jnp.min(jnp.where(logits == m, lane, LANES), axis=-1, keepdims=True)
        if kk == 0:
            top = m
        idx_acc = jnp.where(lane == kk, sel, idx_acc)
        val_acc = jnp.where(lane == kk, jnp.exp(m - top), val_acc)
        logits = jnp.where(lane == sel, -jnp.inf, logits)
    idx_ref[...] = idx_acc
    gate_ref[...] = val_acc / jnp.sum(val_acc, axis=-1, keepdims=True)


def _cross_router(x, mk, mv, pc, nb, tm):
    n = x.shape[0]
    per = n // nb // tm
    full = lambda a: pl.BlockSpec(a.shape, lambda i: (0,) * a.ndim)
    mem = pl.BlockSpec((1, N_MEM, MEM_WIDTH), lambda i: (i // per, 0, 0))
    row = lambda w: pl.BlockSpec((tm, w), lambda i: (i, 0))
    params1 = [pc["g_cross"], pc["w_xq"], pc["xq_gain"]]
    params2 = [pc["w_xo"], pc["g_ffn"], pc["w_router"], pc["b_router"]]
    return pl.pallas_call(
        _cross_router_kernel,
        grid=(n // tm,),
        in_specs=[row(D_MODEL)] + [full(a) for a in params1] + [mem, mem] + [full(a) for a in params2],
        out_specs=[row(D_MODEL), row(D_MODEL), row(LANES), row(LANES)],
        out_shape=[jax.ShapeDtypeStruct((n, D_MODEL), F32), jax.ShapeDtypeStruct((n, D_MODEL), BF16),
                   jax.ShapeDtypeStruct((n, LANES), jnp.int32), jax.ShapeDtypeStruct((n, LANES), F32)],
        compiler_params=_cparams(("parallel",)),
        name="cross_router",
    )(x, *params1, mk, mv, *params2)


MOE_TM = 512
MOE_TF = 128


def _moe_kernel(te_ref, nu_ref, xs_ref, wgu_ref, bgu_ref, wd_ref, bd_ref, o_ref):
    i = pl.program_id(0)
    f = pl.program_id(1)
    half = 2 * MOE_TF

    @pl.when(i < nu_ref[0])
    def _():
        gu = jnp.dot(xs_ref[...], wgu_ref[0].astype(BF16), preferred_element_type=F32) + bgu_ref[0]
        glu = jnp.minimum(gu, SWIGLU_LIMIT)
        fglu = glu * _sigmoid(SWIGLU_ALPHA * glu)
        lin = jnp.clip(gu, -SWIGLU_LIMIT, SWIGLU_LIMIT) + 1.0
        act_a = pltpu.roll(fglu[:, :half], 1, 1) * lin[:, :half]
        act_b = fglu[:, half:] * pltpu.roll(lin[:, half:], half - 1, 1)
        lane = lax.broadcasted_iota(jnp.int32, act_a.shape, 1)
        merged = jnp.where(lane % 2 == 0, act_b, act_a).astype(BF16)
        top = lambda w: lax.bitcast_convert_type(w.astype(BF16).astype(F32), jnp.uint32)
        words = (top(wd_ref[0, MOE_TF:, :]) >> 16) | top(wd_ref[0, :MOE_TF, :])
        part = jnp.dot(merged, pltpu.bitcast(words, BF16), preferred_element_type=F32)

        @pl.when(f == 0)
        def _():
            o_ref[...] = part + bd_ref[0]

        @pl.when(f > 0)
        def _():
            o_ref[...] += part

    @pl.when(jnp.logical_and(i >= nu_ref[0], f == 0))
    def _():
        o_ref[...] = jnp.zeros_like(o_ref)


def _moe_experts(tile_expert, n_used, xs, w_gu, b_gu, w_down, b_down):
    p = xs.shape[0]
    nt = p // MOE_TM
    nf = D_FF // (2 * MOE_TF)
    fi = lambda i, f, nu: jnp.where(i < nu[0], f, nf - 1)
    grid_spec = pltpu.PrefetchScalarGridSpec(
        num_scalar_prefetch=2,
        grid=(nt, nf),
        in_specs=[pl.BlockSpec((MOE_TM, D_MODEL), lambda i, f, te, nu: (i, 0)),
                  pl.BlockSpec((1, D_MODEL, 4 * MOE_TF), lambda i, f, te, nu: (te[i], 0, fi(i, f, nu))),
                  pl.BlockSpec((1, 1, 4 * MOE_TF), lambda i, f, te, nu: (te[i], 0, fi(i, f, nu))),
                  pl.BlockSpec((1, 2 * MOE_TF, D_MODEL), lambda i, f, te, nu: (te[i], fi(i, f, nu), 0)),
                  pl.BlockSpec((1, 1, D_MODEL), lambda i, f, te, nu: (te[i], 0, 0))],
        out_specs=pl.BlockSpec((MOE_TM, D_MODEL), lambda i, f, te, nu: (i, 0)),
    )
    return pl.pallas_call(
        _moe_kernel,
        grid_spec=grid_spec,
        out_shape=jax.ShapeDtypeStruct((p, D_MODEL), F32),
        compiler_params=_cparams(("arbitrary", "arbitrary")),
        name="moe_experts",
    )(tile_expert, n_used, xs, w_gu, b_gu, w_down, b_down)


def _route(top_i):
    n = top_i.shape[0]
    na = n * TOP_K
    flat_e = top_i.reshape(na)
    order = jnp.argsort(flat_e, stable=True).astype(jnp.int32)
    inv = jnp.argsort(order).astype(jnp.int32)
    counts = jnp.sum((flat_e[:, None] == jnp.arange(N_EXPERTS, dtype=jnp.int32)[None, :]).astype(jnp.int32),
                     axis=0)
    padded = ((counts + MOE_TM - 1) // MOE_TM) * MOE_TM
    pad_start = jnp.cumsum(padded) - padded
    start = jnp.cumsum(counts) - counts
    pos = (inv + (pad_start - start)[flat_e]).reshape(n, TOP_K)
    p_rows = ((na + N_EXPERTS * (MOE_TM - 1)) // MOE_TM + 1) * MOE_TM
    nt = p_rows // MOE_TM
    tile_start = jnp.arange(nt, dtype=jnp.int32) * MOE_TM
    pad_end = pad_start + padded
    n_used = (jnp.sum(padded) // MOE_TM).astype(jnp.int32)
    tile_expert = jnp.minimum(jnp.sum((tile_start[:, None] >= pad_end[None, :]).astype(jnp.int32), axis=1),
                              N_EXPERTS - 1)
    tile_expert = jnp.where(jnp.arange(nt) < n_used, tile_expert, tile_expert[jnp.maximum(n_used - 1, 0)])
    row = jnp.arange(p_rows, dtype=jnp.int32)
    row_e = jnp.repeat(tile_expert, MOE_TM)
    rank = row - pad_start[row_e]
    src = jnp.clip(start[row_e] + rank, 0, na - 1)
    row_token = jnp.where(rank < counts[row_e], order[src] // TOP_K, 0)
    return row_token, pos, tile_expert.astype(jnp.int32), n_used.reshape(1)


def _pad_cols(a, width):
    return jnp.pad(a, ((0, 0), (0, width - a.shape[1])))


def _pack_cols(a):
    o_w = 3 * RWKV_WIDTH
    o_a = o_w + DECAY_LORA
    o_g = o_a + AAA_LORA
    fb = A_COLS
    return jnp.concatenate([
        a[:, 0:3 * RWKV_WIDTH],
        a[:, fb:fb + 4 * FOX_WIDTH],
        _pad_cols(a[:, o_g:A_COLS], 256),
        a[:, o_w:o_g],
        _pad_cols(a[:, fb + 4 * FOX_WIDTH:], LANES),
    ], axis=1)


def _unpack_a_cols(z):
    return jnp.concatenate([z[..., 0:3 * RWKV_WIDTH], z[..., C_WA:C_WA + 128], z[..., C_GL:C_GL + GATE_LORA]],
                           axis=-1)


def _prev_rows(z, tm, first):
    nseq, t, _ = z.shape
    if t > tm:
        inner = z[:, tm - 1:t - 1:tm]
        rows = jnp.concatenate([first, inner], axis=1)
    else:
        rows = first
    rows = rows.reshape(-1, 1, Z_COLS)
    return rows[..., 0:3072], rows[..., C_GL:C_GL + 256], rows[..., C_WA:C_WA + 128]


def kernel(x_prompt, x_sample, mem_prompt, cache_fox_k, cache_fox_v, cache_fox_logf, state_rwkv, state_shift,
           cache_mem_k, cache_mem_v, norm_mix, w_in, rwkv_mu, rwkv_w0, rwkv_w2, rwkv_a0, rwkv_a2, rwkv_g2,
           rwkv_k_k, rwkv_k_a, rwkv_r_k, rwkv_lnx_w, rwkv_lnx_b, fox_b_f, fox_q_norm, fox_k_norm, w_out,
           norm_cross, norm_mem, w_xq, w_xk, w_xv, w_xo, xq_norm, xk_norm, norm_ffn, w_router, b_router,
           w_gu, b_gu, w_down, b_down):
    bp, sp, _ = x_prompt.shape
    bs, ts, _ = x_sample.shape
    past = cache_fox_k.shape[2]
    n_p, n_s = bp * sp, bs * ts
    l = 0

    w_in_p = _pack_cols(w_in[l]).astype(BF16)
    mu_p = _pack_cols(jnp.pad(rwkv_mu[l][None, :], ((0, 0), (0, w_in.shape[2] - A_COLS))))
    zero_lora = jnp.zeros((DECAY_LORA, RWKV_WIDTH), F32)
    prep = {
        "mu_m": mu_p[:, 0:3072], "mu_g": mu_p[:, C_GL:C_GL + 256], "mu_w": mu_p[:, C_WA:C_WA + 128],
        "w0": rwkv_w0[l][None], "a0": rwkv_a0[l][None],
        "w2": jnp.concatenate([rwkv_w2[l], zero_lora], axis=0).astype(BF16),
        "a2": jnp.concatenate([zero_lora, rwkv_a2[l]], axis=0).astype(BF16),
        "g2": jnp.pad(rwkv_g2[l], ((0, 256 - GATE_LORA), (0, 0))).astype(BF16),
        "k_k": rwkv_k_k[l][None], "k_a": rwkv_k_a[l][None], "r_k": rwkv_r_k[l].reshape(1, RWKV_WIDTH),
    }
    lnw, lnb = rwkv_lnx_w[l][None], rwkv_lnx_b[l][None]
    qg = jnp.tile(fox_q_norm[l], FOX_HEADS)[None]
    kg = jnp.tile(fox_k_norm[l], FOX_HEADS)[None]
    bf = _pad_cols(fox_b_f[l][None], LANES)
    w_out_b = w_out[l].astype(BF16)
    w_kv = jnp.concatenate([w_xk[l], w_xv[l]], axis=1).astype(BF16)
    pc = {
        "g_cross": norm_cross[l][None], "w_xq": w_xq[l].astype(BF16), "xq_gain": xq_norm[l][None],
        "w_xo": w_xo[l].astype(BF16), "g_ffn": norm_ffn[l][None],
        "w_router": _pad_cols(w_router[l], LANES),
        "b_router": jnp.concatenate([b_router[l], jnp.full((LANES - N_EXPERTS,), NEG_BIG, F32)])[None],
    }

    def mixers(x, nseq, t, first_shift, s0, tm_mm, tm_prep, tb):
        z = _norm_mm(x, norm_mix[l][None], w_in_p, tm_mm, 768)
        z3 = z.reshape(nseq, t, Z_COLS)
        pm, pg, pw = _prev_rows(z3, tm_prep, first_shift)
        *steps, g, bonus = _rwkv_prep(z, pm, pg, pw, prep, tm_prep)
        steps = [a.reshape(nseq, t, HEAD_PAIRS, LANES) for a in steps]
        ya, s_fin = _rwkv_scan(steps, g.reshape(nseq, t, RWKV_WIDTH), bonus.reshape(nseq, t, RWKV_WIDTH),
                               _state_to_pairs(s0), lnw, lnb, tb)
        qb, kb, kf, lf = _fox_proj(z, qg, kg, bf, tm_prep)
        return z, z3, ya.reshape(nseq * t, RWKV_WIDTH), _pairs_to_state(s_fin), qb, kb, kf, lf[:, :FOX_HEADS]

    zero_shift = jnp.zeros((bp, 1, Z_COLS), F32)
    zero_state = jnp.zeros((bp, RWKV_HEADS, HEAD_DIM, HEAD_DIM), F32)
    xp = x_prompt.reshape(n_p, D_MODEL)
    xs = x_sample.reshape(n_s, D_MODEL)
    zp, zp3, ya_p, st_p, qb_p, kb_p, kf_p, lf_p = mixers(xp, bp, sp, zero_shift, zero_state, 1024, 256,
                                                         SCAN_BLOCK)
    shift_s = _pack_cols(jnp.pad(state_shift[l].reshape(bs, A_COLS), ((0, 0), (0, w_in.shape[2] - A_COLS))))
    zs, zs3, ya_s, st_s, qb_s, kb_s, kf_s, lf_s = mixers(xs, bs, ts, shift_s.reshape(bs, 1, Z_COLS),
                                                         state_rwkv[l], 256, ts, ts)

    def f_layouts(fcum, nseq):
        f4 = fcum.reshape(nseq, -1, HEAD_PAIRS, 2)
        return f4.transpose(0, 2, 1, 3), f4.transpose(0, 2, 3, 1)

    fp = jnp.cumsum(lf_p.reshape(bp, sp, FOX_HEADS), axis=1) * LOG2E
    q_aug, k_aug, vt = _augment(qb_p, kb_p, zp[:, C_FV:C_FV + FOX_WIDTH], fp, bp, sp)
    yb_p = _fox_prompt(q_aug, k_aug, vt, zp, bp, sp, min(1024, sp))

    fs = jnp.cumsum(jnp.concatenate([cache_fox_logf[l].astype(F32), lf_s.reshape(bs, ts, FOX_HEADS)], axis=1),
                    axis=1) * LOG2E
    fq_s, fkn_s = f_layouts(fs[:, past:], bs)
    _, fkc_s = f_layouts(fs[:, :past], bs)
    kc = cache_fox_k[l].reshape(bs, past, FOX_WIDTH)
    vc = cache_fox_v[l].reshape(bs, past, FOX_WIDTH)
    yb_s = _fox_sample(qb_s, kc, vc, kb_s, zs, fq_s, fkc_s, fkn_s, bs, ts, 1024)

    x1_p = _out_proj(xp, ya_p, yb_p, w_out_b, 512)
    x1_s = _out_proj(xs, ya_s, yb_s, w_out_b, 256)

    kv_p = _mem_kv(mem_prompt.reshape(bp * N_MEM, D_MODEL), norm_mem[l][None], w_kv, xk_norm[l][None], 256)
    mk_p = kv_p[:, :MEM_WIDTH].reshape(bp, N_MEM, MEM_WIDTH)
    mv_p = kv_p[:, MEM_WIDTH:].reshape(bp, N_MEM, MEM_WIDTH)
    x2_p, h_p, ti_p, tg_p = _cross_router(x1_p, mk_p.astype(BF16), mv_p.astype(BF16), pc, bp, 256)
    mk_s = cache_mem_k[l].reshape(bs, N_MEM, MEM_WIDTH).astype(BF16)
    mv_s = cache_mem_v[l].reshape(bs, N_MEM, MEM_WIDTH).astype(BF16)
    x2_s, h_s, ti_s, tg_s = _cross_router(x1_s, mk_s, mv_s, pc, bs, ts)

    h_all = jnp.concatenate([h_p, h_s], axis=0)
    top_i = jnp.concatenate([ti_p[:, :TOP_K], ti_s[:, :TOP_K]], axis=0)
    gates = jnp.concatenate([tg_p[:, :TOP_K], tg_s[:, :TOP_K]], axis=0)
    row_token, pos, tile_expert, n_used = _route(top_i)
    xs_rows = h_all[row_token]
    out_rows = _moe_experts(tile_expert, n_used, xs_rows, w_gu[l], b_gu[l][:, None, :], w_down[l],
                            b_down[l][:, None, :])
    f = jnp.sum(out_rows[pos] * gates[:, :, None], axis=1)
    y_p = (x2_p + f[:n_p]).reshape(bp, sp, D_MODEL)
    y_s = (x2_s + f[n_p:]).reshape(bs, ts, D_MODEL)

    def caches(z3, kf, lf, nseq, t):
        return (kf.reshape(1, nseq, t, FOX_HEADS, HEAD_DIM),
                z3[..., C_FV:C_FV + FOX_WIDTH].reshape(1, nseq, t, FOX_HEADS, HEAD_DIM),
                lf.reshape(1, nseq, t, FOX_HEADS))

    pk, pv, plf = caches(zp3, kf_p, lf_p, bp, sp)
    sk, sv, slf = caches(zs3, kf_s, lf_s, bs, ts)
    return (y_p, y_s, pk, pv, plf, st_p[None], _unpack_a_cols(zp3[:, -1:])[None],
            mk_p.reshape(1, bp, N_MEM, MEM_HEADS, MEM_HEAD_DIM), mv_p.reshape(1, bp, N_MEM, MEM_HEADS, MEM_HEAD_DIM),
            sk, sv, slf, st_s[None], _unpack_a_cols(zs3[:, -1:])[None])
```

```python
import functools

import jax
import jax.numpy as jnp
from jax import lax
from jax.experimental import pallas as pl
from jax.experimental.pallas import tpu as pltpu

F32 = jnp.float32
BF16 = jnp.bfloat16

D_MODEL = 2048
HEAD_DIM = 64
RWKV_WIDTH = 1024
FOX_WIDTH = 1024
RWKV_HEADS = 16
FOX_HEADS = 16
DECAY_LORA = 64
AAA_LORA = 64
GATE_LORA = 160
A_COLS = 3 * RWKV_WIDTH + DECAY_LORA + AAA_LORA + GATE_LORA
N_MEM = 256
MEM_HEADS = 4
MEM_HEAD_DIM = 128
MEM_WIDTH = 512
N_EXPERTS = 32
TOP_K = 4
D_FF = 2048
SWIGLU_LIMIT = 7.0
SWIGLU_ALPHA = 1.702
RMS_EPS = 1e-6
GN_EPS = 64e-5

LANES = 128
HEAD_PAIRS = 8

C_R, C_K, C_V = 0, 1024, 2048
C_FQ, C_FK, C_FV, C_FG = 3072, 4096, 5120, 6144
C_GL = 7168
C_WA = 7424
C_FL = 7552
Z_COLS = 7680

VMEM_LIMIT = 56 * 1024 * 1024
NEG_BIG = -1e30
LOG2E = 1.4426950408889634


def _cparams(sem):
    return pltpu.CompilerParams(dimension_semantics=sem, vmem_limit_bytes=VMEM_LIMIT)


def _block_ones():
    r = lax.broadcasted_iota(jnp.int32, (LANES, LANES), 0) // HEAD_DIM
    c = lax.broadcasted_iota(jnp.int32, (LANES, LANES), 1) // HEAD_DIM
    return (r == c).astype(BF16)


def _seg_sum64(x):
    ones = _block_ones()
    outs = []
    for s in range(x.shape[-1] // LANES):
        xs = x[:, s * LANES:(s + 1) * LANES]
        hi = xs.astype(BF16)
        lo = (xs - hi.astype(F32)).astype(BF16)
        outs.append(jnp.dot(hi, ones, preferred_element_type=F32)
                    + jnp.dot(lo, ones, preferred_element_type=F32))
    return outs[0] if len(outs) == 1 else jnp.concatenate(outs, axis=-1)


def _softplus(x):
    return jnp.maximum(x, 0.0) + jnp.log(1.0 + jnp.exp(-jnp.abs(x)))


def _sigmoid(x):
    return 1.0 / (1.0 + jnp.exp(-x))


def _norm_mm_kernel(x_ref, g_ref, w_ref, o_ref, xn_ref):
    @pl.when(pl.program_id(1) == 0)
    def _():
        x = x_ref[...]
        ms = jnp.mean(x * x, axis=-1, keepdims=True)
        xn_ref[...] = (x * lax.rsqrt(ms + RMS_EPS) * g_ref[...]).astype(BF16)

    o_ref[...] = jnp.dot(xn_ref[...], w_ref[...], preferred_element_type=F32)


def _norm_mm(x, g, w, tm, tn):
    m, k = x.shape
    n = w.shape[1]
    return pl.pallas_call(
        _norm_mm_kernel,
        grid=(m // tm, n // tn),
        in_specs=[pl.BlockSpec((tm, k), lambda i, j: (i, 0)),
                  pl.BlockSpec((1, k), lambda i, j: (0, 0)),
                  pl.BlockSpec((k, tn), lambda i, j: (0, j))],
        out_specs=pl.BlockSpec((tm, tn), lambda i, j: (i, j)),
        out_shape=jax.ShapeDtypeStruct((m, n), F32),
        scratch_shapes=[pltpu.VMEM((tm, k), BF16)],
        compiler_params=_cparams(("parallel", "arbitrary")),
        name="norm_mm",
    )(x, g, w)


def _rwkv_prep_kernel(zm_ref, zg_ref, zw_ref, pm_ref, pg_ref, pw_ref,
                      mum_ref, mug_ref, muw_ref, w0_ref, w2_ref, a0_ref, a2_ref, g2_ref,
                      kk_ref, ka_ref, rk_ref,
                      r_o, d_o, k_o, v_o, kk_o, b_o, g_o, bonus_o):
    def shifted(z_ref, p_ref, mu_ref):
        z = z_ref[...]
        rolled = pltpu.roll(z, 1, 0)
        row = lax.broadcasted_iota(jnp.int32, z.shape, 0)
        prev = jnp.where(row == 0, p_ref[0], rolled)
        return z + mu_ref[...] * (prev - z)

    zm = shifted(zm_ref, pm_ref, mum_ref)
    zg = shifted(zg_ref, pg_ref, mug_ref)
    zw = shifted(zw_ref, pw_ref, muw_ref)
    r = zm[:, C_R:C_R + RWKV_WIDTH]
    k = zm[:, C_K:C_K + RWKV_WIDTH]
    v = zm[:, C_V:C_V + RWKV_WIDTH]
    lw = jnp.dot(jnp.tanh(zw).astype(BF16), w2_ref[...], preferred_element_type=F32)
    la = jnp.dot(zw.astype(BF16), a2_ref[...], preferred_element_type=F32)
    g = jnp.dot(_sigmoid(zg).astype(BF16), g2_ref[...], preferred_element_type=F32)
    w_log = -_softplus(-(w0_ref[...] + lw)) - 0.5
    a = _sigmoid(a0_ref[...] + la)
    kk = k * kk_ref[...]
    kk = kk * lax.rsqrt(jnp.maximum(_seg_sum64(kk * kk), 1e-24))
    kh = k * (1.0 + (a - 1.0) * ka_ref[...])
    tiles = lambda x: x.reshape(x.shape[0], HEAD_PAIRS, LANES)
    r_o[...] = tiles(r)
    d_o[...] = tiles(jnp.exp(-jnp.exp(w_log)))
    k_o[...] = tiles(kh)
    v_o[...] = tiles(v)
    kk_o[...] = tiles(kk)
    b_o[...] = tiles(kk * a)
    g_o[...] = g
    bonus_o[...] = _seg_sum64(r * kh * rk_ref[...]) * v


def _rwkv_prep(z, prev_m, prev_g, prev_w, pr, tm):
    n = z.shape[0]
    row = lambda w, c: pl.BlockSpec((tm, w), lambda i, c=c: (i, c))
    prev = lambda w: pl.BlockSpec((1, 1, w), lambda i: (i, 0, 0))
    full = lambda a: pl.BlockSpec(a.shape, lambda i: (0,) * a.ndim)
    params = [pr["mu_m"], pr["mu_g"], pr["mu_w"], pr["w0"], pr["w2"], pr["a0"], pr["a2"], pr["g2"],
              pr["k_k"], pr["k_a"], pr["r_k"]]
    tiled = jax.ShapeDtypeStruct((n, HEAD_PAIRS, LANES), F32)
    flat = jax.ShapeDtypeStruct((n, RWKV_WIDTH), F32)
    return pl.pallas_call(
        _rwkv_prep_kernel,
        grid=(n // tm,),
        in_specs=[row(3072, 0), row(256, C_GL // 256), row(128, C_WA // 128),
                  prev(3072), prev(256), prev(128)] + [full(a) for a in params],
        out_specs=[pl.BlockSpec((tm, HEAD_PAIRS, LANES), lambda i: (i, 0, 0))] * 6
        + [pl.BlockSpec((tm, RWKV_WIDTH), lambda i: (i, 0))] * 2,
        out_shape=[tiled] * 6 + [flat] * 2,
        compiler_params=_cparams(("parallel",)),
        name="rwkv_prep",
    )(z, z, z, prev_m, prev_g, prev_w, *params)


SEQ_PER_STEP = 2
SCAN_BLOCK = 128


def _rwkv_scan_kernel(r_ref, d_ref, k_ref, v_ref, kk_ref, b_ref, g_ref, bonus_ref, s0_ref,
                      lw_ref, lb_ref,
                      y_ref, sf_ref,
                      s_ref, oa_ref, *, tb):
    tblk = pl.program_id(1)

    @pl.when(tblk == 0)
    def _():
        s_ref[...] = s0_ref[...]

    oa_ref[...] = jnp.zeros_like(oa_ref)
    r2 = lax.broadcasted_iota(jnp.int32, (2 * LANES, 2 * LANES), 0) // HEAD_DIM
    c2 = lax.broadcasted_iota(jnp.int32, (2 * LANES, 2 * LANES), 1) // HEAD_DIM
    ones = (r2 == c2).astype(BF16)
    lane = lax.broadcasted_iota(jnp.int32, (HEAD_DIM, LANES), 1)
    sub = lax.broadcasted_iota(jnp.int32, (HEAD_DIM, LANES), 0)
    diag = (lane % HEAD_DIM) == sub
    tiles = [(s, hp) for s in range(SEQ_PER_STEP) for hp in range(HEAD_PAIRS)]

    def bcast(ref, s, t, hp):
        return jnp.broadcast_to(ref[s, t, hp:hp + 1, :], (HEAD_DIM, LANES))

    def seg_sums(parts):
        lhs = jnp.concatenate([jnp.concatenate(parts[j:j + 2], axis=1) for j in range(0, len(parts), 2)], axis=0)
        out = jnp.dot(lhs, ones, preferred_element_type=F32)
        return [out[(j // 2) * HEAD_DIM:(j // 2 + 1) * HEAD_DIM, (j % 2) * LANES:(j % 2 + 1) * LANES]
                for j in range(len(parts))]

    def emit_outputs(t_out, valid):
        half = pl.multiple_of((t_out // HEAD_DIM) * HEAD_DIM, HEAD_DIM)
        hit = jnp.logical_and((lane % HEAD_DIM) == (t_out % HEAD_DIM), valid)
        ob = seg_sums([(s_ref[s, hp] * bcast(r_ref, s, t_out, hp)).astype(BF16) for s, hp in tiles])
        for j, (s, hp) in enumerate(tiles):
            cur = oa_ref[s, hp, pl.ds(half, HEAD_DIM), :]
            oa_ref[s, hp, pl.ds(half, HEAD_DIM), :] = jnp.where(hit, ob[j], cur)

    def step(t, carry):
        emit_outputs(jnp.maximum(t - 1, 0), t > 0)
        skk = seg_sums([(s_ref[s, hp] * bcast(kk_ref, s, t, hp)).astype(BF16) for s, hp in tiles])
        vb = seg_sums([jnp.where(diag, bcast(v_ref, s, t, hp), 0.0).astype(BF16) for s, hp in tiles])
        for j, (s, hp) in enumerate(tiles):
            s_ref[s, hp] = (s_ref[s, hp] * bcast(d_ref, s, t, hp)
                            - skk[j] * bcast(b_ref, s, t, hp)
                            + vb[j] * bcast(k_ref, s, t, hp))
        return carry

    lax.fori_loop(0, tb, step, 0)
    emit_outputs(tb - 1, True)

    lane_t = lax.broadcasted_iota(jnp.int32, (HEAD_DIM, LANES), 1)
    low = lane_t < HEAD_DIM
    for s in range(SEQ_PER_STEP):
        slabs = []
        for hp in range(HEAD_PAIRS):
            mt = oa_ref[s, hp].T
            ro = pltpu.roll(mt, HEAD_DIM, 1)
            top = jnp.where(low, mt[:HEAD_DIM], ro[HEAD_DIM:])
            bot = jnp.where(low, ro[:HEAD_DIM], mt[HEAD_DIM:])
            slabs.append(jnp.concatenate([top, bot], axis=0)[:tb])
        o = jnp.concatenate(slabs, axis=-1)
        mean = _seg_sum64(o) * (1.0 / HEAD_DIM)
        cen = o - mean
        var = _seg_sum64(cen * cen) * (1.0 / HEAD_DIM)
        on = cen * lax.rsqrt(var + GN_EPS) * lw_ref[...] + lb_ref[...]
        y_ref[s] = ((on + bonus_ref[s]) * g_ref[s]).astype(y_ref.dtype)

    @pl.when(tblk == pl.num_programs(1) - 1)
    def _():
        sf_ref[...] = s_ref[...]


def _rwkv_scan(steps, g, bonus, s0, lnw, lnb, tb):
    nseq, t = g.shape[:2]
    step_spec = pl.BlockSpec((SEQ_PER_STEP, tb, HEAD_PAIRS, LANES), lambda i, j: (i, j, 0, 0))
    seq_spec = pl.BlockSpec((SEQ_PER_STEP, tb, RWKV_WIDTH), lambda i, j: (i, j, 0))
    st_spec = pl.BlockSpec((SEQ_PER_STEP, HEAD_PAIRS, HEAD_DIM, LANES), lambda i, j: (i, 0, 0, 0))
    par_spec = pl.BlockSpec((1, RWKV_WIDTH), lambda i, j: (0, 0))
    return pl.pallas_call(
        functools.partial(_rwkv_scan_kernel, tb=tb),
        grid=(nseq // SEQ_PER_STEP, t // tb),
        in_specs=[step_spec] * 6 + [seq_spec] * 2 + [st_spec] + [par_spec] * 2,
        out_specs=[seq_spec, st_spec],
        out_shape=[jax.ShapeDtypeStruct((nseq, t, RWKV_WIDTH), BF16),
                   jax.ShapeDtypeStruct(s0.shape, F32)],
        scratch_shapes=[pltpu.VMEM((SEQ_PER_STEP, HEAD_PAIRS, HEAD_DIM, LANES), F32),
                        pltpu.VMEM((SEQ_PER_STEP, HEAD_PAIRS, LANES, LANES), F32)],
        compiler_params=_cparams(("parallel", "arbitrary")),
        name="rwkv_scan",
    )(*steps, g, bonus, s0, lnw, lnb)


def _state_to_pairs(s):
    b = s.shape[0]
    return s.reshape(b, HEAD_PAIRS, 2, HEAD_DIM, HEAD_DIM).transpose(0, 1, 3, 2, 4).reshape(
        b, HEAD_PAIRS, HEAD_DIM, LANES)


def _pairs_to_state(s):
    b = s.shape[0]
    return s.reshape(b, HEAD_PAIRS, HEAD_DIM, 2, HEAD_DIM).transpose(0, 1, 3, 2, 4).reshape(
        b, RWKV_HEADS, HEAD_DIM, HEAD_DIM)


def _fox_proj_kernel(q_ref, k_ref, fl_ref, qg_ref, kg_ref, bf_ref, qb_o, kb_o, kf_o, lf_o):
    def headnorm(x, g):
        ms = _seg_sum64(x * x) * (1.0 / HEAD_DIM)
        return x * lax.rsqrt(ms + RMS_EPS) * g

    q = headnorm(q_ref[...], qg_ref[...])
    k = headnorm(k_ref[...], kg_ref[...])
    qb_o[...] = (q * (HEAD_DIM ** -0.5 * LOG2E)).astype(BF16)
    kb_o[...] = k.astype(BF16)
    kf_o[...] = k
    lf_o[...] = -_softplus(-(fl_ref[...] + bf_ref[...]))


def _fox_proj(z, qg, kg, bf, tm):
    n = z.shape[0]
    col = lambda w, c: pl.BlockSpec((tm, w), lambda i, c=c: (i, c))
    par = lambda w: pl.BlockSpec((1, w), lambda i: (0, 0))
    wide = pl.BlockSpec((tm, FOX_WIDTH), lambda i: (i, 0))
    return pl.pallas_call(
        _fox_proj_kernel,
        grid=(n // tm,),
        in_specs=[col(1024, C_FQ // 1024), col(1024, C_FK // 1024), col(128, C_FL // 128),
                  par(1024), par(1024), par(128)],
        out_specs=[wide, wide, wide, pl.BlockSpec((tm, LANES), lambda i: (i, 0))],
        out_shape=[jax.ShapeDtypeStruct((n, FOX_WIDTH), BF16), jax.ShapeDtypeStruct((n, FOX_WIDTH), BF16),
                   jax.ShapeDtypeStruct((n, FOX_WIDTH), F32), jax.ShapeDtypeStruct((n, LANES), F32)],
        compiler_params=_cparams(("parallel",)),
        name="fox_proj",
    )(z, z, z, qg, kg, bf)


def _fox_prompt_kernel(qi_ref, kj_ref, q_ref, k_ref, vt_ref, gate_ref, y_ref, m_ref, l_ref, acc_ref, *, tq):
    pair = pl.program_id(2)
    qi = qi_ref[pair]
    kj = kj_ref[pair]

    @pl.when(kj == 0)
    def _():
        m_ref[...] = jnp.full_like(m_ref, NEG_BIG)
        l_ref[...] = jnp.zeros_like(l_ref)
        acc_ref[...] = jnp.zeros_like(acc_ref)

    def body(masked):
        for h in range(2):
            st = lax.dot_general(k_ref[0, h], q_ref[0, h], (((1,), (1,)), ((), ())), preferred_element_type=F32)
            if masked:
                key = lax.broadcasted_iota(jnp.int32, st.shape, 0)
                qry = lax.broadcasted_iota(jnp.int32, st.shape, 1)
                st = jnp.where(key <= qry, st, NEG_BIG)
            m_prev = m_ref[h]
            m_new = jnp.maximum(m_prev, jnp.max(st, axis=0, keepdims=True))
            alpha = jnp.exp2(m_prev - m_new)
            p = jnp.exp2(st - m_new)
            l_ref[h] = alpha * l_ref[h] + jnp.sum(p, axis=0, keepdims=True)
            acc_ref[h] = alpha * acc_ref[h] + jnp.dot(vt_ref[0, h], p.astype(BF16), preferred_element_type=F32)
            m_ref[h] = m_new

    @pl.when(kj < qi)
    def _():
        body(False)

    @pl.when(kj == qi)
    def _():
        body(True)
        ot = jnp.concatenate([acc_ref[0] / l_ref[0], acc_ref[1] / l_ref[1]], axis=0)
        y_ref[...] = (ot.T * _sigmoid(gate_ref[...])).astype(y_ref.dtype)


def _fox_prompt(q_aug, k_aug, vt, z, nb, seq, tq):
    nq = seq // tq
    pairs = [(i, j) for i in range(nq) for j in range(i + 1)]
    qi_tab = jnp.asarray([p[0] for p in pairs], jnp.int32)
    kj_tab = jnp.asarray([p[1] for p in pairs], jnp.int32)
    grid_spec = pltpu.PrefetchScalarGridSpec(
        num_scalar_prefetch=2,
        grid=(nb, HEAD_PAIRS, len(pairs)),
        in_specs=[pl.BlockSpec((1, 2, tq, LANES), lambda b, hp, p, qi, kj: (b, hp, qi[p], 0)),
                  pl.BlockSpec((1, 2, tq, LANES), lambda b, hp, p, qi, kj: (b, hp, kj[p], 0)),
                  pl.BlockSpec((1, 2, HEAD_DIM, tq), lambda b, hp, p, qi, kj: (b, hp, 0, kj[p])),
                  pl.BlockSpec((tq, LANES), lambda b, hp, p, qi, kj: (b * nq + qi[p], C_FG // LANES + hp))],
        out_specs=pl.BlockSpec((tq, LANES), lambda b, hp, p, qi, kj: (b * nq + qi[p], hp)),
        scratch_shapes=[pltpu.VMEM((2, 1, tq), F32), pltpu.VMEM((2, 1, tq), F32),
                        pltpu.VMEM((2, HEAD_DIM, tq), F32)],
    )
    return pl.pallas_call(
        functools.partial(_fox_prompt_kernel, tq=tq),
        grid_spec=grid_spec,
        out_shape=jax.ShapeDtypeStruct((nb * seq, FOX_WIDTH), BF16),
        compiler_params=_cparams(("parallel", "parallel", "arbitrary")),
        name="fox_prompt",
    )(qi_tab, kj_tab, q_aug, k_aug, vt, z)


def _split3(x):
    def top(v):
        bits = lax.bitcast_convert_type(v, jnp.uint32) & jnp.uint32(0xFFFF0000)
        return lax.bitcast_convert_type(bits, F32)

    hi = top(x)
    mid = top(x - hi)
    lo = top(x - hi - mid)
    return hi.astype(BF16), mid.astype(BF16), lo.astype(BF16)


def _augment(qb, kb, vz, f2, nb, seq):
    heads = lambda a: a.reshape(nb, seq, FOX_HEADS, HEAD_DIM).transpose(0, 2, 1, 3)
    fh = f2.transpose(0, 2, 1)[..., None]
    one = jnp.ones_like(fh, BF16)
    pad = jnp.zeros((nb, FOX_HEADS, seq, LANES - HEAD_DIM - 6), BF16)
    fq3 = _split3(fh)
    fk3 = _split3(-fh)
    q_aug = jnp.concatenate([heads(qb), *fq3, one, one, one, pad], axis=-1)
    k_aug = jnp.concatenate([heads(kb), one, one, one, *fk3, pad], axis=-1)
    vt = vz.astype(BF16).reshape(nb, seq, FOX_HEADS, HEAD_DIM).transpose(0, 2, 3, 1)
    return q_aug, k_aug, vt


def _fox_sample_kernel(q_ref, kc_ref, vc_ref, kn_ref, vn_ref, fq_ref, fkc_ref, fkn_ref, gate_ref, y_ref,
                       m_ref, l_ref, acc_ref, *, nk):
    kj = pl.program_id(1)

    @pl.when(kj == 0)
    def _():
        m_ref[...] = jnp.full_like(m_ref, NEG_BIG)
        l_ref[...] = jnp.zeros_like(l_ref)
        acc_ref[...] = jnp.zeros_like(acc_ref)

    def head_slice(ref, h):
        return ref[:, h * HEAD_DIM:(h + 1) * HEAD_DIM]

    t = q_ref.shape[0]

    def attend(ks, vs, fk_ref, masked):
        s = jnp.concatenate(
            [lax.dot_general(head_slice(q_ref, h), ks[h], (((1,), (1,)), ((), ())), preferred_element_type=F32)
             for h in range(FOX_HEADS)], axis=0)
        fk = jnp.concatenate([jnp.broadcast_to(fk_ref[0, h:h + 1, :], (t, s.shape[1])) for h in range(FOX_HEADS)],
                             axis=0)
        s = s + fq_ref[0] - fk
        if masked:
            qry = lax.broadcasted_iota(jnp.int32, s.shape, 0) % t
            key = lax.broadcasted_iota(jnp.int32, s.shape, 1)
            s = jnp.where(key <= qry, s, NEG_BIG)
        m_prev = m_ref[...]
        m_new = jnp.maximum(m_prev, jnp.max(s, axis=-1, keepdims=True))
        alpha = jnp.exp2(m_prev - m_new)
        p = jnp.exp2(s - m_new)
        l_ref[...] = alpha * l_ref[...] + jnp.sum(p, axis=-1, keepdims=True)
        p = p.astype(BF16)
        pv = jnp.concatenate([jnp.dot(p[h * t:(h + 1) * t], vs[h], preferred_element_type=F32)
                              for h in range(FOX_HEADS)], axis=0)
        acc_ref[...] = alpha * acc_ref[...] + pv
        m_ref[...] = m_new

    @pl.when(kj < nk)
    def _():
        rows = lambda h: pl.ds(h, kc_ref.shape[1] // FOX_HEADS, stride=FOX_HEADS)
        attend([kc_ref[0, rows(h), :].astype(BF16) for h in range(FOX_HEADS)],
               [vc_ref[0, rows(h), :].astype(BF16) for h in range(FOX_HEADS)], fkc_ref, False)

    @pl.when(kj == nk)
    def _():
        attend([head_slice(kn_ref, h) for h in range(FOX_HEADS)],
               [head_slice(vn_ref, h).astype(BF16) for h in range(FOX_HEADS)], fkn_ref, True)
        o = acc_ref[...] / l_ref[...]
        for h in range(FOX_HEADS):
            gate = _sigmoid(head_slice(gate_ref, h))
            y_ref[:, h * HEAD_DIM:(h + 1) * HEAD_DIM] = (o[h * t:(h + 1) * t] * gate).astype(y_ref.dtype)


def _fox_sample(qb, kc, vc, kb, z, fq, fkc, fkn, nb, t, tk):
    past = kc.shape[1]
    nk = past // tk
    last = nk - 1
    kc = kc.reshape(nb, past * FOX_HEADS, HEAD_DIM)
    vc = vc.reshape(nb, past * FOX_HEADS, HEAD_DIM)
    wide = lambda c: pl.BlockSpec((t, FOX_WIDTH), lambda b, j, c=c: (b, c))
    cache = pl.BlockSpec((1, tk * FOX_HEADS, HEAD_DIM), lambda b, j: (b, jnp.minimum(j, last), 0))
    return pl.pallas_call(
        functools.partial(_fox_sample_kernel, nk=nk),
        grid=(nb, nk + 1),
        in_specs=[wide(0), cache, cache, wide(0), wide(C_FV // FOX_WIDTH),
                  pl.BlockSpec((1, FOX_HEADS * t, 1), lambda b, j: (b, 0, 0)),
                  pl.BlockSpec((1, FOX_HEADS, tk), lambda b, j: (b, 0, jnp.minimum(j, last))),
                  pl.BlockSpec((1, FOX_HEADS, t), lambda b, j: (b, 0, 0)),
                  wide(C_FG // FOX_WIDTH)],
        out_specs=wide(0),
        out_shape=jax.ShapeDtypeStruct((nb * t, FOX_WIDTH), BF16),
        scratch_shapes=[pltpu.VMEM((FOX_HEADS * t, 1), F32), pltpu.VMEM((FOX_HEADS * t, 1), F32),
                        pltpu.VMEM((FOX_HEADS * t, HEAD_DIM), F32)],
        compiler_params=_cparams(("parallel", "arbitrary")),
        name="fox_sample",
    )(qb, kc, vc, kb, z, fq, fkc, fkn, z)


def _out_proj_kernel(x_ref, ya_ref, yb_ref, wa_ref, wb_ref, o_ref):
    o_ref[...] = (x_ref[...]
                  + jnp.dot(ya_ref[...], wa_ref[...], preferred_element_type=F32)
                  + jnp.dot(yb_ref[...], wb_ref[...], preferred_element_type=F32))


def _out_proj(x, ya, yb, w, tm):
    n = x.shape[0]
    return pl.pallas_call(
        _out_proj_kernel,
        grid=(n // tm,),
        in_specs=[pl.BlockSpec((tm, D_MODEL), lambda i: (i, 0)),
                  pl.BlockSpec((tm, RWKV_WIDTH), lambda i: (i, 0)),
                  pl.BlockSpec((tm, FOX_WIDTH), lambda i: (i, 0)),
                  pl.BlockSpec((RWKV_WIDTH, D_MODEL), lambda i: (0, 0)),
                  pl.BlockSpec((FOX_WIDTH, D_MODEL), lambda i: (1, 0))],
        out_specs=pl.BlockSpec((tm, D_MODEL), lambda i: (i, 0)),
        out_shape=jax.ShapeDtypeStruct((n, D_MODEL), F32),
        compiler_params=_cparams(("parallel",)),
        name="out_proj",
    )(x, ya, yb, w, w)


def _mem_kv_kernel(x_ref, g_ref, w_ref, kg_ref, o_ref):
    x = x_ref[...]
    ms = jnp.mean(x * x, axis=-1, keepdims=True)
    xn = (x * lax.rsqrt(ms + RMS_EPS) * g_ref[...]).astype(BF16)
    kv = jnp.dot(xn, w_ref[...], preferred_element_type=F32)
    for h in range(MEM_HEADS):
        kh = kv[:, h * MEM_HEAD_DIM:(h + 1) * MEM_HEAD_DIM]
        ms = jnp.mean(kh * kh, axis=-1, keepdims=True)
        o_ref[:, h * MEM_HEAD_DIM:(h + 1) * MEM_HEAD_DIM] = kh * lax.rsqrt(ms + RMS_EPS) * kg_ref[...]
    o_ref[:, MEM_WIDTH:] = kv[:, MEM_WIDTH:]


def _mem_kv(mem, g, w_kv, kg, tm):
    n = mem.shape[0]
    return pl.pallas_call(
        _mem_kv_kernel,
        grid=(n // tm,),
        in_specs=[pl.BlockSpec((tm, D_MODEL), lambda i: (i, 0)),
                  pl.BlockSpec((1, D_MODEL), lambda i: (0, 0)),
                  pl.BlockSpec((D_MODEL, 2 * MEM_WIDTH), lambda i: (0, 0)),
                  pl.BlockSpec((1, MEM_HEAD_DIM), lambda i: (0, 0))],
        out_specs=pl.BlockSpec((tm, 2 * MEM_WIDTH), lambda i: (i, 0)),
        out_shape=jax.ShapeDtypeStruct((n, 2 * MEM_WIDTH), F32),
        compiler_params=_cparams(("parallel",)),
        name="mem_kv",
    )(mem, g, w_kv, kg)


def _cross_router_kernel(x_ref, gc_ref, wq_ref, qg_ref, mk_ref, mv_ref, wo_ref,
                         gf_ref, wr_ref, br_ref,
                         x2_ref, h_ref, idx_ref, gate_ref):
    x = x_ref[...]
    ms = jnp.mean(x * x, axis=-1, keepdims=True)
    xn = (x * lax.rsqrt(ms + RMS_EPS) * gc_ref[...]).astype(BF16)
    q = jnp.dot(xn, wq_ref[...], preferred_element_type=F32)
    outs = []
    for h in range(MEM_HEADS):
        sl = slice(h * MEM_HEAD_DIM, (h + 1) * MEM_HEAD_DIM)
        qh = q[:, sl]
        qms = jnp.mean(qh * qh, axis=-1, keepdims=True)
        qh = (qh * lax.rsqrt(qms + RMS_EPS) * qg_ref[...]).astype(BF16)
        s = lax.dot_general(qh, mk_ref[0, :, sl], (((1,), (1,)), ((), ())), preferred_element_type=F32)
        s = s * (MEM_HEAD_DIM ** -0.5)
        p = jnp.exp(s - jnp.max(s, axis=-1, keepdims=True))
        p = p / jnp.sum(p, axis=-1, keepdims=True)
        outs.append(jnp.dot(p.astype(BF16), mv_ref[0, :, sl], preferred_element_type=F32))
    o = jnp.concatenate(outs, axis=-1).astype(BF16)
    x2 = x + jnp.dot(o, wo_ref[...], preferred_element_type=F32)
    x2_ref[...] = x2

    ms2 = jnp.mean(x2 * x2, axis=-1, keepdims=True)
    hn = x2 * lax.rsqrt(ms2 + RMS_EPS) * gf_ref[...]
    h_ref[...] = hn.astype(BF16)
    logits = jnp.dot(hn, wr_ref[...], preferred_element_type=F32, precision=lax.Precision.HIGHEST)
    logits = logits + br_ref[...]
    lane = lax.broadcasted_iota(jnp.int32, logits.shape, 1)
    idx_acc = jnp.zeros(logits.shape, jnp.int32)
    val_acc = jnp.zeros(logits.shape, F32)
    top = None
    for kk in range(TOP_K):
        m = jnp.max(logits, axis=-1, keepdims=True)
        sel = jnp.min(jnp.where(logits == m, lane, LANES), axis=-1, keepdims=True)
        if kk == 0:
            top = m
        idx_acc = jnp.where(lane == kk, sel, idx_acc)
        val_acc = jnp.where(lane == kk, jnp.exp(m - top), val_acc)
        logits = jnp.where(lane == sel, -jnp.inf, logits)
    idx_ref[...] = idx_acc
    gate_ref[...] = val_acc / jnp.sum(val_acc, axis=-1, keepdims=True)


def _cross_router(x, mk, mv, pc, nb, tm):
    n = x.shape[0]
    per = n // nb // tm
    full = lambda a: pl.BlockSpec(a.shape, lambda i: (0,) * a.ndim)
    mem = pl.BlockSpec((1, N_MEM, MEM_WIDTH), lambda i: (i // per, 0, 0))
    row = lambda w: pl.BlockSpec((tm, w), lambda i: (i, 0))
    params1 = [pc["g_cross"], pc["w_xq"], pc["xq_gain"]]
    params2 = [pc["w_xo"], pc["g_ffn"], pc["w_router"], pc["b_router"]]
    return pl.pallas_call(
        _cross_router_kernel,
        grid=(n // tm,),
        in_specs=[row(D_MODEL)] + [full(a) for a in params1] + [mem, mem] + [full(a) for a in params2],
        out_specs=[row(D_MODEL), row(D_MODEL), row(LANES), row(LANES)],
        out_shape=[jax.ShapeDtypeStruct((n, D_MODEL), F32), jax.ShapeDtypeStruct((n, D_MODEL), BF16),
                   jax.ShapeDtypeStruct((n, LANES), jnp.int32), jax.ShapeDtypeStruct((n, LANES), F32)],
        compiler_params=_cparams(("parallel",)),
        name="cross_router",
    )(x, *params1, mk, mv, *params2)


MOE_TM = 768
MOE_TF = 256
MOE_TN = 512
MOE_NG = D_FF // (2 * MOE_TF)
MOE_ND = D_MODEL // MOE_TN


def _moe_kernel(te_ref, tv_ref, xs_ref, wgu_ref, bgu_ref, wd_ref, bd_ref, o_ref, act_ref):
    s = pl.program_id(1)
    used = tv_ref[pl.program_id(0)] > 0
    half = 2 * MOE_TF

    @pl.when(jnp.logical_and(used, s < MOE_NG))
    def _():
        gu = jnp.dot(xs_ref[...], wgu_ref[0].astype(BF16), preferred_element_type=F32) + bgu_ref[0]
        glu = jnp.minimum(gu, SWIGLU_LIMIT)
        fglu = glu * _sigmoid(SWIGLU_ALPHA * glu)
        lin = jnp.clip(gu, -SWIGLU_LIMIT, SWIGLU_LIMIT) + 1.0
        act_a = pltpu.roll(fglu[:, :half], 1, 1) * lin[:, :half]
        act_b = fglu[:, half:] * pltpu.roll(lin[:, half:], half - 1, 1)
        lane = lax.broadcasted_iota(jnp.int32, act_a.shape, 1)
        merged = jnp.where(lane % 2 == 0, act_b, act_a).astype(BF16)
        for c in range(MOE_NG):
            @pl.when(s == c)
            def _(c=c):
                act_ref[:, c * half:(c + 1) * half] = merged

    @pl.when(jnp.logical_and(used, s >= MOE_NG))
    def _():
        top = lambda w: lax.bitcast_convert_type(w.astype(BF16).astype(F32), jnp.uint32)
        chunks = []
        for c in range(MOE_NG):
            a = wd_ref[0, c * half:c * half + MOE_TF, :]
            b = wd_ref[0, c * half + MOE_TF:(c + 1) * half, :]
            chunks.append(pltpu.bitcast((top(b) >> 16) | top(a), BF16))
        wd = jnp.concatenate(chunks, axis=0)
        o_ref[...] = jnp.dot(act_ref[...], wd, preferred_element_type=F32) + bd_ref[0]

    @pl.when(jnp.logical_and(jnp.logical_not(used), s >= MOE_NG))
    def _():
        o_ref[...] = jnp.zeros_like(o_ref)


def _moe_experts(tile_expert, tile_valid, xs, w_gu, b_gu, w_down, b_down):
    p = xs.shape[0]
    nt = p // MOE_TM
    gi = lambda i, s, tv: jnp.where(tv[i] > 0, jnp.minimum(s, MOE_NG - 1), MOE_NG - 1)
    di = lambda i, s, tv: jnp.where(tv[i] > 0, jnp.maximum(s - MOE_NG, 0), MOE_ND - 1)
    grid_spec = pltpu.PrefetchScalarGridSpec(
        num_scalar_prefetch=2,
        grid=(nt, MOE_NG + MOE_ND),
        in_specs=[pl.BlockSpec((MOE_TM, D_MODEL), lambda i, s, te, tv: (i, 0)),
                  pl.BlockSpec((1, D_MODEL, 4 * MOE_TF), lambda i, s, te, tv: (te[i], 0, gi(i, s, tv))),
                  pl.BlockSpec((1, 1, 4 * MOE_TF), lambda i, s, te, tv: (te[i], 0, gi(i, s, tv))),
                  pl.BlockSpec((1, D_FF, MOE_TN), lambda i, s, te, tv: (te[i], 0, di(i, s, tv))),
                  pl.BlockSpec((1, 1, MOE_TN), lambda i, s, te, tv: (te[i], 0, di(i, s, tv)))],
        out_specs=pl.BlockSpec((MOE_TM, MOE_TN), lambda i, s, te, tv: (i, jnp.maximum(s - MOE_NG, 0))),
        scratch_shapes=[pltpu.VMEM((MOE_TM, D_FF), BF16)],
    )
    return pl.pallas_call(
        _moe_kernel,
        grid_spec=grid_spec,
        out_shape=jax.ShapeDtypeStruct((p, D_MODEL), F32),
        compiler_params=_cparams(("arbitrary", "arbitrary")),
        name="moe_experts",
    )(tile_expert, tile_valid, xs, w_gu, b_gu, w_down, b_down)


def _route(top_i):
    n = top_i.shape[0]
    na = n * TOP_K
    flat_e = top_i.reshape(na)
    order = jnp.argsort(flat_e, stable=True).astype(jnp.int32)
    inv = jnp.argsort(order).astype(jnp.int32)
    counts = jnp.sum((flat_e[:, None] == jnp.arange(N_EXPERTS, dtype=jnp.int32)[None, :]).astype(jnp.int32),
                     axis=0)
    padded = ((counts + MOE_TM - 1) // MOE_TM) * MOE_TM
    pad_start = jnp.cumsum(padded) - padded
    start = jnp.cumsum(counts) - counts
    pos = (inv + (pad_start - start)[flat_e]).reshape(n, TOP_K)
    nt = na // MOE_TM + N_EXPERTS
    p_rows = nt * MOE_TM
    tile_start = jnp.arange(nt, dtype=jnp.int32) * MOE_TM
    pad_end = pad_start + padded
    n_used = (jnp.sum(padded) // MOE_TM).astype(jnp.int32)
    tile_expert = jnp.minimum(jnp.sum((tile_start[:, None] >= pad_end[None, :]).astype(jnp.int32), axis=1),
                              N_EXPERTS - 1)
    used = jnp.arange(nt) < n_used
    tile_expert = jnp.where(used, tile_expert, tile_expert[jnp.maximum(n_used - 1, 0)])
    tile_valid = jnp.where(used, jnp.clip(counts[tile_expert] - (tile_start - pad_start[tile_expert]), 0, MOE_TM), 0)
    row = jnp.arange(p_rows, dtype=jnp.int32)
    row_e = jnp.repeat(tile_expert, MOE_TM)
    rank = row - pad_start[row_e]
    src = jnp.clip(start[row_e] + rank, 0, na - 1)
    row_token = jnp.where(rank < counts[row_e], order[src] // TOP_K, 0)
    return row_token, pos, tile_expert.astype(jnp.int32), tile_valid.astype(jnp.int32)


def _pad_cols(a, width):
    return jnp.pad(a, ((0, 0), (0, width - a.shape[1])))


def _pack_cols(a):
    o_w = 3 * RWKV_WIDTH
    o_a = o_w + DECAY_LORA
    o_g = o_a + AAA_LORA
    fb = A_COLS
    return jnp.concatenate([
        a[:, 0:3 * RWKV_WIDTH],
        a[:, fb:fb + 4 * FOX_WIDTH],
        _pad_cols(a[:, o_g:A_COLS], 256),
        a[:, o_w:o_g],
        _pad_cols(a[:, fb + 4 * FOX_WIDTH:], LANES),
    ], axis=1)


def _unpack_a_cols(z):
    return jnp.concatenate([z[..., 0:3 * RWKV_WIDTH], z[..., C_WA:C_WA + 128], z[..., C_GL:C_GL + GATE_LORA]],
                           axis=-1)


def _prev_rows(z, tm, first):
    nseq, t, _ = z.shape
    if t > tm:
        inner = z[:, tm - 1:t - 1:tm]
        rows = jnp.concatenate([first, inner], axis=1)
    else:
        rows = first
    rows = rows.reshape(-1, 1, Z_COLS)
    return rows[..., 0:3072], rows[..., C_GL:C_GL + 256], rows[..., C_WA:C_WA + 128]


def kernel(x_prompt, x_sample, mem_prompt, cache_fox_k, cache_fox_v, cache_fox_logf, state_rwkv, state_shift,
           cache_mem_k, cache_mem_v, norm_mix, w_in, rwkv_mu, rwkv_w0, rwkv_w2, rwkv_a0, rwkv_a2, rwkv_g2,
           rwkv_k_k, rwkv_k_a, rwkv_r_k, rwkv_lnx_w, rwkv_lnx_b, fox_b_f, fox_q_norm, fox_k_norm, w_out,
           norm_cross, norm_mem, w_xq, w_xk, w_xv, w_xo, xq_norm, xk_norm, norm_ffn, w_router, b_router,
           w_gu, b_gu, w_down, b_down):
    bp, sp, _ = x_prompt.shape
    bs, ts, _ = x_sample.shape
    past = cache_fox_k.shape[2]
    n_p, n_s = bp * sp, bs * ts
    l = 0

    w_in_p = _pack_cols(w_in[l]).astype(BF16)
    mu_p = _pack_cols(jnp.pad(rwkv_mu[l][None, :], ((0, 0), (0, w_in.shape[2] - A_COLS))))
    zero_lora = jnp.zeros((DECAY_LORA, RWKV_WIDTH), F32)
    prep = {
        "mu_m": mu_p[:, 0:3072], "mu_g": mu_p[:, C_GL:C_GL + 256], "mu_w": mu_p[:, C_WA:C_WA + 128],
        "w0": rwkv_w0[l][None], "a0": rwkv_a0[l][None],
        "w2": jnp.concatenate([rwkv_w2[l], zero_lora], axis=0).astype(BF16),
        "a2": jnp.concatenate([zero_lora, rwkv_a2[l]], axis=0).astype(BF16),
        "g2": jnp.pad(rwkv_g2[l], ((0, 256 - GATE_LORA), (0, 0))).astype(BF16),
        "k_k": rwkv_k_k[l][None], "k_a": rwkv_k_a[l][None], "r_k": rwkv_r_k[l].reshape(1, RWKV_WIDTH),
    }
    lnw, lnb = rwkv_lnx_w[l][None], rwkv_lnx_b[l][None]
    qg = jnp.tile(fox_q_norm[l], FOX_HEADS)[None]
    kg = jnp.tile(fox_k_norm[l], FOX_HEADS)[None]
    bf = _pad_cols(fox_b_f[l][None], LANES)
    w_out_b = w_out[l].astype(BF16)
    w_kv = jnp.concatenate([w_xk[l], w_xv[l]], axis=1).astype(BF16)
    pc = {
        "g_cross": norm_cross[l][None], "w_xq": w_xq[l].astype(BF16), "xq_gain": xq_norm[l][None],
        "w_xo": w_xo[l].astype(BF16), "g_ffn": norm_ffn[l][None],
        "w_router": _pad_cols(w_router[l], LANES),
        "b_router": jnp.concatenate([b_router[l], jnp.full((LANES - N_EXPERTS,), NEG_BIG, F32)])[None],
    }

    def mixers(x, nseq, t, first_shift, s0, tm_mm, tm_prep, tb):
        z = _norm_mm(x, norm_mix[l][None], w_in_p, tm_mm, 768)
        z3 = z.reshape(nseq, t, Z_COLS)
        pm, pg, pw = _prev_rows(z3, tm_prep, first_shift)
        *steps, g, bonus = _rwkv_prep(z, pm, pg, pw, prep, tm_prep)
        steps = [a.reshape(nseq, t, HEAD_PAIRS, LANES) for a in steps]
        ya, s_fin = _rwkv_scan(steps, g.reshape(nseq, t, RWKV_WIDTH), bonus.reshape(nseq, t, RWKV_WIDTH),
                               _state_to_pairs(s0), lnw, lnb, tb)
        qb, kb, kf, lf = _fox_proj(z, qg, kg, bf, tm_prep)
        return z, z3, ya.reshape(nseq * t, RWKV_WIDTH), _pairs_to_state(s_fin), qb, kb, kf, lf[:, :FOX_HEADS]

    zero_shift = jnp.zeros((bp, 1, Z_COLS), F32)
    zero_state = jnp.zeros((bp, RWKV_HEADS, HEAD_DIM, HEAD_DIM), F32)
    xp = x_prompt.reshape(n_p, D_MODEL)
    xs = x_sample.reshape(n_s, D_MODEL)
    zp, zp3, ya_p, st_p, qb_p, kb_p, kf_p, lf_p = mixers(xp, bp, sp, zero_shift, zero_state, 1024, 256,
                                                         SCAN_BLOCK)
    shift_s = _pack_cols(jnp.pad(state_shift[l].reshape(bs, A_COLS), ((0, 0), (0, w_in.shape[2] - A_COLS))))
    zs, zs3, ya_s, st_s, qb_s, kb_s, kf_s, lf_s = mixers(xs, bs, ts, shift_s.reshape(bs, 1, Z_COLS),
                                                         state_rwkv[l], 256, ts, ts)

    fp = jnp.cumsum(lf_p.reshape(bp, sp, FOX_HEADS), axis=1) * LOG2E
    q_aug, k_aug, vt = _augment(qb_p, kb_p, zp[:, C_FV:C_FV + FOX_WIDTH], fp, bp, sp)
    yb_p = _fox_prompt(q_aug, k_aug, vt, zp, bp, sp, min(1024, sp))

    fs = jnp.cumsum(jnp.concatenate([cache_fox_logf[l].astype(F32), lf_s.reshape(bs, ts, FOX_HEADS)], axis=1),
                    axis=1) * LOG2E
    fn_s = fs[:, past:].transpose(0, 2, 1)
    yb_s = _fox_sample(qb_s, cache_fox_k[l], cache_fox_v[l], kb_s, zs, fn_s.reshape(bs, FOX_HEADS * ts, 1),
                       fs[:, :past].transpose(0, 2, 1), fn_s, bs, ts, min(512, past))

    x1_p = _out_proj(xp, ya_p, yb_p, w_out_b, 512)
    x1_s = _out_proj(xs, ya_s, yb_s, w_out_b, 256)

    kv_p = _mem_kv(mem_prompt.reshape(bp * N_MEM, D_MODEL), norm_mem[l][None], w_kv, xk_norm[l][None], 256)
    mk_p = kv_p[:, :MEM_WIDTH].reshape(bp, N_MEM, MEM_WIDTH)
    mv_p = kv_p[:, MEM_WIDTH:].reshape(bp, N_MEM, MEM_WIDTH)
    x2_p, h_p, ti_p, tg_p = _cross_router(x1_p, mk_p.astype(BF16), mv_p.astype(BF16), pc, bp, 256)
    mk_s = cache_mem_k[l].reshape(bs, N_MEM, MEM_WIDTH).astype(BF16)
    mv_s = cache_mem_v[l].reshape(bs, N_MEM, MEM_WIDTH).astype(BF16)
    x2_s, h_s, ti_s, tg_s = _cross_router(x1_s, mk_s, mv_s, pc, bs, ts)

    h_all = jnp.concatenate([h_p, h_s], axis=0)
    top_i = jnp.concatenate([ti_p[:, :TOP_K], ti_s[:, :TOP_K]], axis=0)
    gates = jnp.concatenate([tg_p[:, :TOP_K], tg_s[:, :TOP_K]], axis=0)
    row_token, pos, tile_expert, tile_valid = _route(top_i)
    xs_rows = h_all[row_token]
    out_rows = _moe_experts(tile_expert, tile_valid, xs_rows, w_gu[l], b_gu[l][:, None, :], w_down[l],
                            b_down[l][:, None, :])
    f = jnp.sum(out_rows[pos] * gates[:, :, None], axis=1)
    y_p = (x2_p + f[:n_p]).reshape(bp, sp, D_MODEL)
    y_s = (x2_s + f[n_p:]).reshape(bs, ts, D_MODEL)

    def caches(z3, kf, lf, nseq, t):
        return (kf.reshape(1, nseq, t, FOX_HEADS, HEAD_DIM),
                z3[..., C_FV:C_FV + FOX_WIDTH].reshape(1, nseq, t, FOX_HEADS, HEAD_DIM),
                lf.reshape(1, nseq, t, FOX_HEADS))

    pk, pv, plf = caches(zp3, kf_p, lf_p, bp, sp)
    sk, sv, slf = caches(zs3, kf_s, lf_s, bs, ts)
    return (y_p, y_s, pk, pv, plf, st_p[None], _unpack_a_cols(zp3[:, -1:])[None],
            mk_p.reshape(1, bp, N_MEM, MEM_HEADS, MEM_HEAD_DIM), mv_p.reshape(1, bp, N_MEM, MEM_HEADS, MEM_HEAD_DIM),
            sk, sv, slf, st_s[None], _unpack_a_cols(zs3[:, -1:])[None])
```

```python
import functools

import jax
import jax.numpy as jnp
from jax import lax
from jax.experimental import pallas as pl
from jax.experimental.pallas import tpu as pltpu

F32 = jnp.float32
BF16 = jnp.bfloat16

D_MODEL = 2048
HEAD_DIM = 64
RWKV_WIDTH = 1024
FOX_WIDTH = 1024
RWKV_HEADS = 16
FOX_HEADS = 16
DECAY_LORA = 64
AAA_LORA = 64
GATE_LORA = 160
A_COLS = 3 * RWKV_WIDTH + DECAY_LORA + AAA_LORA + GATE_LORA
N_MEM = 256
MEM_HEADS = 4
MEM_HEAD_DIM = 128
MEM_WIDTH = 512
N_EXPERTS = 32
TOP_K = 4
D_FF = 2048
SWIGLU_LIMIT = 7.0
SWIGLU_ALPHA = 1.702
RMS_EPS = 1e-6
GN_EPS = 64e-5

LANES = 128
HEAD_PAIRS = 8

C_R, C_K, C_V = 0, 1024, 2048
C_FQ, C_FK, C_FV, C_FG = 3072, 4096, 5120, 6144
C_GL = 7168
C_WA = 7424
C_FL = 7552
Z_COLS = 7680

VMEM_LIMIT = 56 * 1024 * 1024
NEG_BIG = -1e30
LOG2E = 1.4426950408889634


def _cparams(sem):
    return pltpu.CompilerParams(dimension_semantics=sem, vmem_limit_bytes=VMEM_LIMIT)


def _block_ones():
    r = lax.broadcasted_iota(jnp.int32, (LANES, LANES), 0) // HEAD_DIM
    c = lax.broadcasted_iota(jnp.int32, (LANES, LANES), 1) // HEAD_DIM
    return (r == c).astype(BF16)


def _seg_sum64(x):
    ones = _block_ones()
    outs = []
    for s in range(x.shape[-1] // LANES):
        xs = x[:, s * LANES:(s + 1) * LANES]
        hi = xs.astype(BF16)
        lo = (xs - hi.astype(F32)).astype(BF16)
        outs.append(jnp.dot(hi, ones, preferred_element_type=F32)
                    + jnp.dot(lo, ones, preferred_element_type=F32))
    return outs[0] if len(outs) == 1 else jnp.concatenate(outs, axis=-1)


def _softplus(x):
    return jnp.maximum(x, 0.0) + jnp.log(1.0 + jnp.exp(-jnp.abs(x)))


def _sigmoid(x):
    return 1.0 / (1.0 + jnp.exp(-x))


def _norm_mm_kernel(x_ref, g_ref, w_ref, o_ref, xn_ref):
    @pl.when(pl.program_id(1) == 0)
    def _():
        x = x_ref[...]
        ms = jnp.mean(x * x, axis=-1, keepdims=True)
        xn_ref[...] = (x * lax.rsqrt(ms + RMS_EPS) * g_ref[...]).astype(BF16)

    o_ref[...] = jnp.dot(xn_ref[...], w_ref[...], preferred_element_type=F32)


def _norm_mm(x, g, w, tm, tn):
    m, k = x.shape
    n = w.shape[1]
    return pl.pallas_call(
        _norm_mm_kernel,
        grid=(m // tm, n // tn),
        in_specs=[pl.BlockSpec((tm, k), lambda i, j: (i, 0)),
                  pl.BlockSpec((1, k), lambda i, j: (0, 0)),
                  pl.BlockSpec((k, tn), lambda i, j: (0, j))],
        out_specs=pl.BlockSpec((tm, tn), lambda i, j: (i, j)),
        out_shape=jax.ShapeDtypeStruct((m, n), F32),
        scratch_shapes=[pltpu.VMEM((tm, k), BF16)],
        compiler_params=_cparams(("parallel", "arbitrary")),
        name="norm_mm",
    )(x, g, w)


def _rwkv_prep_kernel(zm_ref, zg_ref, zw_ref, pm_ref, pg_ref, pw_ref,
                      mum_ref, mug_ref, muw_ref, w0_ref, w2_ref, a0_ref, a2_ref, g2_ref,
                      kk_ref, ka_ref, rk_ref,
                      r_o, d_o, k_o, v_o, kk_o, b_o, g_o, bonus_o):
    def shifted(z_ref, p_ref, mu_ref):
        z = z_ref[...]
        rolled = pltpu.roll(z, 1, 0)
        row = lax.broadcasted_iota(jnp.int32, z.shape, 0)
        prev = jnp.where(row == 0, p_ref[0], rolled)
        return z + mu_ref[...] * (prev - z)

    zm = shifted(zm_ref, pm_ref, mum_ref)
    zg = shifted(zg_ref, pg_ref, mug_ref)
    zw = shifted(zw_ref, pw_ref, muw_ref)
    r = zm[:, C_R:C_R + RWKV_WIDTH]
    k = zm[:, C_K:C_K + RWKV_WIDTH]
    v = zm[:, C_V:C_V + RWKV_WIDTH]
    lw = jnp.dot(jnp.tanh(zw).astype(BF16), w2_ref[...], preferred_element_type=F32)
    la = jnp.dot(zw.astype(BF16), a2_ref[...], preferred_element_type=F32)
    g = jnp.dot(_sigmoid(zg).astype(BF16), g2_ref[...], preferred_element_type=F32)
    w_log = -_softplus(-(w0_ref[...] + lw)) - 0.5
    a = _sigmoid(a0_ref[...] + la)
    kk = k * kk_ref[...]
    kk = kk * lax.rsqrt(jnp.maximum(_seg_sum64(kk * kk), 1e-24))
    kh = k * (1.0 + (a - 1.0) * ka_ref[...])
    tiles = lambda x: x.reshape(x.shape[0], HEAD_PAIRS, LANES)
    r_o[...] = tiles(r)
    d_o[...] = tiles(jnp.exp(-jnp.exp(w_log)))
    k_o[...] = tiles(kh)
    v_o[...] = tiles(v)
    kk_o[...] = tiles(kk)
    b_o[...] = tiles(kk * a)
    g_o[...] = g
    bonus_o[...] = _seg_sum64(r * kh * rk_ref[...]) * v


def _rwkv_prep(z, prev_m, prev_g, prev_w, pr, tm):
    n = z.shape[0]
    row = lambda w, c: pl.BlockSpec((tm, w), lambda i, c=c: (i, c))
    prev = lambda w: pl.BlockSpec((1, 1, w), lambda i: (i, 0, 0))
    full = lambda a: pl.BlockSpec(a.shape, lambda i: (0,) * a.ndim)
    params = [pr["mu_m"], pr["mu_g"], pr["mu_w"], pr["w0"], pr["w2"], pr["a0"], pr["a2"], pr["g2"],
              pr["k_k"], pr["k_a"], pr["r_k"]]
    tiled = jax.ShapeDtypeStruct((n, HEAD_PAIRS, LANES), F32)
    flat = jax.ShapeDtypeStruct((n, RWKV_WIDTH), F32)
    return pl.pallas_call(
        _rwkv_prep_kernel,
        grid=(n // tm,),
        in_specs=[row(3072, 0), row(256, C_GL // 256), row(128, C_WA // 128),
                  prev(3072), prev(256), prev(128)] + [full(a) for a in params],
        out_specs=[pl.BlockSpec((tm, HEAD_PAIRS, LANES), lambda i: (i, 0, 0))] * 6
        + [pl.BlockSpec((tm, RWKV_WIDTH), lambda i: (i, 0))] * 2,
        out_shape=[tiled] * 6 + [flat] * 2,
        compiler_params=_cparams(("parallel",)),
        name="rwkv_prep",
    )(z, z, z, prev_m, prev_g, prev_w, *params)


SEQ_PER_STEP = 2
SCAN_BLOCK = 128


def _rwkv_scan_kernel(r_ref, d_ref, k_ref, v_ref, kk_ref, b_ref, g_ref, bonus_ref, s0_ref,
                      lw_ref, lb_ref,
                      y_ref, sf_ref,
                      s_ref, oa_ref, *, tb):
    tblk = pl.program_id(1)

    @pl.when(tblk == 0)
    def _():
        s_ref[...] = s0_ref[...]

    oa_ref[...] = jnp.zeros_like(oa_ref)
    r2 = lax.broadcasted_iota(jnp.int32, (2 * LANES, 2 * LANES), 0) // HEAD_DIM
    c2 = lax.broadcasted_iota(jnp.int32, (2 * LANES, 2 * LANES), 1) // HEAD_DIM
    ones = (r2 == c2).astype(BF16)
    lane = lax.broadcasted_iota(jnp.int32, (HEAD_DIM, LANES), 1)
    sub = lax.broadcasted_iota(jnp.int32, (HEAD_DIM, LANES), 0)
    diag = (lane % HEAD_DIM) == sub
    tiles = [(s, hp) for s in range(SEQ_PER_STEP) for hp in range(HEAD_PAIRS)]

    def bcast(ref, s, t, hp):
        return jnp.broadcast_to(ref[s, t, hp:hp + 1, :], (HEAD_DIM, LANES))

    def seg_sums(parts):
        lhs = jnp.concatenate([jnp.concatenate(parts[j:j + 2], axis=1) for j in range(0, len(parts), 2)], axis=0)
        out = jnp.dot(lhs, ones, preferred_element_type=F32)
        return [out[(j // 2) * HEAD_DIM:(j // 2 + 1) * HEAD_DIM, (j % 2) * LANES:(j % 2 + 1) * LANES]
                for j in range(len(parts))]

    def emit_outputs(t_out, valid):
        half = pl.multiple_of((t_out // HEAD_DIM) * HEAD_DIM, HEAD_DIM)
        hit = jnp.logical_and((lane % HEAD_DIM) == (t_out % HEAD_DIM), valid)
        ob = seg_sums([(s_ref[s, hp] * bcast(r_ref, s, t_out, hp)).astype(BF16) for s, hp in tiles])
        for j, (s, hp) in enumerate(tiles):
            cur = oa_ref[s, hp, pl.ds(half, HEAD_DIM), :]
            oa_ref[s, hp, pl.ds(half, HEAD_DIM), :] = jnp.where(hit, ob[j], cur)

    def step(t, carry):
        emit_outputs(jnp.maximum(t - 1, 0), t > 0)
        skk = seg_sums([(s_ref[s, hp] * bcast(kk_ref, s, t, hp)).astype(BF16) for s, hp in tiles])
        vb = seg_sums([jnp.where(diag, bcast(v_ref, s, t, hp), 0.0).astype(BF16) for s, hp in tiles])
        for j, (s, hp) in enumerate(tiles):
            s_ref[s, hp] = (s_ref[s, hp] * bcast(d_ref, s, t, hp)
                            - skk[j] * bcast(b_ref, s, t, hp)
                            + vb[j] * bcast(k_ref, s, t, hp))
        return carry

    lax.fori_loop(0, tb, step, 0)
    emit_outputs(tb - 1, True)

    lane_t = lax.broadcasted_iota(jnp.int32, (HEAD_DIM, LANES), 1)
    low = lane_t < HEAD_DIM
    for s in range(SEQ_PER_STEP):
        slabs = []
        for hp in range(HEAD_PAIRS):
            mt = oa_ref[s, hp].T
            ro = pltpu.roll(mt, HEAD_DIM, 1)
            top = jnp.where(low, mt[:HEAD_DIM], ro[HEAD_DIM:])
            bot = jnp.where(low, ro[:HEAD_DIM], mt[HEAD_DIM:])
            slabs.append(jnp.concatenate([top, bot], axis=0)[:tb])
        o = jnp.concatenate(slabs, axis=-1)
        mean = _seg_sum64(o) * (1.0 / HEAD_DIM)
        cen = o - mean
        var = _seg_sum64(cen * cen) * (1.0 / HEAD_DIM)
        on = cen * lax.rsqrt(var + GN_EPS) * lw_ref[...] + lb_ref[...]
        y_ref[s] = ((on + bonus_ref[s]) * g_ref[s]).astype(y_ref.dtype)

    @pl.when(tblk == pl.num_programs(1) - 1)
    def _():
        sf_ref[...] = s_ref[...]


def _rwkv_scan(steps, g, bonus, s0, lnw, lnb, tb):
    nseq, t = g.shape[:2]
    step_spec = pl.BlockSpec((SEQ_PER_STEP, tb, HEAD_PAIRS, LANES), lambda i, j: (i, j, 0, 0))
    seq_spec = pl.BlockSpec((SEQ_PER_STEP, tb, RWKV_WIDTH), lambda i, j: (i, j, 0))
    st_spec = pl.BlockSpec((SEQ_PER_STEP, HEAD_PAIRS, HEAD_DIM, LANES), lambda i, j: (i, 0, 0, 0))
    par_spec = pl.BlockSpec((1, RWKV_WIDTH), lambda i, j: (0, 0))
    return pl.pallas_call(
        functools.partial(_rwkv_scan_kernel, tb=tb),
        grid=(nseq // SEQ_PER_STEP, t // tb),
        in_specs=[step_spec] * 6 + [seq_spec] * 2 + [st_spec] + [par_spec] * 2,
        out_specs=[seq_spec, st_spec],
        out_shape=[jax.ShapeDtypeStruct((nseq, t, RWKV_WIDTH), BF16),
                   jax.ShapeDtypeStruct(s0.shape, F32)],
        scratch_shapes=[pltpu.VMEM((SEQ_PER_STEP, HEAD_PAIRS, HEAD_DIM, LANES), F32),
                        pltpu.VMEM((SEQ_PER_STEP, HEAD_PAIRS, LANES, LANES), F32)],
        compiler_params=_cparams(("parallel", "arbitrary")),
        name="rwkv_scan",
    )(*steps, g, bonus, s0, lnw, lnb)


def _state_to_pairs(s):
    b = s.shape[0]
    return s.reshape(b, HEAD_PAIRS, 2, HEAD_DIM, HEAD_DIM).transpose(0, 1, 3, 2, 4).reshape(
        b, HEAD_PAIRS, HEAD_DIM, LANES)


def _pairs_to_state(s):
    b = s.shape[0]
    return s.reshape(b, HEAD_PAIRS, HEAD_DIM, 2, HEAD_DIM).transpose(0, 1, 3, 2, 4).reshape(
        b, RWKV_HEADS, HEAD_DIM, HEAD_DIM)


def _fox_proj_kernel(q_ref, k_ref, v_ref, fl_ref, qg_ref, kg_ref, bf_ref, qb_o, kb_o, lf_o, *cache_o, dim_major):
    def headnorm(x, g):
        ms = _seg_sum64(x * x) * (1.0 / HEAD_DIM)
        return x * lax.rsqrt(ms + RMS_EPS) * g

    q = headnorm(q_ref[...], qg_ref[...])
    k = headnorm(k_ref[...], kg_ref[...])
    qb_o[...] = (q * (HEAD_DIM ** -0.5 * LOG2E)).astype(BF16)
    kb_o[...] = k.astype(BF16)
    lf = -_softplus(-(fl_ref[...] + bf_ref[...]))
    lf_o[...] = lf
    if dim_major:
        kt_o, vt_o, vtb_o, lft_o = cache_o
        vt = v_ref[...].T
        kt_o[0] = k.T
        vt_o[0] = vt
        vtb_o[0] = vt.astype(BF16)
        lft_o[0] = lf.T[:FOX_HEADS]
    else:
        cache_o[0][...] = k


def _fox_proj(z, qg, kg, bf, tm, nseq=None):
    n = z.shape[0]
    col = lambda w, c: pl.BlockSpec((tm, w), lambda i, c=c: (i, c))
    par = lambda w: pl.BlockSpec((1, w), lambda i: (0, 0))
    wide = pl.BlockSpec((tm, FOX_WIDTH), lambda i: (i, 0))
    out_specs = [wide, wide, pl.BlockSpec((tm, LANES), lambda i: (i, 0))]
    out_shape = [jax.ShapeDtypeStruct((n, FOX_WIDTH), BF16), jax.ShapeDtypeStruct((n, FOX_WIDTH), BF16),
                 jax.ShapeDtypeStruct((n, LANES), F32)]
    if nseq is None:
        out_specs.append(wide)
        out_shape.append(jax.ShapeDtypeStruct((n, FOX_WIDTH), F32))
    else:
        t = n // nseq
        per = t // tm
        tall = pl.BlockSpec((1, FOX_WIDTH, tm), lambda i: (i // per, 0, i % per))
        out_specs += [tall, tall, tall, pl.BlockSpec((1, FOX_HEADS, tm), lambda i: (i // per, 0, i % per))]
        out_shape += [jax.ShapeDtypeStruct((nseq, FOX_WIDTH, t), F32), jax.ShapeDtypeStruct((nseq, FOX_WIDTH, t), F32),
                      jax.ShapeDtypeStruct((nseq, FOX_WIDTH, t), BF16),
                      jax.ShapeDtypeStruct((nseq, FOX_HEADS, t), F32)]
    return pl.pallas_call(
        functools.partial(_fox_proj_kernel, dim_major=nseq is not None),
        grid=(n // tm,),
        in_specs=[col(1024, C_FQ // 1024), col(1024, C_FK // 1024), col(1024, C_FV // 1024), col(128, C_FL // 128),
                  par(1024), par(1024), par(128)],
        out_specs=out_specs,
        out_shape=out_shape,
        compiler_params=_cparams(("parallel",)),
        name="fox_proj",
    )(z, z, z, z, qg, kg, bf)


def _fox_prompt_kernel(qi_ref, kj_ref, q_ref, k_ref, vt_ref, gate_ref, y_ref, m_ref, l_ref, acc_ref, *, tq):
    pair = pl.program_id(2)
    qi = qi_ref[pair]
    kj = kj_ref[pair]

    @pl.when(kj == 0)
    def _():
        m_ref[...] = jnp.full_like(m_ref, NEG_BIG)
        l_ref[...] = jnp.zeros_like(l_ref)
        acc_ref[...] = jnp.zeros_like(acc_ref)

    def body(masked):
        for h in range(2):
            st = lax.dot_general(k_ref[0, h], q_ref[0, h], (((1,), (1,)), ((), ())), preferred_element_type=F32)
            if masked:
                key = lax.broadcasted_iota(jnp.int32, st.shape, 0)
                qry = lax.broadcasted_iota(jnp.int32, st.shape, 1)
                st = jnp.where(key <= qry, st, NEG_BIG)
            m_prev = m_ref[h]
            m_new = jnp.maximum(m_prev, jnp.max(st, axis=0, keepdims=True))
            alpha = jnp.exp2(m_prev - m_new)
            p = jnp.exp2(st - m_new)
            l_ref[h] = alpha * l_ref[h] + jnp.sum(p, axis=0, keepdims=True)
            acc_ref[h] = alpha * acc_ref[h] + jnp.dot(vt_ref[0, h], p.astype(BF16), preferred_element_type=F32)
            m_ref[h] = m_new

    @pl.when(kj < qi)
    def _():
        body(False)

    @pl.when(kj == qi)
    def _():
        body(True)
        ot = jnp.concatenate([acc_ref[0] / l_ref[0], acc_ref[1] / l_ref[1]], axis=0)
        y_ref[...] = (ot.T * _sigmoid(gate_ref[...])).astype(y_ref.dtype)


def _fox_prompt(q_aug, k_aug, vt, z, nb, seq, tq):
    nq = seq // tq
    pairs = [(i, j) for i in range(nq) for j in range(i + 1)]
    qi_tab = jnp.asarray([p[0] for p in pairs], jnp.int32)
    kj_tab = jnp.asarray([p[1] for p in pairs], jnp.int32)
    grid_spec = pltpu.PrefetchScalarGridSpec(
        num_scalar_prefetch=2,
        grid=(nb, HEAD_PAIRS, len(pairs)),
        in_specs=[pl.BlockSpec((1, 2, tq, LANES), lambda b, hp, p, qi, kj: (b, hp, qi[p], 0)),
                  pl.BlockSpec((1, 2, tq, LANES), lambda b, hp, p, qi, kj: (b, hp, kj[p], 0)),
                  pl.BlockSpec((1, 2, HEAD_DIM, tq), lambda b, hp, p, qi, kj: (b, hp, 0, kj[p])),
                  pl.BlockSpec((tq, LANES), lambda b, hp, p, qi, kj: (b * nq + qi[p], C_FG // LANES + hp))],
        out_specs=pl.BlockSpec((tq, LANES), lambda b, hp, p, qi, kj: (b * nq + qi[p], hp)),
        scratch_shapes=[pltpu.VMEM((2, 1, tq), F32), pltpu.VMEM((2, 1, tq), F32),
                        pltpu.VMEM((2, HEAD_DIM, tq), F32)],
    )
    return pl.pallas_call(
        functools.partial(_fox_prompt_kernel, tq=tq),
        grid_spec=grid_spec,
        out_shape=jax.ShapeDtypeStruct((nb * seq, FOX_WIDTH), BF16),
        compiler_params=_cparams(("parallel", "parallel", "arbitrary")),
        name="fox_prompt",
    )(qi_tab, kj_tab, q_aug, k_aug, vt, z)


def _split3(x):
    def top(v):
        bits = lax.bitcast_convert_type(v, jnp.uint32) & jnp.uint32(0xFFFF0000)
        return lax.bitcast_convert_type(bits, F32)

    hi = top(x)
    mid = top(x - hi)
    lo = top(x - hi - mid)
    return hi.astype(BF16), mid.astype(BF16), lo.astype(BF16)


def _augment(qb, kb, f2, nb, seq):
    heads = lambda a: a.reshape(nb, seq, FOX_HEADS, HEAD_DIM).transpose(0, 2, 1, 3)
    fh = f2[..., None]
    one = jnp.ones_like(fh, BF16)
    pad = jnp.zeros((nb, FOX_HEADS, seq, LANES - HEAD_DIM - 6), BF16)
    fq3 = _split3(fh)
    fk3 = _split3(-fh)
    q_aug = jnp.concatenate([heads(qb), *fq3, one, one, one, pad], axis=-1)
    k_aug = jnp.concatenate([heads(kb), one, one, one, *fk3, pad], axis=-1)
    return q_aug, k_aug


def _fox_sample_kernel(q_ref, kc_ref, vc_ref, kn_ref, vn_ref, fq_ref, fkc_ref, fkn_ref, gate_ref, y_ref,
                       m_ref, l_ref, acc_ref, *, nk):
    kj = pl.program_id(1)

    @pl.when(kj == 0)
    def _():
        m_ref[...] = jnp.full_like(m_ref, NEG_BIG)
        l_ref[...] = jnp.zeros_like(l_ref)
        acc_ref[...] = jnp.zeros_like(acc_ref)

    def head_slice(ref, h):
        return ref[:, h * HEAD_DIM:(h + 1) * HEAD_DIM]

    t = q_ref.shape[0]

    nt_dims = (((1,), (1,)), ((), ()))

    def attend(ks, vs, fk_ref, masked, dim_major):
        if dim_major:
            qk = lambda h: jnp.dot(head_slice(q_ref, h), ks[h], preferred_element_type=F32)
            pv_of = lambda ph, h: lax.dot_general(ph, vs[h], nt_dims, preferred_element_type=F32)
        else:
            qk = lambda h: lax.dot_general(head_slice(q_ref, h), ks[h], nt_dims, preferred_element_type=F32)
            pv_of = lambda ph, h: jnp.dot(ph, vs[h], preferred_element_type=F32)
        s = jnp.concatenate([qk(h) for h in range(FOX_HEADS)], axis=0)
        fk = jnp.concatenate([jnp.broadcast_to(fk_ref[0, h:h + 1, :], (t, s.shape[1])) for h in range(FOX_HEADS)],
                             axis=0)
        s = s + fq_ref[0] - fk
        if masked:
            qry = lax.broadcasted_iota(jnp.int32, s.shape, 0) % t
            key = lax.broadcasted_iota(jnp.int32, s.shape, 1)
            s = jnp.where(key <= qry, s, NEG_BIG)
        m_prev = m_ref[...]
        m_new = jnp.maximum(m_prev, jnp.max(s, axis=-1, keepdims=True))
        alpha = jnp.exp2(m_prev - m_new)
        p = jnp.exp2(s - m_new)
        l_ref[...] = alpha * l_ref[...] + jnp.sum(p, axis=-1, keepdims=True)
        p = p.astype(BF16)
        pv = jnp.concatenate([pv_of(p[h * t:(h + 1) * t], h) for h in range(FOX_HEADS)], axis=0)
        acc_ref[...] = alpha * acc_ref[...] + pv
        m_ref[...] = m_new

    @pl.when(kj < nk)
    def _():
        attend([kc_ref[0, h].astype(BF16) for h in range(FOX_HEADS)],
               [vc_ref[0, h].astype(BF16) for h in range(FOX_HEADS)], fkc_ref, False, True)

    @pl.when(kj == nk)
    def _():
        attend([head_slice(kn_ref, h) for h in range(FOX_HEADS)],
               [head_slice(vn_ref, h).astype(BF16) for h in range(FOX_HEADS)], fkn_ref, True, False)
        o = acc_ref[...] / l_ref[...]
        for h in range(FOX_HEADS):
            gate = _sigmoid(head_slice(gate_ref, h))
            y_ref[:, h * HEAD_DIM:(h + 1) * HEAD_DIM] = (o[h * t:(h + 1) * t] * gate).astype(y_ref.dtype)


def _fox_sample(qb, kc, vc, kb, z, fq, fkc, fkn, nb, t, tk):
    past = kc.shape[3]
    nk = past // tk
    last = nk - 1
    wide = lambda c: pl.BlockSpec((t, FOX_WIDTH), lambda b, j, c=c: (b, c))
    cache = pl.BlockSpec((1, FOX_HEADS, HEAD_DIM, tk), lambda b, j: (b, 0, 0, jnp.minimum(j, last)))
    return pl.pallas_call(
        functools.partial(_fox_sample_kernel, nk=nk),
        grid=(nb, nk + 1),
        in_specs=[wide(0), cache, cache, wide(0), wide(C_FV // FOX_WIDTH),
                  pl.BlockSpec((1, FOX_HEADS * t, 1), lambda b, j: (b, 0, 0)),
                  pl.BlockSpec((1, FOX_HEADS, tk), lambda b, j: (b, 0, jnp.minimum(j, last))),
                  pl.BlockSpec((1, FOX_HEADS, t), lambda b, j: (b, 0, 0)),
                  wide(C_FG // FOX_WIDTH)],
        out_specs=wide(0),
        out_shape=jax.ShapeDtypeStruct((nb * t, FOX_WIDTH), BF16),
        scratch_shapes=[pltpu.VMEM((FOX_HEADS * t, 1), F32), pltpu.VMEM((FOX_HEADS * t, 1), F32),
                        pltpu.VMEM((FOX_HEADS * t, HEAD_DIM), F32)],
        compiler_params=_cparams(("parallel", "arbitrary")),
        name="fox_sample",
    )(qb, kc, vc, kb, z, fq, fkc, fkn, z)


def _out_proj_kernel(x_ref, ya_ref, yb_ref, wa_ref, wb_ref, o_ref):
    o_ref[...] = (x_ref[...]
                  + jnp.dot(ya_ref[...], wa_ref[...], preferred_element_type=F32)
                  + jnp.dot(yb_ref[...], wb_ref[...], preferred_element_type=F32))


def _out_proj(x, ya, yb, w, tm):
    n = x.shape[0]
    return pl.pallas_call(
        _out_proj_kernel,
        grid=(n // tm,),
        in_specs=[pl.BlockSpec((tm, D_MODEL), lambda i: (i, 0)),
                  pl.BlockSpec((tm, RWKV_WIDTH), lambda i: (i, 0)),
                  pl.BlockSpec((tm, FOX_WIDTH), lambda i: (i, 0)),
                  pl.BlockSpec((RWKV_WIDTH, D_MODEL), lambda i: (0, 0)),
                  pl.BlockSpec((FOX_WIDTH, D_MODEL), lambda i: (1, 0))],
        out_specs=pl.BlockSpec((tm, D_MODEL), lambda i: (i, 0)),
        out_shape=jax.ShapeDtypeStruct((n, D_MODEL), F32),
        compiler_params=_cparams(("parallel",)),
        name="out_proj",
    )(x, ya, yb, w, w)


def _mem_kv_kernel(x_ref, g_ref, w_ref, kg_ref, o_ref):
    x = x_ref[...]
    ms = jnp.mean(x * x, axis=-1, keepdims=True)
    xn = (x * lax.rsqrt(ms + RMS_EPS) * g_ref[...]).astype(BF16)
    kv = jnp.dot(xn, w_ref[...], preferred_element_type=F32)
    for h in range(MEM_HEADS):
        kh = kv[:, h * MEM_HEAD_DIM:(h + 1) * MEM_HEAD_DIM]
        ms = jnp.mean(kh * kh, axis=-1, keepdims=True)
        o_ref[:, h * MEM_HEAD_DIM:(h + 1) * MEM_HEAD_DIM] = kh * lax.rsqrt(ms + RMS_EPS) * kg_ref[...]
    o_ref[:, MEM_WIDTH:] = kv[:, MEM_WIDTH:]


def _mem_kv(mem, g, w_kv, kg, tm):
    n = mem.shape[0]
    return pl.pallas_call(
        _mem_kv_kernel,
        grid=(n // tm,),
        in_specs=[pl.BlockSpec((tm, D_MODEL), lambda i: (i, 0)),
                  pl.BlockSpec((1, D_MODEL), lambda i: (0, 0)),
                  pl.BlockSpec((D_MODEL, 2 * MEM_WIDTH), lambda i: (0, 0)),
                  pl.BlockSpec((1, MEM_HEAD_DIM), lambda i: (0, 0))],
        out_specs=pl.BlockSpec((tm, 2 * MEM_WIDTH), lambda i: (i, 0)),
        out_shape=jax.ShapeDtypeStruct((n, 2 * MEM_WIDTH), F32),
        compiler_params=_cparams(("parallel",)),
        name="mem_kv",
    )(mem, g, w_kv, kg)


def _cross_router_kernel(x_ref, gc_ref, wq_ref, qg_ref, mk_ref, mv_ref, wo_ref,
                         gf_ref, wr_ref, br_ref,
                         x2_ref, h_ref, idx_ref, gate_ref):
    x = x_ref[...]
    ms = jnp.mean(x * x, axis=-1, keepdims=True)
    xn = (x * lax.rsqrt(ms + RMS_EPS) * gc_ref[...]).astype(BF16)
    q = jnp.dot(xn, wq_ref[...], preferred_element_type=F32)
    outs = []
    for h in range(MEM_HEADS):
        sl = slice(h * MEM_HEAD_DIM, (h + 1) * MEM_HEAD_DIM)
        qh = q[:, sl]
        qms = jnp.mean(qh * qh, axis=-1, keepdims=True)
        qh = (qh * lax.rsqrt(qms + RMS_EPS) * qg_ref[...]).astype(BF16)
        s = lax.dot_general(qh, mk_ref[0, :, sl], (((1,), (1,)), ((), ())), preferred_element_type=F32)
        s = s * (MEM_HEAD_DIM ** -0.5)
        p = jnp.exp(s - jnp.max(s, axis=-1, keepdims=True))
        p = p / jnp.sum(p, axis=-1, keepdims=True)
        outs.append(jnp.dot(p.astype(BF16), mv_ref[0, :, sl], preferred_element_type=F32))
    o = jnp.concatenate(outs, axis=-1).astype(BF16)
    x2 = x + jnp.dot(o, wo_ref[...], preferred_element_type=F32)
    x2_ref[...] = x2

    ms2 = jnp.mean(x2 * x2, axis=-1, keepdims=True)
    hn = x2 * lax.rsqrt(ms2 + RMS_EPS) * gf_ref[...]
    h_ref[...] = hn.astype(BF16)
    logits = jnp.dot(hn, wr_ref[...], preferred_element_type=F32, precision=lax.Precision.HIGHEST)
    logits = logits + br_ref[...]
    lane = lax.broadcasted_iota(jnp.int32, logits.shape, 1)
    idx_acc = jnp.zeros(logits.shape, jnp.int32)
    val_acc = jnp.zeros(logits.shape, F32)
    top = None
    for kk in range(TOP_K):
        m = jnp.max(logits, axis=-1, keepdims=True)
        sel = jnp.min(jnp.where(logits == m, lane, LANES), axis=-1, keepdims=True)
        if kk == 0:
            top = m
        idx_acc = jnp.where(lane == kk, sel, idx_acc)
        val_acc = jnp.where(lane == kk, jnp.exp(m - top), val_acc)
        logits = jnp.where(lane == sel, -jnp.inf, logits)
    idx_ref[...] = idx_acc
    gate_ref[...] = val_acc / jnp.sum(val_acc, axis=-1, keepdims=True)


def _cross_router(x, mk, mv, pc, nb, tm):
    n = x.shape[0]
    per = n // nb // tm
    full = lambda a: pl.BlockSpec(a.shape, lambda i: (0,) * a.ndim)
    mem = pl.BlockSpec((1, N_MEM, MEM_WIDTH), lambda i: (i // per, 0, 0))
    row = lambda w: pl.BlockSpec((tm, w), lambda i: (i, 0))
    params1 = [pc["g_cross"], pc["w_xq"], pc["xq_gain"]]
    params2 = [pc["w_xo"], pc["g_ffn"], pc["w_router"], pc["b_router"]]
    return pl.pallas_call(
        _cross_router_kernel,
        grid=(n // tm,),
        in_specs=[row(D_MODEL)] + [full(a) for a in params1] + [mem, mem] + [full(a) for a in params2],
        out_specs=[row(D_MODEL), row(D_MODEL), row(LANES), row(LANES)],
        out_shape=[jax.ShapeDtypeStruct((n, D_MODEL), F32), jax.ShapeDtypeStruct((n, D_MODEL), BF16),
                   jax.ShapeDtypeStruct((n, LANES), jnp.int32), jax.ShapeDtypeStruct((n, LANES), F32)],
        compiler_params=_cparams(("parallel",)),
        name="cross_router",
    )(x, *params1, mk, mv, *params2)


MOE_TM = 768
MOE_TF = 256
MOE_TN = 512
MOE_NG = D_FF // (2 * MOE_TF)
MOE_ND = D_MODEL // MOE_TN


def _moe_kernel(te_ref, tv_ref, xs_ref, wgu_ref, bgu_ref, wd_ref, bd_ref, o_ref, act_ref):
    s = pl.program_id(1)
    used = tv_ref[pl.program_id(0)] > 0
    half = 2 * MOE_TF

    @pl.when(jnp.logical_and(used, s < MOE_NG))
    def _():
        gu = jnp.dot(xs_ref[...], wgu_ref[0].astype(BF16), preferred_element_type=F32) + bgu_ref[0]
        glu = jnp.minimum(gu, SWIGLU_LIMIT)
        fglu = glu * _sigmoid(SWIGLU_ALPHA * glu)
        lin = jnp.clip(gu, -SWIGLU_LIMIT, SWIGLU_LIMIT) + 1.0
        act_a = pltpu.roll(fglu[:, :half], 1, 1) * lin[:, :half]
        act_b = fglu[:, half:] * pltpu.roll(lin[:, half:], half - 1, 1)
        lane = lax.broadcasted_iota(jnp.int32, act_a.shape, 1)
        merged = jnp.where(lane % 2 == 0, act_b, act_a).astype(BF16)
        for c in range(MOE_NG):
            @pl.when(s == c)
            def _(c=c):
                act_ref[:, c * half:(c + 1) * half] = merged

    @pl.when(jnp.logical_and(used, s >= MOE_NG))
    def _():
        top = lambda w: lax.bitcast_convert_type(w.astype(BF16).astype(F32), jnp.uint32)
        chunks = []
        for c in range(MOE_NG):
            a = wd_ref[0, c * half:c * half + MOE_TF, :]
            b = wd_ref[0, c * half + MOE_TF:(c + 1) * half, :]
            chunks.append(pltpu.bitcast((top(b) >> 16) | top(a), BF16))
        wd = jnp.concatenate(chunks, axis=0)
        o_ref[...] = jnp.dot(act_ref[...], wd, preferred_element_type=F32) + bd_ref[0]

    @pl.when(jnp.logical_and(jnp.logical_not(used), s >= MOE_NG))
    def _():
        o_ref[...] = jnp.zeros_like(o_ref)


def _moe_experts(tile_expert, tile_valid, xs, w_gu, b_gu, w_down, b_down):
    p = xs.shape[0]
    nt = p // MOE_TM
    gi = lambda i, s, tv: jnp.where(tv[i] > 0, jnp.minimum(s, MOE_NG - 1), MOE_NG - 1)
    di = lambda i, s, tv: jnp.where(tv[i] > 0, jnp.maximum(s - MOE_NG, 0), MOE_ND - 1)
    grid_spec = pltpu.PrefetchScalarGridSpec(
        num_scalar_prefetch=2,
        grid=(nt, MOE_NG + MOE_ND),
        in_specs=[pl.BlockSpec((MOE_TM, D_MODEL), lambda i, s, te, tv: (i, 0)),
                  pl.BlockSpec((1, D_MODEL, 4 * MOE_TF), lambda i, s, te, tv: (te[i], 0, gi(i, s, tv))),
                  pl.BlockSpec((1, 1, 4 * MOE_TF), lambda i, s, te, tv: (te[i], 0, gi(i, s, tv))),
                  pl.BlockSpec((1, D_FF, MOE_TN), lambda i, s, te, tv: (te[i], 0, di(i, s, tv))),
                  pl.BlockSpec((1, 1, MOE_TN), lambda i, s, te, tv: (te[i], 0, di(i, s, tv)))],
        out_specs=pl.BlockSpec((MOE_TM, MOE_TN), lambda i, s, te, tv: (i, jnp.maximum(s - MOE_NG, 0))),
        scratch_shapes=[pltpu.VMEM((MOE_TM, D_FF), BF16)],
    )
    return pl.pallas_call(
        _moe_kernel,
        grid_spec=grid_spec,
        out_shape=jax.ShapeDtypeStruct((p, D_MODEL), F32),
        compiler_params=_cparams(("arbitrary", "arbitrary")),
        name="moe_experts",
    )(tile_expert, tile_valid, xs, w_gu, b_gu, w_down, b_down)


def _route(top_i):
    n = top_i.shape[0]
    na = n * TOP_K
    flat_e = top_i.reshape(na)
    order = jnp.argsort(flat_e, stable=True).astype(jnp.int32)
    inv = jnp.argsort(order).astype(jnp.int32)
    counts = jnp.sum((flat_e[:, None] == jnp.arange(N_EXPERTS, dtype=jnp.int32)[None, :]).astype(jnp.int32),
                     axis=0)
    padded = ((counts + MOE_TM - 1) // MOE_TM) * MOE_TM
    pad_start = jnp.cumsum(padded) - padded
    start = jnp.cumsum(counts) - counts
    pos = (inv + (pad_start - start)[flat_e]).reshape(n, TOP_K)
    nt = na // MOE_TM + N_EXPERTS
    p_rows = nt * MOE_TM
    tile_start = jnp.arange(nt, dtype=jnp.int32) * MOE_TM
    pad_end = pad_start + padded
    n_used = (jnp.sum(padded) // MOE_TM).astype(jnp.int32)
    tile_expert = jnp.minimum(jnp.sum((tile_start[:, None] >= pad_end[None, :]).astype(jnp.int32), axis=1),
                              N_EXPERTS - 1)
    used = jnp.arange(nt) < n_used
    tile_expert = jnp.where(used, tile_expert, tile_expert[jnp.maximum(n_used - 1, 0)])
    tile_valid = jnp.where(used, jnp.clip(counts[tile_expert] - (tile_start - pad_start[tile_expert]), 0, MOE_TM), 0)
    row = jnp.arange(p_rows, dtype=jnp.int32)
    row_e = jnp.repeat(tile_expert, MOE_TM)
    rank = row - pad_start[row_e]
    src = jnp.clip(start[row_e] + rank, 0, na - 1)
    row_token = jnp.where(rank < counts[row_e], order[src] // TOP_K, 0)
    return row_token, pos, tile_expert.astype(jnp.int32), tile_valid.astype(jnp.int32)


def _pad_cols(a, width):
    return jnp.pad(a, ((0, 0), (0, width - a.shape[1])))


def _pack_cols(a):
    o_w = 3 * RWKV_WIDTH
    o_a = o_w + DECAY_LORA
    o_g = o_a + AAA_LORA
    fb = A_COLS
    return jnp.concatenate([
        a[:, 0:3 * RWKV_WIDTH],
        a[:, fb:fb + 4 * FOX_WIDTH],
        _pad_cols(a[:, o_g:A_COLS], 256),
        a[:, o_w:o_g],
        _pad_cols(a[:, fb + 4 * FOX_WIDTH:], LANES),
    ], axis=1)


def _unpack_a_cols(z):
    return jnp.concatenate([z[..., 0:3 * RWKV_WIDTH], z[..., C_WA:C_WA + 128], z[..., C_GL:C_GL + GATE_LORA]],
                           axis=-1)


def _prev_rows(z, tm, first):
    nseq, t, _ = z.shape
    if t > tm:
        inner = z[:, tm - 1:t - 1:tm]
        rows = jnp.concatenate([first, inner], axis=1)
    else:
        rows = first
    rows = rows.reshape(-1, 1, Z_COLS)
    return rows[..., 0:3072], rows[..., C_GL:C_GL + 256], rows[..., C_WA:C_WA + 128]


def kernel(x_prompt, x_sample, mem_prompt, cache_fox_k, cache_fox_v, cache_fox_logf, state_rwkv, state_shift,
           cache_mem_k, cache_mem_v, norm_mix, w_in, rwkv_mu, rwkv_w0, rwkv_w2, rwkv_a0, rwkv_a2, rwkv_g2,
           rwkv_k_k, rwkv_k_a, rwkv_r_k, rwkv_lnx_w, rwkv_lnx_b, fox_b_f, fox_q_norm, fox_k_norm, w_out,
           norm_cross, norm_mem, w_xq, w_xk, w_xv, w_xo, xq_norm, xk_norm, norm_ffn, w_router, b_router,
           w_gu, b_gu, w_down, b_down):
    bp, sp, _ = x_prompt.shape
    bs, ts, _ = x_sample.shape
    past = cache_fox_k.shape[2]
    n_p, n_s = bp * sp, bs * ts
    l = 0

    w_in_p = _pack_cols(w_in[l]).astype(BF16)
    mu_p = _pack_cols(jnp.pad(rwkv_mu[l][None, :], ((0, 0), (0, w_in.shape[2] - A_COLS))))
    zero_lora = jnp.zeros((DECAY_LORA, RWKV_WIDTH), F32)
    prep = {
        "mu_m": mu_p[:, 0:3072], "mu_g": mu_p[:, C_GL:C_GL + 256], "mu_w": mu_p[:, C_WA:C_WA + 128],
        "w0": rwkv_w0[l][None], "a0": rwkv_a0[l][None],
        "w2": jnp.concatenate([rwkv_w2[l], zero_lora], axis=0).astype(BF16),
        "a2": jnp.concatenate([zero_lora, rwkv_a2[l]], axis=0).astype(BF16),
        "g2": jnp.pad(rwkv_g2[l], ((0, 256 - GATE_LORA), (0, 0))).astype(BF16),
        "k_k": rwkv_k_k[l][None], "k_a": rwkv_k_a[l][None], "r_k": rwkv_r_k[l].reshape(1, RWKV_WIDTH),
    }
    lnw, lnb = rwkv_lnx_w[l][None], rwkv_lnx_b[l][None]
    qg = jnp.tile(fox_q_norm[l], FOX_HEADS)[None]
    kg = jnp.tile(fox_k_norm[l], FOX_HEADS)[None]
    bf = _pad_cols(fox_b_f[l][None], LANES)
    w_out_b = w_out[l].astype(BF16)
    w_kv = jnp.concatenate([w_xk[l], w_xv[l]], axis=1).astype(BF16)
    pc = {
        "g_cross": norm_cross[l][None], "w_xq": w_xq[l].astype(BF16), "xq_gain": xq_norm[l][None],
        "w_xo": w_xo[l].astype(BF16), "g_ffn": norm_ffn[l][None],
        "w_router": _pad_cols(w_router[l], LANES),
        "b_router": jnp.concatenate([b_router[l], jnp.full((LANES - N_EXPERTS,), NEG_BIG, F32)])[None],
    }

    def mixers(x, nseq, t, first_shift, s0, tm_mm, tm_prep, tb, dim_major):
        z = _norm_mm(x, norm_mix[l][None], w_in_p, tm_mm, 768)
        z3 = z.reshape(nseq, t, Z_COLS)
        pm, pg, pw = _prev_rows(z3, tm_prep, first_shift)
        *steps, g, bonus = _rwkv_prep(z, pm, pg, pw, prep, tm_prep)
        steps = [a.reshape(nseq, t, HEAD_PAIRS, LANES) for a in steps]
        ya, s_fin = _rwkv_scan(steps, g.reshape(nseq, t, RWKV_WIDTH), bonus.reshape(nseq, t, RWKV_WIDTH),
                               _state_to_pairs(s0), lnw, lnb, tb)
        fox = _fox_proj(z, qg, kg, bf, tm_prep, nseq if dim_major else None)
        return z, z3, ya.reshape(nseq * t, RWKV_WIDTH), _pairs_to_state(s_fin), fox

    zero_shift = jnp.zeros((bp, 1, Z_COLS), F32)
    zero_state = jnp.zeros((bp, RWKV_HEADS, HEAD_DIM, HEAD_DIM), F32)
    xp = x_prompt.reshape(n_p, D_MODEL)
    xs = x_sample.reshape(n_s, D_MODEL)
    zp, zp3, ya_p, st_p, (qb_p, kb_p, lf_p, kt_p, vt_p, vtb_p, lft_p) = mixers(
        xp, bp, sp, zero_shift, zero_state, 1024, 256, SCAN_BLOCK, True)
    shift_s = _pack_cols(jnp.pad(state_shift[l].reshape(bs, A_COLS), ((0, 0), (0, w_in.shape[2] - A_COLS))))
    zs, zs3, ya_s, st_s, (qb_s, kb_s, lf_s, kf_s) = mixers(
        xs, bs, ts, shift_s.reshape(bs, 1, Z_COLS), state_rwkv[l], 256, ts, ts, False)
    lf_s = lf_s[:, :FOX_HEADS]

    fp = jnp.cumsum(lft_p, axis=2) * LOG2E
    q_aug, k_aug = _augment(qb_p, kb_p, fp, bp, sp)
    yb_p = _fox_prompt(q_aug, k_aug, vtb_p.reshape(bp, FOX_HEADS, HEAD_DIM, sp), zp, bp, sp, min(1024, sp))

    fs = jnp.cumsum(jnp.concatenate([cache_fox_logf[l].astype(F32), lf_s.reshape(bs, ts, FOX_HEADS)], axis=1),
                    axis=1) * LOG2E
    fn_s = fs[:, past:].transpose(0, 2, 1)
    kc_t = cache_fox_k[l].transpose(0, 2, 3, 1)
    vc_t = cache_fox_v[l].transpose(0, 2, 3, 1)
    yb_s = _fox_sample(qb_s, kc_t, vc_t, kb_s, zs, fn_s.reshape(bs, FOX_HEADS * ts, 1),
                       fs[:, :past].transpose(0, 2, 1), fn_s, bs, ts, min(512, past))

    x1_p = _out_proj(xp, ya_p, yb_p, w_out_b, 512)
    x1_s = _out_proj(xs, ya_s, yb_s, w_out_b, 256)

    kv_p = _mem_kv(mem_prompt.reshape(bp * N_MEM, D_MODEL), norm_mem[l][None], w_kv, xk_norm[l][None], 256)
    mk_p = kv_p[:, :MEM_WIDTH].reshape(bp, N_MEM, MEM_WIDTH)
    mv_p = kv_p[:, MEM_WIDTH:].reshape(bp, N_MEM, MEM_WIDTH)
    x2_p, h_p, ti_p, tg_p = _cross_router(x1_p, mk_p.astype(BF16), mv_p.astype(BF16), pc, bp, 256)
    mk_s = cache_mem_k[l].reshape(bs, N_MEM, MEM_WIDTH).astype(BF16)
    mv_s = cache_mem_v[l].reshape(bs, N_MEM, MEM_WIDTH).astype(BF16)
    x2_s, h_s, ti_s, tg_s = _cross_router(x1_s, mk_s, mv_s, pc, bs, ts)

    h_all = jnp.concatenate([h_p, h_s], axis=0)
    top_i = jnp.concatenate([ti_p[:, :TOP_K], ti_s[:, :TOP_K]], axis=0)
    gates = jnp.concatenate([tg_p[:, :TOP_K], tg_s[:, :TOP_K]], axis=0)
    row_token, pos, tile_expert, tile_valid = _route(top_i)
    xs_rows = h_all[row_token]
    out_rows = _moe_experts(tile_expert, tile_valid, xs_rows, w_gu[l], b_gu[l][:, None, :], w_down[l],
                            b_down[l][:, None, :])
    def combine(x2, rows, gate):
        return x2 + sum(out_rows[rows[:, k]] * gate[:, k:k + 1] for k in range(TOP_K))

    y_p = combine(x2_p, pos[:n_p], gates[:n_p]).reshape(bp, sp, D_MODEL)
    y_s = combine(x2_s, pos[n_p:], gates[n_p:]).reshape(bs, ts, D_MODEL)

    to_cache = lambda a: a.reshape(1, bp, FOX_HEADS, HEAD_DIM, sp).transpose(0, 1, 4, 2, 3)
    pk, pv, plf = to_cache(kt_p), to_cache(vt_p), lft_p.transpose(0, 2, 1)[None]
    sk = kf_s.reshape(1, bs, ts, FOX_HEADS, HEAD_DIM)
    sv = zs3[..., C_FV:C_FV + FOX_WIDTH].reshape(1, bs, ts, FOX_HEADS, HEAD_DIM)
    slf = lf_s.reshape(1, bs, ts, FOX_HEADS)
    return (y_p, y_s, pk, pv, plf, st_p[None], _unpack_a_cols(zp3[:, -1:])[None],
            mk_p.reshape(1, bp, N_MEM, MEM_HEADS, MEM_HEAD_DIM), mv_p.reshape(1, bp, N_MEM, MEM_HEADS, MEM_HEAD_DIM),
            sk, sv, slf, st_s[None], _unpack_a_cols(zs3[:, -1:])[None])
```

```python
import functools

import jax
import jax.numpy as jnp
from jax import lax
from jax.experimental import pallas as pl
from jax.experimental.pallas import tpu as pltpu

F32 = jnp.float32
BF16 = jnp.bfloat16

D_MODEL = 2048
HEAD_DIM = 64
RWKV_WIDTH = 1024
FOX_WIDTH = 1024
RWKV_HEADS = 16
FOX_HEADS = 16
DECAY_LORA = 64
AAA_LORA = 64
GATE_LORA = 160
A_COLS = 3 * RWKV_WIDTH + DECAY_LORA + AAA_LORA + GATE_LORA
N_MEM = 256
MEM_HEADS = 4
MEM_HEAD_DIM = 128
MEM_WIDTH = 512
N_EXPERTS = 32
TOP_K = 4
D_FF = 2048
SWIGLU_LIMIT = 7.0
SWIGLU_ALPHA = 1.702
RMS_EPS = 1e-6
GN_EPS = 64e-5

LANES = 128
HEAD_PAIRS = 8

C_R, C_K, C_V = 0, 1024, 2048
C_FQ, C_FK, C_FV, C_FG = 3072, 4096, 5120, 6144
C_GL = 7168
C_WA = 7424
C_FL = 7552
Z_COLS = 7680

VMEM_LIMIT = 56 * 1024 * 1024
NEG_BIG = -1e30
LOG2E = 1.4426950408889634


def _cparams(sem):
    return pltpu.CompilerParams(dimension_semantics=sem, vmem_limit_bytes=VMEM_LIMIT)


def _block_ones():
    r = lax.broadcasted_iota(jnp.int32, (LANES, LANES), 0) // HEAD_DIM
    c = lax.broadcasted_iota(jnp.int32, (LANES, LANES), 1) // HEAD_DIM
    return (r == c).astype(BF16)


def _seg_sum64(x):
    ones = _block_ones()
    outs = []
    for s in range(x.shape[-1] // LANES):
        xs = x[:, s * LANES:(s + 1) * LANES]
        hi = xs.astype(BF16)
        lo = (xs - hi.astype(F32)).astype(BF16)
        outs.append(jnp.dot(hi, ones, preferred_element_type=F32)
                    + jnp.dot(lo, ones, preferred_element_type=F32))
    return outs[0] if len(outs) == 1 else jnp.concatenate(outs, axis=-1)


def _softplus(x):
    return jnp.maximum(x, 0.0) + jnp.log(1.0 + jnp.exp(-jnp.abs(x)))


def _sigmoid(x):
    return 1.0 / (1.0 + jnp.exp(-x))


def _norm_mm_kernel(x_ref, g_ref, w_ref, o_ref, xn_ref):
    @pl.when(pl.program_id(1) == 0)
    def _():
        x = x_ref[...]
        ms = jnp.mean(x * x, axis=-1, keepdims=True)
        xn_ref[...] = (x * lax.rsqrt(ms + RMS_EPS) * g_ref[...]).astype(BF16)

    o_ref[...] = jnp.dot(xn_ref[...], w_ref[...], preferred_element_type=F32)


def _norm_mm(x, g, w, tm, tn):
    m, k = x.shape
    n = w.shape[1]
    return pl.pallas_call(
        _norm_mm_kernel,
        grid=(m // tm, n // tn),
        in_specs=[pl.BlockSpec((tm, k), lambda i, j: (i, 0)),
                  pl.BlockSpec((1, k), lambda i, j: (0, 0)),
                  pl.BlockSpec((k, tn), lambda i, j: (0, j))],
        out_specs=pl.BlockSpec((tm, tn), lambda i, j: (i, j)),
        out_shape=jax.ShapeDtypeStruct((m, n), F32),
        scratch_shapes=[pltpu.VMEM((tm, k), BF16)],
        compiler_params=_cparams(("parallel", "arbitrary")),
        name="norm_mm",
    )(x, g, w)


def _rwkv_prep_kernel(zm_ref, zg_ref, zw_ref, pm_ref, pg_ref, pw_ref,
                      mum_ref, mug_ref, muw_ref, w0_ref, w2_ref, a0_ref, a2_ref, g2_ref,
                      kk_ref, ka_ref, rk_ref,
                      r_o, d_o, k_o, v_o, kk_o, b_o, g_o, bonus_o):
    def shifted(z_ref, p_ref, mu_ref):
        z = z_ref[...]
        rolled = pltpu.roll(z, 1, 0)
        row = lax.broadcasted_iota(jnp.int32, z.shape, 0)
        prev = jnp.where(row == 0, p_ref[0], rolled)
        return z + mu_ref[...] * (prev - z)

    zm = shifted(zm_ref, pm_ref, mum_ref)
    zg = shifted(zg_ref, pg_ref, mug_ref)
    zw = shifted(zw_ref, pw_ref, muw_ref)
    r = zm[:, C_R:C_R + RWKV_WIDTH]
    k = zm[:, C_K:C_K + RWKV_WIDTH]
    v = zm[:, C_V:C_V + RWKV_WIDTH]
    lw = jnp.dot(jnp.tanh(zw).astype(BF16), w2_ref[...], preferred_element_type=F32)
    la = jnp.dot(zw.astype(BF16), a2_ref[...], preferred_element_type=F32)
    g = jnp.dot(_sigmoid(zg).astype(BF16), g2_ref[...], preferred_element_type=F32)
    w_log = -_softplus(-(w0_ref[...] + lw)) - 0.5
    a = _sigmoid(a0_ref[...] + la)
    kk = k * kk_ref[...]
    kk = kk * lax.rsqrt(jnp.maximum(_seg_sum64(kk * kk), 1e-24))
    kh = k * (1.0 + (a - 1.0) * ka_ref[...])
    tiles = lambda x: x.reshape(x.shape[0], HEAD_PAIRS, LANES)
    r_o[...] = tiles(r)
    d_o[...] = tiles(jnp.exp(-jnp.exp(w_log)))
    k_o[...] = tiles(kh)
    v_o[...] = tiles(v)
    kk_o[...] = tiles(kk)
    b_o[...] = tiles(kk * a)
    g_o[...] = g
    bonus_o[...] = _seg_sum64(r * kh * rk_ref[...]) * v


def _rwkv_prep(z, prev_m, prev_g, prev_w, pr, tm):
    n = z.shape[0]
    row = lambda w, c: pl.BlockSpec((tm, w), lambda i, c=c: (i, c))
    prev = lambda w: pl.BlockSpec((1, 1, w), lambda i: (i, 0, 0))
    full = lambda a: pl.BlockSpec(a.shape, lambda i: (0,) * a.ndim)
    params = [pr["mu_m"], pr["mu_g"], pr["mu_w"], pr["w0"], pr["w2"], pr["a0"], pr["a2"], pr["g2"],
              pr["k_k"], pr["k_a"], pr["r_k"]]
    tiled = jax.ShapeDtypeStruct((n, HEAD_PAIRS, LANES), F32)
    flat = jax.ShapeDtypeStruct((n, RWKV_WIDTH), F32)
    return pl.pallas_call(
        _rwkv_prep_kernel,
        grid=(n // tm,),
        in_specs=[row(3072, 0), row(256, C_GL // 256), row(128, C_WA // 128),
                  prev(3072), prev(256), prev(128)] + [full(a) for a in params],
        out_specs=[pl.BlockSpec((tm, HEAD_PAIRS, LANES), lambda i: (i, 0, 0))] * 6
        + [pl.BlockSpec((tm, RWKV_WIDTH), lambda i: (i, 0))] * 2,
        out_shape=[tiled] * 6 + [flat] * 2,
        compiler_params=_cparams(("parallel",)),
        name="rwkv_prep",
    )(z, z, z, prev_m, prev_g, prev_w, *params)


SEQ_PER_STEP = 2
SCAN_BLOCK = 128


def _rwkv_scan_kernel(r_ref, d_ref, k_ref, v_ref, kk_ref, b_ref, g_ref, bonus_ref, s0_ref,
                      lw_ref, lb_ref,
                      y_ref, sf_ref,
                      s_ref, oa_ref, *, tb):
    tblk = pl.program_id(1)

    @pl.when(tblk == 0)
    def _():
        s_ref[...] = s0_ref[...]

    oa_ref[...] = jnp.zeros_like(oa_ref)
    r2 = lax.broadcasted_iota(jnp.int32, (2 * LANES, 2 * LANES), 0) // HEAD_DIM
    c2 = lax.broadcasted_iota(jnp.int32, (2 * LANES, 2 * LANES), 1) // HEAD_DIM
    ones = (r2 == c2).astype(BF16)
    lane = lax.broadcasted_iota(jnp.int32, (HEAD_DIM, LANES), 1)
    sub = lax.broadcasted_iota(jnp.int32, (HEAD_DIM, LANES), 0)
    diag = (lane % HEAD_DIM) == sub
    tiles = [(s, hp) for s in range(SEQ_PER_STEP) for hp in range(HEAD_PAIRS)]

    def bcast(ref, s, t, hp):
        return jnp.broadcast_to(ref[s, t, hp:hp + 1, :], (HEAD_DIM, LANES))

    def seg_sums(parts):
        lhs = jnp.concatenate([jnp.concatenate(parts[j:j + 2], axis=1) for j in range(0, len(parts), 2)], axis=0)
        out = jnp.dot(lhs, ones, preferred_element_type=F32)
        return [out[(j // 2) * HEAD_DIM:(j // 2 + 1) * HEAD_DIM, (j % 2) * LANES:(j % 2 + 1) * LANES]
                for j in range(len(parts))]

    def emit_outputs(t_out, valid):
        half = pl.multiple_of((t_out // HEAD_DIM) * HEAD_DIM, HEAD_DIM)
        hit = jnp.logical_and((lane % HEAD_DIM) == (t_out % HEAD_DIM), valid)
        ob = seg_sums([(s_ref[s, hp] * bcast(r_ref, s, t_out, hp)).astype(BF16) for s, hp in tiles])
        for j, (s, hp) in enumerate(tiles):
            cur = oa_ref[s, hp, pl.ds(half, HEAD_DIM), :]
            oa_ref[s, hp, pl.ds(half, HEAD_DIM), :] = jnp.where(hit, ob[j], cur)

    def step(t, carry):
        emit_outputs(jnp.maximum(t - 1, 0), t > 0)
        skk = seg_sums([(s_ref[s, hp] * bcast(kk_ref, s, t, hp)).astype(BF16) for s, hp in tiles])
        vb = seg_sums([jnp.where(diag, bcast(v_ref, s, t, hp), 0.0).astype(BF16) for s, hp in tiles])
        for j, (s, hp) in enumerate(tiles):
            s_ref[s, hp] = (s_ref[s, hp] * bcast(d_ref, s, t, hp)
                            - skk[j] * bcast(b_ref, s, t, hp)
                            + vb[j] * bcast(k_ref, s, t, hp))
        return carry

    lax.fori_loop(0, tb, step, 0)
    emit_outputs(tb - 1, True)

    lane_t = lax.broadcasted_iota(jnp.int32, (HEAD_DIM, LANES), 1)
    low = lane_t < HEAD_DIM
    for s in range(SEQ_PER_STEP):
        slabs = []
        for hp in range(HEAD_PAIRS):
            mt = oa_ref[s, hp].T
            ro = pltpu.roll(mt, HEAD_DIM, 1)
            top = jnp.where(low, mt[:HEAD_DIM], ro[HEAD_DIM:])
            bot = jnp.where(low, ro[:HEAD_DIM], mt[HEAD_DIM:])
            slabs.append(jnp.concatenate([top, bot], axis=0)[:tb])
        o = jnp.concatenate(slabs, axis=-1)
        mean = _seg_sum64(o) * (1.0 / HEAD_DIM)
        cen = o - mean
        var = _seg_sum64(cen * cen) * (1.0 / HEAD_DIM)
        on = cen * lax.rsqrt(var + GN_EPS) * lw_ref[...] + lb_ref[...]
        y_ref[s] = ((on + bonus_ref[s]) * g_ref[s]).astype(y_ref.dtype)

    @pl.when(tblk == pl.num_programs(1) - 1)
    def _():
        sf_ref[...] = s_ref[...]


def _rwkv_scan(steps, g, bonus, s0, lnw, lnb, tb):
    nseq, t = g.shape[:2]
    step_spec = pl.BlockSpec((SEQ_PER_STEP, tb, HEAD_PAIRS, LANES), lambda i, j: (i, j, 0, 0))
    seq_spec = pl.BlockSpec((SEQ_PER_STEP, tb, RWKV_WIDTH), lambda i, j: (i, j, 0))
    st_spec = pl.BlockSpec((SEQ_PER_STEP, HEAD_PAIRS, HEAD_DIM, LANES), lambda i, j: (i, 0, 0, 0))
    par_spec = pl.BlockSpec((1, RWKV_WIDTH), lambda i, j: (0, 0))
    return pl.pallas_call(
        functools.partial(_rwkv_scan_kernel, tb=tb),
        grid=(nseq // SEQ_PER_STEP, t // tb),
        in_specs=[step_spec] * 6 + [seq_spec] * 2 + [st_spec] + [par_spec] * 2,
        out_specs=[seq_spec, st_spec],
        out_shape=[jax.ShapeDtypeStruct((nseq, t, RWKV_WIDTH), BF16),
                   jax.ShapeDtypeStruct(s0.shape, F32)],
        scratch_shapes=[pltpu.VMEM((SEQ_PER_STEP, HEAD_PAIRS, HEAD_DIM, LANES), F32),
                        pltpu.VMEM((SEQ_PER_STEP, HEAD_PAIRS, LANES, LANES), F32)],
        compiler_params=_cparams(("parallel", "arbitrary")),
        name="rwkv_scan",
    )(*steps, g, bonus, s0, lnw, lnb)


def _state_to_pairs(s):
    b = s.shape[0]
    return s.reshape(b, HEAD_PAIRS, 2, HEAD_DIM, HEAD_DIM).transpose(0, 1, 3, 2, 4).reshape(
        b, HEAD_PAIRS, HEAD_DIM, LANES)


def _pairs_to_state(s):
    b = s.shape[0]
    return s.reshape(b, HEAD_PAIRS, HEAD_DIM, 2, HEAD_DIM).transpose(0, 1, 3, 2, 4).reshape(
        b, RWKV_HEADS, HEAD_DIM, HEAD_DIM)


def _fox_proj_kernel(q_ref, k_ref, v_ref, fl_ref, qg_ref, kg_ref, bf_ref, qb_o, kb_o, lf_o, *cache_o, dim_major):
    def headnorm(x, g):
        ms = _seg_sum64(x * x) * (1.0 / HEAD_DIM)
        return x * lax.rsqrt(ms + RMS_EPS) * g

    q = headnorm(q_ref[...], qg_ref[...])
    k = headnorm(k_ref[...], kg_ref[...])
    q = q * (HEAD_DIM ** -0.5 * LOG2E)
    lf = -_softplus(-(fl_ref[...] + bf_ref[...]))
    lf_o[...] = lf
    if dim_major:
        lane = lax.broadcasted_iota(jnp.int32, (q.shape[0], LANES), 1)
        for h in range(FOX_HEADS):
            slab = slice((h // 2) * LANES, (h // 2 + 1) * LANES)
            for x, o in ((q, qb_o), (k, kb_o)):
                xs = x[:, slab] if h % 2 == 0 else pltpu.roll(x[:, slab], HEAD_DIM, 1)
                o[0, h] = jnp.where(lane < HEAD_DIM, xs, 0.0).astype(BF16)
        kt_o, vt_o, vtb_o, lft_o = cache_o
        vt = v_ref[...].T
        kt_o[0] = k.T
        vt_o[0] = vt
        vtb_o[0] = vt.astype(BF16)
        lft_o[0] = lf.T[:FOX_HEADS]
    else:
        qb_o[...] = q.astype(BF16)
        kb_o[...] = k.astype(BF16)
        cache_o[0][...] = k


def _fox_proj(z, qg, kg, bf, tm, nseq=None):
    n = z.shape[0]
    col = lambda w, c: pl.BlockSpec((tm, w), lambda i, c=c: (i, c))
    par = lambda w: pl.BlockSpec((1, w), lambda i: (0, 0))
    wide = pl.BlockSpec((tm, FOX_WIDTH), lambda i: (i, 0))
    lf_spec = pl.BlockSpec((tm, LANES), lambda i: (i, 0))
    lf_shape = jax.ShapeDtypeStruct((n, LANES), F32)
    if nseq is None:
        out_specs = [wide, wide, lf_spec, wide]
        out_shape = [jax.ShapeDtypeStruct((n, FOX_WIDTH), BF16), jax.ShapeDtypeStruct((n, FOX_WIDTH), BF16),
                     lf_shape, jax.ShapeDtypeStruct((n, FOX_WIDTH), F32)]
    else:
        t = n // nseq
        per = t // tm
        heads = pl.BlockSpec((1, FOX_HEADS, tm, LANES), lambda i: (i // per, 0, i % per, 0))
        tall = pl.BlockSpec((1, FOX_WIDTH, tm), lambda i: (i // per, 0, i % per))
        padded = jax.ShapeDtypeStruct((nseq, FOX_HEADS, t, LANES), BF16)
        out_specs = [heads, heads, lf_spec, tall, tall, tall,
                     pl.BlockSpec((1, FOX_HEADS, tm), lambda i: (i // per, 0, i % per))]
        out_shape = [padded, padded, lf_shape,
                     jax.ShapeDtypeStruct((nseq, FOX_WIDTH, t), F32), jax.ShapeDtypeStruct((nseq, FOX_WIDTH, t), F32),
                     jax.ShapeDtypeStruct((nseq, FOX_WIDTH, t), BF16),
                     jax.ShapeDtypeStruct((nseq, FOX_HEADS, t), F32)]
    return pl.pallas_call(
        functools.partial(_fox_proj_kernel, dim_major=nseq is not None),
        grid=(n // tm,),
        in_specs=[col(1024, C_FQ // 1024), col(1024, C_FK // 1024), col(1024, C_FV // 1024), col(128, C_FL // 128),
                  par(1024), par(1024), par(128)],
        out_specs=out_specs,
        out_shape=out_shape,
        compiler_params=_cparams(("parallel",)),
        name="fox_proj",
    )(z, z, z, z, qg, kg, bf)


F_TERMS = 8


def _fox_prompt_kernel(qi_ref, kj_ref, q_ref, k_ref, fq_ref, fk_ref, vt_ref, gate_ref, y_ref,
                       qa_ref, m_ref, l_ref, acc_ref, *, tq):
    pair = pl.program_id(2)
    qi = qi_ref[pair]
    kj = kj_ref[pair]
    lane = lax.broadcasted_iota(jnp.int32, (tq, LANES), 1)
    zero = jnp.zeros((tq, LANES), BF16)

    def with_terms(x, f, h):
        lo = HEAD_DIM + F_TERMS * h
        return x + jnp.where(jnp.logical_and(lane >= lo, lane < lo + F_TERMS), f, zero)

    @pl.when(kj == 0)
    def _():
        m_ref[...] = jnp.full_like(m_ref, NEG_BIG)
        l_ref[...] = jnp.zeros_like(l_ref)
        acc_ref[...] = jnp.zeros_like(acc_ref)
        for h in range(2):
            qa_ref[h] = with_terms(q_ref[0, h], fq_ref[0, 0], h)

    def body(masked):
        for h in range(2):
            st = lax.dot_general(with_terms(k_ref[0, h], fk_ref[0, 0], h), qa_ref[h], (((1,), (1,)), ((), ())),
                                 preferred_element_type=F32)
            if masked:
                key = lax.broadcasted_iota(jnp.int32, st.shape, 0)
                qry = lax.broadcasted_iota(jnp.int32, st.shape, 1)
                st = jnp.where(key <= qry, st, NEG_BIG)
            m_prev = m_ref[h]
            m_new = jnp.maximum(m_prev, jnp.max(st, axis=0, keepdims=True))
            alpha = jnp.exp2(m_prev - m_new)
            p = jnp.exp2(st - m_new)
            l_ref[h] = alpha * l_ref[h] + jnp.sum(p, axis=0, keepdims=True)
            acc_ref[h] = alpha * acc_ref[h] + jnp.dot(vt_ref[0, h], p.astype(BF16), preferred_element_type=F32)
            m_ref[h] = m_new

    @pl.when(kj < qi)
    def _():
        body(False)

    @pl.when(kj == qi)
    def _():
        body(True)
        ot = jnp.concatenate([acc_ref[0] / l_ref[0], acc_ref[1] / l_ref[1]], axis=0)
        y_ref[...] = (ot.T * _sigmoid(gate_ref[...])).astype(y_ref.dtype)


def _fox_prompt(q_pad, k_pad, fq, fk, vt, z, nb, seq, tq):
    nq = seq // tq
    pairs = [(i, j) for i in range(nq) for j in range(i + 1)]
    qi_tab = jnp.asarray([p[0] for p in pairs], jnp.int32)
    kj_tab = jnp.asarray([p[1] for p in pairs], jnp.int32)
    grid_spec = pltpu.PrefetchScalarGridSpec(
        num_scalar_prefetch=2,
        grid=(nb, HEAD_PAIRS, len(pairs)),
        in_specs=[pl.BlockSpec((1, 2, tq, LANES), lambda b, hp, p, qi, kj: (b, hp, qi[p], 0)),
                  pl.BlockSpec((1, 2, tq, LANES), lambda b, hp, p, qi, kj: (b, hp, kj[p], 0)),
                  pl.BlockSpec((1, 1, tq, LANES), lambda b, hp, p, qi, kj: (b, hp, qi[p], 0)),
                  pl.BlockSpec((1, 1, tq, LANES), lambda b, hp, p, qi, kj: (b, hp, kj[p], 0)),
                  pl.BlockSpec((1, 2, HEAD_DIM, tq), lambda b, hp, p, qi, kj: (b, hp, 0, kj[p])),
                  pl.BlockSpec((tq, LANES), lambda b, hp, p, qi, kj: (b * nq + qi[p], C_FG // LANES + hp))],
        out_specs=pl.BlockSpec((tq, LANES), lambda b, hp, p, qi, kj: (b * nq + qi[p], hp)),
        scratch_shapes=[pltpu.VMEM((2, tq, LANES), BF16), pltpu.VMEM((2, 1, tq), F32), pltpu.VMEM((2, 1, tq), F32),
                        pltpu.VMEM((2, HEAD_DIM, tq), F32)],
    )
    return pl.pallas_call(
        functools.partial(_fox_prompt_kernel, tq=tq),
        grid_spec=grid_spec,
        out_shape=jax.ShapeDtypeStruct((nb * seq, FOX_WIDTH), BF16),
        compiler_params=_cparams(("parallel", "parallel", "arbitrary")),
        name="fox_prompt",
    )(qi_tab, kj_tab, q_pad, k_pad, fq, fk, vt, z)


def _split3(x):
    def top(v):
        bits = lax.bitcast_convert_type(v, jnp.uint32) & jnp.uint32(0xFFFF0000)
        return lax.bitcast_convert_type(bits, F32)

    hi = top(x)
    mid = top(x - hi)
    lo = top(x - hi - mid)
    return hi.astype(BF16), mid.astype(BF16), lo.astype(BF16)


def _prefix_terms(f2, nb, seq):
    one = jnp.ones_like(f2, BF16)
    none = jnp.zeros_like(f2, BF16)

    def lanes(terms):
        t = jnp.stack([*terms, none, none], axis=-1)
        t = t.reshape(nb, HEAD_PAIRS, 2, seq, F_TERMS).transpose(0, 1, 3, 2, 4).reshape(nb, HEAD_PAIRS, seq, 16)
        return jnp.pad(t, ((0, 0), (0, 0), (0, 0), (HEAD_DIM, LANES - HEAD_DIM - 16)))

    return lanes([*_split3(f2), one, one, one]), lanes([one, one, one, *_split3(-f2)])


def _fox_sample_kernel(q_ref, kc_ref, vc_ref, kn_ref, vn_ref, fq_ref, fkc_ref, fkn_ref, gate_ref, y_ref,
                       m_ref, l_ref, acc_ref, *, nk):
    kj = pl.program_id(1)

    @pl.when(kj == 0)
    def _():
        m_ref[...] = jnp.full_like(m_ref, NEG_BIG)
        l_ref[...] = jnp.zeros_like(l_ref)
        acc_ref[...] = jnp.zeros_like(acc_ref)

    def head_slice(ref, h):
        return ref[:, h * HEAD_DIM:(h + 1) * HEAD_DIM]

    t = q_ref.shape[0]

    nt_dims = (((1,), (1,)), ((), ()))

    def attend(ks, vs, fk_ref, masked, dim_major):
        if dim_major:
            qk = lambda h: jnp.dot(head_slice(q_ref, h), ks[h], preferred_element_type=F32)
            pv_of = lambda ph, h: lax.dot_general(ph, vs[h], nt_dims, preferred_element_type=F32)
        else:
            qk = lambda h: lax.dot_general(head_slice(q_ref, h), ks[h], nt_dims, preferred_element_type=F32)
            pv_of = lambda ph, h: jnp.dot(ph, vs[h], preferred_element_type=F32)
        s = jnp.concatenate([qk(h) for h in range(FOX_HEADS)], axis=0)
        fk = jnp.concatenate([jnp.broadcast_to(fk_ref[0, h:h + 1, :], (t, s.shape[1])) for h in range(FOX_HEADS)],
                             axis=0)
        s = s + fq_ref[0] - fk
        if masked:
            qry = lax.broadcasted_iota(jnp.int32, s.shape, 0) % t
            key = lax.broadcasted_iota(jnp.int32, s.shape, 1)
            s = jnp.where(key <= qry, s, NEG_BIG)
        m_prev = m_ref[...]
        m_new = jnp.maximum(m_prev, jnp.max(s, axis=-1, keepdims=True))
        alpha = jnp.exp2(m_prev - m_new)
        p = jnp.exp2(s - m_new)
        l_ref[...] = alpha * l_ref[...] + jnp.sum(p, axis=-1, keepdims=True)
        p = p.astype(BF16)
        pv = jnp.concatenate([pv_of(p[h * t:(h + 1) * t], h) for h in range(FOX_HEADS)], axis=0)
        acc_ref[...] = alpha * acc_ref[...] + pv
        m_ref[...] = m_new

    @pl.when(kj < nk)
    def _():
        attend([kc_ref[0, h].astype(BF16) for h in range(FOX_HEADS)],
               [vc_ref[0, h].astype(BF16) for h in range(FOX_HEADS)], fkc_ref, False, True)

    @pl.when(kj == nk)
    def _():
        attend([head_slice(kn_ref, h) for h in range(FOX_HEADS)],
               [head_slice(vn_ref, h).astype(BF16) for h in range(FOX_HEADS)], fkn_ref, True, False)
        o = acc_ref[...] / l_ref[...]
        for h in range(FOX_HEADS):
            gate = _sigmoid(head_slice(gate_ref, h))
            y_ref[:, h * HEAD_DIM:(h + 1) * HEAD_DIM] = (o[h * t:(h + 1) * t] * gate).astype(y_ref.dtype)


def _fox_sample(qb, kc, vc, kb, z, fq, fkc, fkn, nb, t, tk):
    past = kc.shape[3]
    nk = past // tk
    last = nk - 1
    wide = lambda c: pl.BlockSpec((t, FOX_WIDTH), lambda b, j, c=c: (b, c))
    cache = pl.BlockSpec((1, FOX_HEADS, HEAD_DIM, tk), lambda b, j: (b, 0, 0, jnp.minimum(j, last)))
    return pl.pallas_call(
        functools.partial(_fox_sample_kernel, nk=nk),
        grid=(nb, nk + 1),
        in_specs=[wide(0), cache, cache, wide(0), wide(C_FV // FOX_WIDTH),
                  pl.BlockSpec((1, FOX_HEADS * t, 1), lambda b, j: (b, 0, 0)),
                  pl.BlockSpec((1, FOX_HEADS, tk), lambda b, j: (b, 0, jnp.minimum(j, last))),
                  pl.BlockSpec((1, FOX_HEADS, t), lambda b, j: (b, 0, 0)),
                  wide(C_FG // FOX_WIDTH)],
        out_specs=wide(0),
        out_shape=jax.ShapeDtypeStruct((nb * t, FOX_WIDTH), BF16),
        scratch_shapes=[pltpu.VMEM((FOX_HEADS * t, 1), F32), pltpu.VMEM((FOX_HEADS * t, 1), F32),
                        pltpu.VMEM((FOX_HEADS * t, HEAD_DIM), F32)],
        compiler_params=_cparams(("parallel", "arbitrary")),
        name="fox_sample",
    )(qb, kc, vc, kb, z, fq, fkc, fkn, z)


def _out_proj_kernel(x_ref, ya_ref, yb_ref, wa_ref, wb_ref, o_ref):
    o_ref[...] = (x_ref[...]
                  + jnp.dot(ya_ref[...], wa_ref[...], preferred_element_type=F32)
                  + jnp.dot(yb_ref[...], wb_ref[...], preferred_element_type=F32))


def _out_proj(x, ya, yb, w, tm):
    n = x.shape[0]
    return pl.pallas_call(
        _out_proj_kernel,
        grid=(n // tm,),
        in_specs=[pl.BlockSpec((tm, D_MODEL), lambda i: (i, 0)),
                  pl.BlockSpec((tm, RWKV_WIDTH), lambda i: (i, 0)),
                  pl.BlockSpec((tm, FOX_WIDTH), lambda i: (i, 0)),
                  pl.BlockSpec((RWKV_WIDTH, D_MODEL), lambda i: (0, 0)),
                  pl.BlockSpec((FOX_WIDTH, D_MODEL), lambda i: (1, 0))],
        out_specs=pl.BlockSpec((tm, D_MODEL), lambda i: (i, 0)),
        out_shape=jax.ShapeDtypeStruct((n, D_MODEL), F32),
        compiler_params=_cparams(("parallel",)),
        name="out_proj",
    )(x, ya, yb, w, w)


def _mem_kv_kernel(x_ref, g_ref, w_ref, kg_ref, o_ref):
    x = x_ref[...]
    ms = jnp.mean(x * x, axis=-1, keepdims=True)
    xn = (x * lax.rsqrt(ms + RMS_EPS) * g_ref[...]).astype(BF16)
    kv = jnp.dot(xn, w_ref[...], preferred_element_type=F32)
    for h in range(MEM_HEADS):
        kh = kv[:, h * MEM_HEAD_DIM:(h + 1) * MEM_HEAD_DIM]
        ms = jnp.mean(kh * kh, axis=-1, keepdims=True)
        o_ref[:, h * MEM_HEAD_DIM:(h + 1) * MEM_HEAD_DIM] = kh * lax.rsqrt(ms + RMS_EPS) * kg_ref[...]
    o_ref[:, MEM_WIDTH:] = kv[:, MEM_WIDTH:]


def _mem_kv(mem, g, w_kv, kg, tm):
    n = mem.shape[0]
    return pl.pallas_call(
        _mem_kv_kernel,
        grid=(n // tm,),
        in_specs=[pl.BlockSpec((tm, D_MODEL), lambda i: (i, 0)),
                  pl.BlockSpec((1, D_MODEL), lambda i: (0, 0)),
                  pl.BlockSpec((D_MODEL, 2 * MEM_WIDTH), lambda i: (0, 0)),
                  pl.BlockSpec((1, MEM_HEAD_DIM), lambda i: (0, 0))],
        out_specs=pl.BlockSpec((tm, 2 * MEM_WIDTH), lambda i: (i, 0)),
        out_shape=jax.ShapeDtypeStruct((n, 2 * MEM_WIDTH), F32),
        compiler_params=_cparams(("parallel",)),
        name="mem_kv",
    )(mem, g, w_kv, kg)


def _cross_router_kernel(x_ref, gc_ref, wq_ref, qg_ref, mk_ref, mv_ref, wo_ref,
                         gf_ref, wr_ref, br_ref,
                         x2_ref, h_ref, idx_ref, gate_ref):
    x = x_ref[...]
    ms = jnp.mean(x * x, axis=-1, keepdims=True)
    xn = (x * lax.rsqrt(ms + RMS_EPS) * gc_ref[...]).astype(BF16)
    q = jnp.dot(xn, wq_ref[...], preferred_element_type=F32)
    outs = []
    for h in range(MEM_HEADS):
        sl = slice(h * MEM_HEAD_DIM, (h + 1) * MEM_HEAD_DIM)
        qh = q[:, sl]
        qms = jnp.mean(qh * qh, axis=-1, keepdims=True)
        qh = (qh * lax.rsqrt(qms + RMS_EPS) * qg_ref[...]).astype(BF16)
        s = lax.dot_general(qh, mk_ref[0, :, sl], (((1,), (1,)), ((), ())), preferred_element_type=F32)
        s = s * (MEM_HEAD_DIM ** -0.5)
        p = jnp.exp(s - jnp.max(s, axis=-1, keepdims=True))
        p = p / jnp.sum(p, axis=-1, keepdims=True)
        outs.append(jnp.dot(p.astype(BF16), mv_ref[0, :, sl], preferred_element_type=F32))
    o = jnp.concatenate(outs, axis=-1).astype(BF16)
    x2 = x + jnp.dot(o, wo_ref[...], preferred_element_type=F32)
    x2_ref[...] = x2

    ms2 = jnp.mean(x2 * x2, axis=-1, keepdims=True)
    hn = x2 * lax.rsqrt(ms2 + RMS_EPS) * gf_ref[...]
    h_ref[...] = hn.astype(BF16)
    logits = jnp.dot(hn, wr_ref[...], preferred_element_type=F32, precision=lax.Precision.HIGHEST)
    logits = logits + br_ref[...]
    lane = lax.broadcasted_iota(jnp.int32, logits.shape, 1)
    idx_acc = jnp.zeros(logits.shape, jnp.int32)
    val_acc = jnp.zeros(logits.shape, F32)
    top = None
    for kk in range(TOP_K):
        m = jnp.max(logits, axis=-1, keepdims=True)
        sel = jnp.min(jnp.where(logits == m, lane, LANES), axis=-1, keepdims=True)
        if kk == 0:
            top = m
        idx_acc = jnp.where(lane == kk, sel, idx_acc)
        val_acc = jnp.where(lane == kk, jnp.exp(m - top), val_acc)
        logits = jnp.where(lane == sel, -jnp.inf, logits)
    idx_ref[...] = idx_acc
    gate_ref[...] = val_acc / jnp.sum(val_acc, axis=-1, keepdims=True)


def _cross_router(x, mk, mv, pc, nb, tm):
    n = x.shape[0]
    per = n // nb // tm
    full = lambda a: pl.BlockSpec(a.shape, lambda i: (0,) * a.ndim)
    mem = pl.BlockSpec((1, N_MEM, MEM_WIDTH), lambda i: (i // per, 0, 0))
    row = lambda w: pl.BlockSpec((tm, w), lambda i: (i, 0))
    params1 = [pc["g_cross"], pc["w_xq"], pc["xq_gain"]]
    params2 = [pc["w_xo"], pc["g_ffn"], pc["w_router"], pc["b_router"]]
    return pl.pallas_call(
        _cross_router_kernel,
        grid=(n // tm,),
        in_specs=[row(D_MODEL)] + [full(a) for a in params1] + [mem, mem] + [full(a) for a in params2],
        out_specs=[row(D_MODEL), row(D_MODEL), row(LANES), row(LANES)],
        out_shape=[jax.ShapeDtypeStruct((n, D_MODEL), F32), jax.ShapeDtypeStruct((n, D_MODEL), BF16),
                   jax.ShapeDtypeStruct((n, LANES), jnp.int32), jax.ShapeDtypeStruct((n, LANES), F32)],
        compiler_params=_cparams(("parallel",)),
        name="cross_router",
    )(x, *params1, mk, mv, *params2)


MOE_TM = 768
MOE_TF = 256
MOE_TN = 512
MOE_NG = D_FF // (2 * MOE_TF)
MOE_ND = D_MODEL // MOE_TN


def _moe_kernel(te_ref, tv_ref, xs_ref, wgu_ref, bgu_ref, wd_ref, bd_ref, o_ref, act_ref):
    s = pl.program_id(1)
    used = tv_ref[pl.program_id(0)] > 0
    half = 2 * MOE_TF

    @pl.when(jnp.logical_and(used, s < MOE_NG))
    def _():
        gu = jnp.dot(xs_ref[...], wgu_ref[0].astype(BF16), preferred_element_type=F32) + bgu_ref[0]
        glu = jnp.minimum(gu, SWIGLU_LIMIT)
        fglu = glu * _sigmoid(SWIGLU_ALPHA * glu)
        lin = jnp.clip(gu, -SWIGLU_LIMIT, SWIGLU_LIMIT) + 1.0
        act_a = pltpu.roll(fglu[:, :half], 1, 1) * lin[:, :half]
        act_b = fglu[:, half:] * pltpu.roll(lin[:, half:], half - 1, 1)
        lane = lax.broadcasted_iota(jnp.int32, act_a.shape, 1)
        merged = jnp.where(lane % 2 == 0, act_b, act_a).astype(BF16)
        for c in range(MOE_NG):
            @pl.when(s == c)
            def _(c=c):
                act_ref[:, c * half:(c + 1) * half] = merged

    @pl.when(jnp.logical_and(used, s >= MOE_NG))
    def _():
        top = lambda w: lax.bitcast_convert_type(w.astype(BF16).astype(F32), jnp.uint32)
        chunks = []
        for c in range(MOE_NG):
            a = wd_ref[0, c * half:c * half + MOE_TF, :]
            b = wd_ref[0, c * half + MOE_TF:(c + 1) * half, :]
            chunks.append(pltpu.bitcast((top(b) >> 16) | top(a), BF16))
        wd = jnp.concatenate(chunks, axis=0)
        o_ref[...] = jnp.dot(act_ref[...], wd, preferred_element_type=F32) + bd_ref[0]

    @pl.when(jnp.logical_and(jnp.logical_not(used), s >= MOE_NG))
    def _():
        o_ref[...] = jnp.zeros_like(o_ref)


def _moe_experts(tile_expert, tile_valid, xs, w_gu, b_gu, w_down, b_down):
    p = xs.shape[0]
    nt = p // MOE_TM
    gi = lambda i, s, tv: jnp.where(tv[i] > 0, jnp.minimum(s, MOE_NG - 1), MOE_NG - 1)
    di = lambda i, s, tv: jnp.where(tv[i] > 0, jnp.maximum(s - MOE_NG, 0), MOE_ND - 1)
    grid_spec = pltpu.PrefetchScalarGridSpec(
        num_scalar_prefetch=2,
        grid=(nt, MOE_NG + MOE_ND),
        in_specs=[pl.BlockSpec((MOE_TM, D_MODEL), lambda i, s, te, tv: (i, 0)),
                  pl.BlockSpec((1, D_MODEL, 4 * MOE_TF), lambda i, s, te, tv: (te[i], 0, gi(i, s, tv))),
                  pl.BlockSpec((1, 1, 4 * MOE_TF), lambda i, s, te, tv: (te[i], 0, gi(i, s, tv))),
                  pl.BlockSpec((1, D_FF, MOE_TN), lambda i, s, te, tv: (te[i], 0, di(i, s, tv))),
                  pl.BlockSpec((1, 1, MOE_TN), lambda i, s, te, tv: (te[i], 0, di(i, s, tv)))],
        out_specs=pl.BlockSpec((MOE_TM, MOE_TN), lambda i, s, te, tv: (i, jnp.maximum(s - MOE_NG, 0))),
        scratch_shapes=[pltpu.VMEM((MOE_TM, D_FF), BF16)],
    )
    return pl.pallas_call(
        _moe_kernel,
        grid_spec=grid_spec,
        out_shape=jax.ShapeDtypeStruct((p, D_MODEL), F32),
        compiler_params=_cparams(("arbitrary", "arbitrary")),
        name="moe_experts",
    )(tile_expert, tile_valid, xs, w_gu, b_gu, w_down, b_down)


def _route(top_i):
    n = top_i.shape[0]
    na = n * TOP_K
    flat_e = top_i.reshape(na)
    order = jnp.argsort(flat_e, stable=True).astype(jnp.int32)
    inv = jnp.argsort(order).astype(jnp.int32)
    counts = jnp.sum((flat_e[:, None] == jnp.arange(N_EXPERTS, dtype=jnp.int32)[None, :]).astype(jnp.int32),
                     axis=0)
    padded = ((counts + MOE_TM - 1) // MOE_TM) * MOE_TM
    pad_start = jnp.cumsum(padded) - padded
    start = jnp.cumsum(counts) - counts
    pos = (inv + (pad_start - start)[flat_e]).reshape(n, TOP_K)
    nt = na // MOE_TM + N_EXPERTS
    p_rows = nt * MOE_TM
    tile_start = jnp.arange(nt, dtype=jnp.int32) * MOE_TM
    pad_end = pad_start + padded
    n_used = (jnp.sum(padded) // MOE_TM).astype(jnp.int32)
    tile_expert = jnp.minimum(jnp.sum((tile_start[:, None] >= pad_end[None, :]).astype(jnp.int32), axis=1),
                              N_EXPERTS - 1)
    used = jnp.arange(nt) < n_used
    tile_expert = jnp.where(used, tile_expert, tile_expert[jnp.maximum(n_used - 1, 0)])
    tile_valid = jnp.where(used, jnp.clip(counts[tile_expert] - (tile_start - pad_start[tile_expert]), 0, MOE_TM), 0)
    row = jnp.arange(p_rows, dtype=jnp.int32)
    row_e = jnp.repeat(tile_expert, MOE_TM)
    rank = row - pad_start[row_e]
    src = jnp.clip(start[row_e] + rank, 0, na - 1)
    row_token = jnp.where(rank < counts[row_e], order[src] // TOP_K, 0)
    return row_token, pos, tile_expert.astype(jnp.int32), tile_valid.astype(jnp.int32)


def _pad_cols(a, width):
    return jnp.pad(a, ((0, 0), (0, width - a.shape[1])))


def _pack_cols(a):
    o_w = 3 * RWKV_WIDTH
    o_a = o_w + DECAY_LORA
    o_g = o_a + AAA_LORA
    fb = A_COLS
    return jnp.concatenate([
        a[:, 0:3 * RWKV_WIDTH],
        a[:, fb:fb + 4 * FOX_WIDTH],
        _pad_cols(a[:, o_g:A_COLS], 256),
        a[:, o_w:o_g],
        _pad_cols(a[:, fb + 4 * FOX_WIDTH:], LANES),
    ], axis=1)


def _unpack_a_cols(z):
    return jnp.concatenate([z[..., 0:3 * RWKV_WIDTH], z[..., C_WA:C_WA + 128], z[..., C_GL:C_GL + GATE_LORA]],
                           axis=-1)


def _prev_rows(z, tm, first):
    nseq, t, _ = z.shape
    if t > tm:
        inner = z[:, tm - 1:t - 1:tm]
        rows = jnp.concatenate([first, inner], axis=1)
    else:
        rows = first
    rows = rows.reshape(-1, 1, Z_COLS)
    return rows[..., 0:3072], rows[..., C_GL:C_GL + 256], rows[..., C_WA:C_WA + 128]


def kernel(x_prompt, x_sample, mem_prompt, cache_fox_k, cache_fox_v, cache_fox_logf, state_rwkv, state_shift,
           cache_mem_k, cache_mem_v, norm_mix, w_in, rwkv_mu, rwkv_w0, rwkv_w2, rwkv_a0, rwkv_a2, rwkv_g2,
           rwkv_k_k, rwkv_k_a, rwkv_r_k, rwkv_lnx_w, rwkv_lnx_b, fox_b_f, fox_q_norm, fox_k_norm, w_out,
           norm_cross, norm_mem, w_xq, w_xk, w_xv, w_xo, xq_norm, xk_norm, norm_ffn, w_router, b_router,
           w_gu, b_gu, w_down, b_down):
    bp, sp, _ = x_prompt.shape
    bs, ts, _ = x_sample.shape
    past = cache_fox_k.shape[2]
    n_p, n_s = bp * sp, bs * ts
    l = 0

    w_in_p = _pack_cols(w_in[l]).astype(BF16)
    mu_p = _pack_cols(jnp.pad(rwkv_mu[l][None, :], ((0, 0), (0, w_in.shape[2] - A_COLS))))
    zero_lora = jnp.zeros((DECAY_LORA, RWKV_WIDTH), F32)
    prep = {
        "mu_m": mu_p[:, 0:3072], "mu_g": mu_p[:, C_GL:C_GL + 256], "mu_w": mu_p[:, C_WA:C_WA + 128],
        "w0": rwkv_w0[l][None], "a0": rwkv_a0[l][None],
        "w2": jnp.concatenate([rwkv_w2[l], zero_lora], axis=0).astype(BF16),
        "a2": jnp.concatenate([zero_lora, rwkv_a2[l]], axis=0).astype(BF16),
        "g2": jnp.pad(rwkv_g2[l], ((0, 256 - GATE_LORA), (0, 0))).astype(BF16),
        "k_k": rwkv_k_k[l][None], "k_a": rwkv_k_a[l][None], "r_k": rwkv_r_k[l].reshape(1, RWKV_WIDTH),
    }
    lnw, lnb = rwkv_lnx_w[l][None], rwkv_lnx_b[l][None]
    qg = jnp.tile(fox_q_norm[l], FOX_HEADS)[None]
    kg = jnp.tile(fox_k_norm[l], FOX_HEADS)[None]
    bf = _pad_cols(fox_b_f[l][None], LANES)
    w_out_b = w_out[l].astype(BF16)
    w_kv = jnp.concatenate([w_xk[l], w_xv[l]], axis=1).astype(BF16)
    pc = {
        "g_cross": norm_cross[l][None], "w_xq": w_xq[l].astype(BF16), "xq_gain": xq_norm[l][None],
        "w_xo": w_xo[l].astype(BF16), "g_ffn": norm_ffn[l][None],
        "w_router": _pad_cols(w_router[l], LANES),
        "b_router": jnp.concatenate([b_router[l], jnp.full((LANES - N_EXPERTS,), NEG_BIG, F32)])[None],
    }

    def mixers(x, nseq, t, first_shift, s0, tm_mm, tm_prep, tb, dim_major):
        z = _norm_mm(x, norm_mix[l][None], w_in_p, tm_mm, 768)
        z3 = z.reshape(nseq, t, Z_COLS)
        pm, pg, pw = _prev_rows(z3, tm_prep, first_shift)
        *steps, g, bonus = _rwkv_prep(z, pm, pg, pw, prep, tm_prep)
        steps = [a.reshape(nseq, t, HEAD_PAIRS, LANES) for a in steps]
        ya, s_fin = _rwkv_scan(steps, g.reshape(nseq, t, RWKV_WIDTH), bonus.reshape(nseq, t, RWKV_WIDTH),
                               _state_to_pairs(s0), lnw, lnb, tb)
        fox = _fox_proj(z, qg, kg, bf, tm_prep, nseq if dim_major else None)
        return z, z3, ya.reshape(nseq * t, RWKV_WIDTH), _pairs_to_state(s_fin), fox

    zero_shift = jnp.zeros((bp, 1, Z_COLS), F32)
    zero_state = jnp.zeros((bp, RWKV_HEADS, HEAD_DIM, HEAD_DIM), F32)
    xp = x_prompt.reshape(n_p, D_MODEL)
    xs = x_sample.reshape(n_s, D_MODEL)
    zp, zp3, ya_p, st_p, (qb_p, kb_p, lf_p, kt_p, vt_p, vtb_p, lft_p) = mixers(
        xp, bp, sp, zero_shift, zero_state, 1024, 256, SCAN_BLOCK, True)
    shift_s = _pack_cols(jnp.pad(state_shift[l].reshape(bs, A_COLS), ((0, 0), (0, w_in.shape[2] - A_COLS))))
    zs, zs3, ya_s, st_s, (qb_s, kb_s, lf_s, kf_s) = mixers(
        xs, bs, ts, shift_s.reshape(bs, 1, Z_COLS), state_rwkv[l], 256, ts, ts, False)
    lf_s = lf_s[:, :FOX_HEADS]

    fp = jnp.cumsum(lft_p, axis=2) * LOG2E
    fq_p, fk_p = _prefix_terms(fp, bp, sp)
    yb_p = _fox_prompt(qb_p, kb_p, fq_p, fk_p, vtb_p.reshape(bp, FOX_HEADS, HEAD_DIM, sp), zp, bp, sp,
                       min(1024, sp))

    fs = jnp.cumsum(jnp.concatenate([cache_fox_logf[l].astype(F32), lf_s.reshape(bs, ts, FOX_HEADS)], axis=1),
                    axis=1) * LOG2E
    fn_s = fs[:, past:].transpose(0, 2, 1)
    kc_t = cache_fox_k[l].transpose(0, 2, 3, 1)
    vc_t = cache_fox_v[l].transpose(0, 2, 3, 1)
    yb_s = _fox_sample(qb_s, kc_t, vc_t, kb_s, zs, fn_s.reshape(bs, FOX_HEADS * ts, 1),
                       fs[:, :past].transpose(0, 2, 1), fn_s, bs, ts, min(512, past))

    x1_p = _out_proj(xp, ya_p, yb_p, w_out_b, 512)
    x1_s = _out_proj(xs, ya_s, yb_s, w_out_b, 256)

    kv_p = _mem_kv(mem_prompt.reshape(bp * N_MEM, D_MODEL), norm_mem[l][None], w_kv, xk_norm[l][None], 256)
    mk_p = kv_p[:, :MEM_WIDTH].reshape(bp, N_MEM, MEM_WIDTH)
    mv_p = kv_p[:, MEM_WIDTH:].reshape(bp, N_MEM, MEM_WIDTH)
    x2_p, h_p, ti_p, tg_p = _cross_router(x1_p, mk_p.astype(BF16), mv_p.astype(BF16), pc, bp, 256)
    mk_s = cache_mem_k[l].reshape(bs, N_MEM, MEM_WIDTH).astype(BF16)
    mv_s = cache_mem_v[l].reshape(bs, N_MEM, MEM_WIDTH).astype(BF16)
    x2_s, h_s, ti_s, tg_s = _cross_router(x1_s, mk_s, mv_s, pc, bs, ts)

    h_all = jnp.concatenate([h_p, h_s], axis=0)
    top_i = jnp.concatenate([ti_p[:, :TOP_K], ti_s[:, :TOP_K]], axis=0)
    gates = jnp.concatenate([tg_p[:, :TOP_K], tg_s[:, :TOP_K]], axis=0)
    row_token, pos, tile_expert, tile_valid = _route(top_i)
    xs_rows = h_all[row_token]
    out_rows = _moe_experts(tile_expert, tile_valid, xs_rows, w_gu[l], b_gu[l][:, None, :], w_down[l],
                            b_down[l][:, None, :])
    def combine(x2, rows, gate):
        return x2 + sum(out_rows[rows[:, k]] * gate[:, k:k + 1] for k in range(TOP_K))

    y_p = combine(x2_p, pos[:n_p], gates[:n_p]).reshape(bp, sp, D_MODEL)
    y_s = combine(x2_s, pos[n_p:], gates[n_p:]).reshape(bs, ts, D_MODEL)

    to_cache = lambda a: a.reshape(1, bp, FOX_HEADS, HEAD_DIM, sp).transpose(0, 1, 4, 2, 3)
    pk, pv, plf = to_cache(kt_p), to_cache(vt_p), lft_p.transpose(0, 2, 1)[None]
    sk = kf_s.reshape(1, bs, ts, FOX_HEADS, HEAD_DIM)
    sv = zs3[..., C_FV:C_FV + FOX_WIDTH].reshape(1, bs, ts, FOX_HEADS, HEAD_DIM)
    slf = lf_s.reshape(1, bs, ts, FOX_HEADS)
    return (y_p, y_s, pk, pv, plf, st_p[None], _unpack_a_cols(zp3[:, -1:])[None],
            mk_p.reshape(1, bp, N_MEM, MEM_HEADS, MEM_HEAD_DIM), mv_p.reshape(1, bp, N_MEM, MEM_HEADS, MEM_HEAD_DIM),
            sk, sv, slf, st_s[None], _unpack_a_cols(zs3[:, -1:])[None])
```

```python
import functools

import jax
import jax.numpy as jnp
from jax import lax
from jax.experimental import pallas as pl
from jax.experimental.pallas import tpu as pltpu

F32 = jnp.float32
BF16 = jnp.bfloat16

D_MODEL = 2048
HEAD_DIM = 64
RWKV_WIDTH = 1024
FOX_WIDTH = 1024
RWKV_HEADS = 16
FOX_HEADS = 16
DECAY_LORA = 64
AAA_LORA = 64
GATE_LORA = 160
A_COLS = 3 * RWKV_WIDTH + DECAY_LORA + AAA_LORA + GATE_LORA
N_MEM = 256
MEM_HEADS = 4
MEM_HEAD_DIM = 128
MEM_WIDTH = 512
N_EXPERTS = 32
TOP_K = 4
D_FF = 2048
SWIGLU_LIMIT = 7.0
SWIGLU_ALPHA = 1.702
RMS_EPS = 1e-6
GN_EPS = 64e-5

LANES = 128
HEAD_PAIRS = 8

C_R, C_K, C_V = 0, 1024, 2048
C_FQ, C_FK, C_FV, C_FG = 3072, 4096, 5120, 6144
C_GL = 7168
C_WA = 7424
C_FL = 7552
Z_COLS = 7680

VMEM_LIMIT = 56 * 1024 * 1024
NEG_BIG = -1e30
LOG2E = 1.4426950408889634


def _cparams(sem):
    return pltpu.CompilerParams(dimension_semantics=sem, vmem_limit_bytes=VMEM_LIMIT)


def _block_ones():
    r = lax.broadcasted_iota(jnp.int32, (LANES, LANES), 0) // HEAD_DIM
    c = lax.broadcasted_iota(jnp.int32, (LANES, LANES), 1) // HEAD_DIM
    return (r == c).astype(BF16)


def _seg_sum64(x):
    ones = _block_ones()
    outs = []
    for s in range(x.shape[-1] // LANES):
        xs = x[:, s * LANES:(s + 1) * LANES]
        hi = xs.astype(BF16)
        lo = (xs - hi.astype(F32)).astype(BF16)
        outs.append(jnp.dot(hi, ones, preferred_element_type=F32)
                    + jnp.dot(lo, ones, preferred_element_type=F32))
    return outs[0] if len(outs) == 1 else jnp.concatenate(outs, axis=-1)


def _softplus(x):
    return jnp.maximum(x, 0.0) + jnp.log(1.0 + jnp.exp(-jnp.abs(x)))


def _sigmoid(x):
    return 1.0 / (1.0 + jnp.exp(-x))


def _norm_mm_kernel(x_ref, g_ref, w_ref, o_ref, xn_ref):
    @pl.when(pl.program_id(1) == 0)
    def _():
        x = x_ref[...]
        ms = jnp.mean(x * x, axis=-1, keepdims=True)
        xn_ref[...] = (x * lax.rsqrt(ms + RMS_EPS) * g_ref[...]).astype(BF16)

    o_ref[...] = jnp.dot(xn_ref[...], w_ref[...], preferred_element_type=F32)


def _norm_mm(x, g, w, tm, tn):
    m, k = x.shape
    n = w.shape[1]
    return pl.pallas_call(
        _norm_mm_kernel,
        grid=(m // tm, n // tn),
        in_specs=[pl.BlockSpec((tm, k), lambda i, j: (i, 0)),
                  pl.BlockSpec((1, k), lambda i, j: (0, 0)),
                  pl.BlockSpec((k, tn), lambda i, j: (0, j))],
        out_specs=pl.BlockSpec((tm, tn), lambda i, j: (i, j)),
        out_shape=jax.ShapeDtypeStruct((m, n), F32),
        scratch_shapes=[pltpu.VMEM((tm, k), BF16)],
        compiler_params=_cparams(("parallel", "arbitrary")),
        name="norm_mm",
    )(x, g, w)


def _rwkv_prep_kernel(zm_ref, zg_ref, zw_ref, pm_ref, pg_ref, pw_ref,
                      mum_ref, mug_ref, muw_ref, w0_ref, w2_ref, a0_ref, a2_ref, g2_ref,
                      kk_ref, ka_ref, rk_ref,
                      r_o, d_o, k_o, v_o, kk_o, b_o, g_o, bonus_o):
    def shifted(z_ref, p_ref, mu_ref):
        z = z_ref[...]
        rolled = pltpu.roll(z, 1, 0)
        row = lax.broadcasted_iota(jnp.int32, z.shape, 0)
        prev = jnp.where(row == 0, p_ref[0], rolled)
        return z + mu_ref[...] * (prev - z)

    zm = shifted(zm_ref, pm_ref, mum_ref)
    zg = shifted(zg_ref, pg_ref, mug_ref)
    zw = shifted(zw_ref, pw_ref, muw_ref)
    r = zm[:, C_R:C_R + RWKV_WIDTH]
    k = zm[:, C_K:C_K + RWKV_WIDTH]
    v = zm[:, C_V:C_V + RWKV_WIDTH]
    lw = jnp.dot(jnp.tanh(zw).astype(BF16), w2_ref[...], preferred_element_type=F32)
    la = jnp.dot(zw.astype(BF16), a2_ref[...], preferred_element_type=F32)
    g = jnp.dot(_sigmoid(zg).astype(BF16), g2_ref[...], preferred_element_type=F32)
    w_log = -_softplus(-(w0_ref[...] + lw)) - 0.5
    a = _sigmoid(a0_ref[...] + la)
    kk = k * kk_ref[...]
    kk = kk * lax.rsqrt(jnp.maximum(_seg_sum64(kk * kk), 1e-24))
    kh = k * (1.0 + (a - 1.0) * ka_ref[...])
    tiles = lambda x: x.reshape(x.shape[0], HEAD_PAIRS, LANES)
    r_o[...] = tiles(r)
    d_o[...] = tiles(jnp.exp(-jnp.exp(w_log)))
    k_o[...] = tiles(kh)
    v_o[...] = tiles(v)
    kk_o[...] = tiles(kk)
    b_o[...] = tiles(kk * a)
    g_o[...] = g
    bonus_o[...] = _seg_sum64(r * kh * rk_ref[...]) * v


def _rwkv_prep(z, prev_m, prev_g, prev_w, pr, tm):
    n = z.shape[0]
    row = lambda w, c: pl.BlockSpec((tm, w), lambda i, c=c: (i, c))
    prev = lambda w: pl.BlockSpec((1, 1, w), lambda i: (i, 0, 0))
    full = lambda a: pl.BlockSpec(a.shape, lambda i: (0,) * a.ndim)
    params = [pr["mu_m"], pr["mu_g"], pr["mu_w"], pr["w0"], pr["w2"], pr["a0"], pr["a2"], pr["g2"],
              pr["k_k"], pr["k_a"], pr["r_k"]]
    tiled = jax.ShapeDtypeStruct((n, HEAD_PAIRS, LANES), F32)
    flat = jax.ShapeDtypeStruct((n, RWKV_WIDTH), F32)
    return pl.pallas_call(
        _rwkv_prep_kernel,
        grid=(n // tm,),
        in_specs=[row(3072, 0), row(256, C_GL // 256), row(128, C_WA // 128),
                  prev(3072), prev(256), prev(128)] + [full(a) for a in params],
        out_specs=[pl.BlockSpec((tm, HEAD_PAIRS, LANES), lambda i: (i, 0, 0))] * 6
        + [pl.BlockSpec((tm, RWKV_WIDTH), lambda i: (i, 0))] * 2,
        out_shape=[tiled] * 6 + [flat] * 2,
        compiler_params=_cparams(("parallel",)),
        name="rwkv_prep",
    )(z, z, z, prev_m, prev_g, prev_w, *params)


SEQ_PER_STEP = 2
SCAN_BLOCK = 128


def _rwkv_scan_kernel(r_ref, d_ref, k_ref, v_ref, kk_ref, b_ref, g_ref, bonus_ref, s0_ref,
                      lw_ref, lb_ref,
                      y_ref, sf_ref,
                      s_ref, oa_ref, *, tb):
    tblk = pl.program_id(1)

    @pl.when(tblk == 0)
    def _():
        s_ref[...] = s0_ref[...]

    oa_ref[...] = jnp.zeros_like(oa_ref)
    r2 = lax.broadcasted_iota(jnp.int32, (2 * LANES, 2 * LANES), 0) // HEAD_DIM
    c2 = lax.broadcasted_iota(jnp.int32, (2 * LANES, 2 * LANES), 1) // HEAD_DIM
    ones = (r2 == c2).astype(BF16)
    lane = lax.broadcasted_iota(jnp.int32, (HEAD_DIM, LANES), 1)
    sub = lax.broadcasted_iota(jnp.int32, (HEAD_DIM, LANES), 0)
    diag = (lane % HEAD_DIM) == sub
    tiles = [(s, hp) for s in range(SEQ_PER_STEP) for hp in range(HEAD_PAIRS)]

    def bcast(ref, s, t, hp):
        return jnp.broadcast_to(ref[s, t, hp:hp + 1, :], (HEAD_DIM, LANES))

    def seg_sums(parts):
        lhs = jnp.concatenate([jnp.concatenate(parts[j:j + 2], axis=1) for j in range(0, len(parts), 2)], axis=0)
        out = jnp.dot(lhs, ones, preferred_element_type=F32)
        return [out[(j // 2) * HEAD_DIM:(j // 2 + 1) * HEAD_DIM, (j % 2) * LANES:(j % 2 + 1) * LANES]
                for j in range(len(parts))]

    def emit_outputs(t_out, valid):
        half = pl.multiple_of((t_out // HEAD_DIM) * HEAD_DIM, HEAD_DIM)
        hit = jnp.logical_and((lane % HEAD_DIM) == (t_out % HEAD_DIM), valid)
        ob = seg_sums([(s_ref[s, hp] * bcast(r_ref, s, t_out, hp)).astype(BF16) for s, hp in tiles])
        for j, (s, hp) in enumerate(tiles):
            cur = oa_ref[s, hp, pl.ds(half, HEAD_DIM), :]
            oa_ref[s, hp, pl.ds(half, HEAD_DIM), :] = jnp.where(hit, ob[j], cur)

    def step(t, carry):
        emit_outputs(jnp.maximum(t - 1, 0), t > 0)
        skk = seg_sums([(s_ref[s, hp] * bcast(kk_ref, s, t, hp)).astype(BF16) for s, hp in tiles])
        vb = seg_sums([jnp.where(diag, bcast(v_ref, s, t, hp), 0.0).astype(BF16) for s, hp in tiles])
        for j, (s, hp) in enumerate(tiles):
            s_ref[s, hp] = (s_ref[s, hp] * bcast(d_ref, s, t, hp)
                            - skk[j] * bcast(b_ref, s, t, hp)
                            + vb[j] * bcast(k_ref, s, t, hp))
        return carry

    lax.fori_loop(0, tb, step, 0)
    emit_outputs(tb - 1, True)

    lane_t = lax.broadcasted_iota(jnp.int32, (HEAD_DIM, LANES), 1)
    low = lane_t < HEAD_DIM
    for s in range(SEQ_PER_STEP):
        slabs = []
        for hp in range(HEAD_PAIRS):
            mt = oa_ref[s, hp].T
            ro = pltpu.roll(mt, HEAD_DIM, 1)
            top = jnp.where(low, mt[:HEAD_DIM], ro[HEAD_DIM:])
            bot = jnp.where(low, ro[:HEAD_DIM], mt[HEAD_DIM:])
            slabs.append(jnp.concatenate([top, bot], axis=0)[:tb])
        o = jnp.concatenate(slabs, axis=-1)
        mean = _seg_sum64(o) * (1.0 / HEAD_DIM)
        cen = o - mean
        var = _seg_sum64(cen * cen) * (1.0 / HEAD_DIM)
        on = cen * lax.rsqrt(var + GN_EPS) * lw_ref[...] + lb_ref[...]
        y_ref[s] = ((on + bonus_ref[s]) * g_ref[s]).astype(y_ref.dtype)

    @pl.when(tblk == pl.num_programs(1) - 1)
    def _():
        sf_ref[...] = s_ref[...]


def _rwkv_scan(steps, g, bonus, s0, lnw, lnb, tb):
    nseq, t = g.shape[:2]
    step_spec = pl.BlockSpec((SEQ_PER_STEP, tb, HEAD_PAIRS, LANES), lambda i, j: (i, j, 0, 0))
    seq_spec = pl.BlockSpec((SEQ_PER_STEP, tb, RWKV_WIDTH), lambda i, j: (i, j, 0))
    st_spec = pl.BlockSpec((SEQ_PER_STEP, HEAD_PAIRS, HEAD_DIM, LANES), lambda i, j: (i, 0, 0, 0))
    par_spec = pl.BlockSpec((1, RWKV_WIDTH), lambda i, j: (0, 0))
    return pl.pallas_call(
        functools.partial(_rwkv_scan_kernel, tb=tb),
        grid=(nseq // SEQ_PER_STEP, t // tb),
        in_specs=[step_spec] * 6 + [seq_spec] * 2 + [st_spec] + [par_spec] * 2,
        out_specs=[seq_spec, st_spec],
        out_shape=[jax.ShapeDtypeStruct((nseq, t, RWKV_WIDTH), BF16),
                   jax.ShapeDtypeStruct(s0.shape, F32)],
        scratch_shapes=[pltpu.VMEM((SEQ_PER_STEP, HEAD_PAIRS, HEAD_DIM, LANES), F32),
                        pltpu.VMEM((SEQ_PER_STEP, HEAD_PAIRS, LANES, LANES), F32)],
        compiler_params=_cparams(("parallel", "arbitrary")),
        name="rwkv_scan",
    )(*steps, g, bonus, s0, lnw, lnb)


def _state_to_pairs(s):
    b = s.shape[0]
    return s.reshape(b, HEAD_PAIRS, 2, HEAD_DIM, HEAD_DIM).transpose(0, 1, 3, 2, 4).reshape(
        b, HEAD_PAIRS, HEAD_DIM, LANES)


def _pairs_to_state(s):
    b = s.shape[0]
    return s.reshape(b, HEAD_PAIRS, HEAD_DIM, 2, HEAD_DIM).transpose(0, 1, 3, 2, 4).reshape(
        b, RWKV_HEADS, HEAD_DIM, HEAD_DIM)


def _fox_proj_kernel(q_ref, k_ref, v_ref, fl_ref, qg_ref, kg_ref, bf_ref, qb_o, kb_o, lf_o, *cache_o, dim_major):
    def headnorm(x, g):
        ms = _seg_sum64(x * x) * (1.0 / HEAD_DIM)
        return x * lax.rsqrt(ms + RMS_EPS) * g

    q = headnorm(q_ref[...], qg_ref[...])
    k = headnorm(k_ref[...], kg_ref[...])
    q = q * (HEAD_DIM ** -0.5 * LOG2E)
    lf = -_softplus(-(fl_ref[...] + bf_ref[...]))
    lf_o[...] = lf
    if dim_major:
        lane = lax.broadcasted_iota(jnp.int32, (q.shape[0], LANES), 1)
        for h in range(FOX_HEADS):
            slab = slice((h // 2) * LANES, (h // 2 + 1) * LANES)
            for x, o in ((q, qb_o), (k, kb_o)):
                xs = x[:, slab] if h % 2 == 0 else pltpu.roll(x[:, slab], HEAD_DIM, 1)
                o[0, h] = jnp.where(lane < HEAD_DIM, xs, 0.0).astype(BF16)
        kt_o, vt_o, vtb_o, lft_o = cache_o
        vt = v_ref[...].T
        kt_o[0] = k.T
        vt_o[0] = vt
        vtb_o[0] = vt.astype(BF16)
        lft_o[0] = lf.T[:FOX_HEADS]
    else:
        qb_o[...] = q.astype(BF16)
        kb_o[...] = k.astype(BF16)
        cache_o[0][...] = k


def _fox_proj(z, qg, kg, bf, tm, nseq=None):
    n = z.shape[0]
    col = lambda w, c: pl.BlockSpec((tm, w), lambda i, c=c: (i, c))
    par = lambda w: pl.BlockSpec((1, w), lambda i: (0, 0))
    wide = pl.BlockSpec((tm, FOX_WIDTH), lambda i: (i, 0))
    lf_spec = pl.BlockSpec((tm, LANES), lambda i: (i, 0))
    lf_shape = jax.ShapeDtypeStruct((n, LANES), F32)
    if nseq is None:
        out_specs = [wide, wide, lf_spec, wide]
        out_shape = [jax.ShapeDtypeStruct((n, FOX_WIDTH), BF16), jax.ShapeDtypeStruct((n, FOX_WIDTH), BF16),
                     lf_shape, jax.ShapeDtypeStruct((n, FOX_WIDTH), F32)]
    else:
        t = n // nseq
        per = t // tm
        heads = pl.BlockSpec((1, FOX_HEADS, tm, LANES), lambda i: (i // per, 0, i % per, 0))
        tall = pl.BlockSpec((1, FOX_WIDTH, tm), lambda i: (i // per, 0, i % per))
        padded = jax.ShapeDtypeStruct((nseq, FOX_HEADS, t, LANES), BF16)
        out_specs = [heads, heads, lf_spec, tall, tall, tall,
                     pl.BlockSpec((1, FOX_HEADS, tm), lambda i: (i // per, 0, i % per))]
        out_shape = [padded, padded, lf_shape,
                     jax.ShapeDtypeStruct((nseq, FOX_WIDTH, t), F32), jax.ShapeDtypeStruct((nseq, FOX_WIDTH, t), F32),
                     jax.ShapeDtypeStruct((nseq, FOX_WIDTH, t), BF16),
                     jax.ShapeDtypeStruct((nseq, FOX_HEADS, t), F32)]
    return pl.pallas_call(
        functools.partial(_fox_proj_kernel, dim_major=nseq is not None),
        grid=(n // tm,),
        in_specs=[col(1024, C_FQ // 1024), col(1024, C_FK // 1024), col(1024, C_FV // 1024), col(128, C_FL // 128),
                  par(1024), par(1024), par(128)],
        out_specs=out_specs,
        out_shape=out_shape,
        compiler_params=_cparams(("parallel",)),
        name="fox_proj",
    )(z, z, z, z, qg, kg, bf)


F_TERMS = 8


def _fox_prompt_kernel(qi_ref, kj_ref, q_ref, k_ref, fq_ref, fk_ref, vt_ref, gate_ref, y_ref,
                       qa_ref, m_ref, l_ref, acc_ref, *, tq):
    pair = pl.program_id(2)
    qi = qi_ref[pair]
    kj = kj_ref[pair]
    lane = lax.broadcasted_iota(jnp.int32, (tq, LANES), 1)
    zero = jnp.zeros((tq, LANES), BF16)

    def with_terms(x, f, h):
        lo = HEAD_DIM + F_TERMS * h
        return x + jnp.where(jnp.logical_and(lane >= lo, lane < lo + F_TERMS), f, zero)

    @pl.when(kj == 0)
    def _():
        m_ref[...] = jnp.full_like(m_ref, NEG_BIG)
        l_ref[...] = jnp.zeros_like(l_ref)
        acc_ref[...] = jnp.zeros_like(acc_ref)
        for h in range(2):
            qa_ref[h] = with_terms(q_ref[0, h], fq_ref[0, 0], h)

    def body(masked):
        for h in range(2):
            st = lax.dot_general(with_terms(k_ref[0, h], fk_ref[0, 0], h), qa_ref[h], (((1,), (1,)), ((), ())),
                                 preferred_element_type=F32)
            if masked:
                key = lax.broadcasted_iota(jnp.int32, st.shape, 0)
                qry = lax.broadcasted_iota(jnp.int32, st.shape, 1)
                st = jnp.where(key <= qry, st, NEG_BIG)
            m_prev = m_ref[h]
            m_new = jnp.maximum(m_prev, jnp.max(st, axis=0, keepdims=True))
            alpha = jnp.exp2(m_prev - m_new)
            p = jnp.exp2(st - m_new)
            l_ref[h] = alpha * l_ref[h] + jnp.sum(p, axis=0, keepdims=True)
            acc_ref[h] = alpha * acc_ref[h] + jnp.dot(vt_ref[0, h], p.astype(BF16), preferred_element_type=F32)
            m_ref[h] = m_new

    @pl.when(kj < qi)
    def _():
        body(False)

    @pl.when(kj == qi)
    def _():
        body(True)
        ot = jnp.concatenate([acc_ref[0] / l_ref[0], acc_ref[1] / l_ref[1]], axis=0)
        y_ref[...] = (ot.T * _sigmoid(gate_ref[...])).astype(y_ref.dtype)


def _fox_prompt(q_pad, k_pad, fq, fk, vt, z, nb, seq, tq):
    nq = seq // tq
    pairs = [(i, j) for i in range(nq) for j in range(i + 1)]
    qi_tab = jnp.asarray([p[0] for p in pairs], jnp.int32)
    kj_tab = jnp.asarray([p[1] for p in pairs], jnp.int32)
    grid_spec = pltpu.PrefetchScalarGridSpec(
        num_scalar_prefetch=2,
        grid=(nb, HEAD_PAIRS, len(pairs)),
        in_specs=[pl.BlockSpec((1, 2, tq, LANES), lambda b, hp, p, qi, kj: (b, hp, qi[p], 0)),
                  pl.BlockSpec((1, 2, tq, LANES), lambda b, hp, p, qi, kj: (b, hp, kj[p], 0)),
                  pl.BlockSpec((1, 1, tq, LANES), lambda b, hp, p, qi, kj: (b, hp, qi[p], 0)),
                  pl.BlockSpec((1, 1, tq, LANES), lambda b, hp, p, qi, kj: (b, hp, kj[p], 0)),
                  pl.BlockSpec((1, 2, HEAD_DIM, tq), lambda b, hp, p, qi, kj: (b, hp, 0, kj[p])),
                  pl.BlockSpec((tq, LANES), lambda b, hp, p, qi, kj: (b * nq + qi[p], C_FG // LANES + hp))],
        out_specs=pl.BlockSpec((tq, LANES), lambda b, hp, p, qi, kj: (b * nq + qi[p], hp)),
        scratch_shapes=[pltpu.VMEM((2, tq, LANES), BF16), pltpu.VMEM((2, 1, tq), F32), pltpu.VMEM((2, 1, tq), F32),
                        pltpu.VMEM((2, HEAD_DIM, tq), F32)],
    )
    return pl.pallas_call(
        functools.partial(_fox_prompt_kernel, tq=tq),
        grid_spec=grid_spec,
        out_shape=jax.ShapeDtypeStruct((nb * seq, FOX_WIDTH), BF16),
        compiler_params=_cparams(("parallel", "parallel", "arbitrary")),
        name="fox_prompt",
    )(qi_tab, kj_tab, q_pad, k_pad, fq, fk, vt, z)


def _split3(x):
    def top(v):
        bits = lax.bitcast_convert_type(v, jnp.uint32) & jnp.uint32(0xFFFF0000)
        return lax.bitcast_convert_type(bits, F32)

    hi = top(x)
    mid = top(x - hi)
    lo = top(x - hi - mid)
    return hi.astype(BF16), mid.astype(BF16), lo.astype(BF16)


def _prefix_terms(f2, nb, seq):
    one = jnp.ones_like(f2, BF16)
    none = jnp.zeros_like(f2, BF16)

    def lanes(terms):
        t = jnp.stack([*terms, none, none], axis=-1)
        t = t.reshape(nb, HEAD_PAIRS, 2, seq, F_TERMS).transpose(0, 1, 3, 2, 4).reshape(nb, HEAD_PAIRS, seq, 16)
        return jnp.pad(t, ((0, 0), (0, 0), (0, 0), (HEAD_DIM, LANES - HEAD_DIM - 16)))

    return lanes([*_split3(f2), one, one, one]), lanes([one, one, one, *_split3(-f2)])


def _fox_sample_kernel(q_ref, kc_ref, vc_ref, kn_ref, vn_ref, fq_ref, fkc_ref, fkn_ref, gate_ref, y_ref,
                       m_ref, l_ref, acc_ref, *, nk):
    kj = pl.program_id(1)

    @pl.when(kj == 0)
    def _():
        m_ref[...] = jnp.full_like(m_ref, NEG_BIG)
        l_ref[...] = jnp.zeros_like(l_ref)
        acc_ref[...] = jnp.zeros_like(acc_ref)

    def head_slice(ref, h):
        return ref[:, h * HEAD_DIM:(h + 1) * HEAD_DIM]

    t = q_ref.shape[0]

    nt_dims = (((1,), (1,)), ((), ()))

    def attend(ks, vs, fk_ref, masked, dim_major):
        if dim_major:
            qk = lambda h: jnp.dot(head_slice(q_ref, h), ks[h], preferred_element_type=F32)
            pv_of = lambda ph, h: lax.dot_general(ph, vs[h], nt_dims, preferred_element_type=F32)
        else:
            qk = lambda h: lax.dot_general(head_slice(q_ref, h), ks[h], nt_dims, preferred_element_type=F32)
            pv_of = lambda ph, h: jnp.dot(ph, vs[h], preferred_element_type=F32)
        s = jnp.concatenate([qk(h) for h in range(FOX_HEADS)], axis=0)
        fk = jnp.concatenate([jnp.broadcast_to(fk_ref[0, h:h + 1, :], (t, s.shape[1])) for h in range(FOX_HEADS)],
                             axis=0)
        s = s + fq_ref[0] - fk
        if masked:
            qry = lax.broadcasted_iota(jnp.int32, s.shape, 0) % t
            key = lax.broadcasted_iota(jnp.int32, s.shape, 1)
            s = jnp.where(key <= qry, s, NEG_BIG)
        m_prev = m_ref[...]
        m_new = jnp.maximum(m_prev, jnp.max(s, axis=-1, keepdims=True))
        alpha = jnp.exp2(m_prev - m_new)
        p = jnp.exp2(s - m_new)
        l_ref[...] = alpha * l_ref[...] + jnp.sum(p, axis=-1, keepdims=True)
        p = p.astype(BF16)
        pv = jnp.concatenate([pv_of(p[h * t:(h + 1) * t], h) for h in range(FOX_HEADS)], axis=0)
        acc_ref[...] = alpha * acc_ref[...] + pv
        m_ref[...] = m_new

    @pl.when(kj < nk)
    def _():
        attend([kc_ref[0, h].astype(BF16) for h in range(FOX_HEADS)],
               [vc_ref[0, h].astype(BF16) for h in range(FOX_HEADS)], fkc_ref, False, True)

    @pl.when(kj == nk)
    def _():
        attend([head_slice(kn_ref, h) for h in range(FOX_HEADS)],
               [head_slice(vn_ref, h).astype(BF16) for h in range(FOX_HEADS)], fkn_ref, True, False)
        o = acc_ref[...] / l_ref[...]
        for h in range(FOX_HEADS):
            gate = _sigmoid(head_slice(gate_ref, h))
            y_ref[:, h * HEAD_DIM:(h + 1) * HEAD_DIM] = (o[h * t:(h + 1) * t] * gate).astype(y_ref.dtype)


def _fox_sample(qb, kc, vc, kb, z, fq, fkc, fkn, nb, t, tk):
    past = kc.shape[3]
    nk = past // tk
    last = nk - 1
    wide = lambda c: pl.BlockSpec((t, FOX_WIDTH), lambda b, j, c=c: (b, c))
    cache = pl.BlockSpec((1, FOX_HEADS, HEAD_DIM, tk), lambda b, j: (b, 0, 0, jnp.minimum(j, last)))
    return pl.pallas_call(
        functools.partial(_fox_sample_kernel, nk=nk),
        grid=(nb, nk + 1),
        in_specs=[wide(0), cache, cache, wide(0), wide(C_FV // FOX_WIDTH),
                  pl.BlockSpec((1, FOX_HEADS * t, 1), lambda b, j: (b, 0, 0)),
                  pl.BlockSpec((1, FOX_HEADS, tk), lambda b, j: (b, 0, jnp.minimum(j, last))),
                  pl.BlockSpec((1, FOX_HEADS, t), lambda b, j: (b, 0, 0)),
                  wide(C_FG // FOX_WIDTH)],
        out_specs=wide(0),
        out_shape=jax.ShapeDtypeStruct((nb * t, FOX_WIDTH), BF16),
        scratch_shapes=[pltpu.VMEM((FOX_HEADS * t, 1), F32), pltpu.VMEM((FOX_HEADS * t, 1), F32),
                        pltpu.VMEM((FOX_HEADS * t, HEAD_DIM), F32)],
        compiler_params=_cparams(("parallel", "arbitrary")),
        name="fox_sample",
    )(qb, kc, vc, kb, z, fq, fkc, fkn, z)


def _out_proj_kernel(x_ref, ya_ref, yb_ref, wa_ref, wb_ref, o_ref):
    o_ref[...] = (x_ref[...]
                  + jnp.dot(ya_ref[...], wa_ref[...], preferred_element_type=F32)
                  + jnp.dot(yb_ref[...], wb_ref[...], preferred_element_type=F32))


def _out_proj(x, ya, yb, w, tm):
    n = x.shape[0]
    return pl.pallas_call(
        _out_proj_kernel,
        grid=(n // tm,),
        in_specs=[pl.BlockSpec((tm, D_MODEL), lambda i: (i, 0)),
                  pl.BlockSpec((tm, RWKV_WIDTH), lambda i: (i, 0)),
                  pl.BlockSpec((tm, FOX_WIDTH), lambda i: (i, 0)),
                  pl.BlockSpec((RWKV_WIDTH, D_MODEL), lambda i: (0, 0)),
                  pl.BlockSpec((FOX_WIDTH, D_MODEL), lambda i: (1, 0))],
        out_specs=pl.BlockSpec((tm, D_MODEL), lambda i: (i, 0)),
        out_shape=jax.ShapeDtypeStruct((n, D_MODEL), F32),
        compiler_params=_cparams(("parallel",)),
        name="out_proj",
    )(x, ya, yb, w, w)


def _mem_kv_kernel(x_ref, g_ref, w_ref, kg_ref, o_ref):
    x = x_ref[...]
    ms = jnp.mean(x * x, axis=-1, keepdims=True)
    xn = (x * lax.rsqrt(ms + RMS_EPS) * g_ref[...]).astype(BF16)
    kv = jnp.dot(xn, w_ref[...], preferred_element_type=F32)
    for h in range(MEM_HEADS):
        kh = kv[:, h * MEM_HEAD_DIM:(h + 1) * MEM_HEAD_DIM]
        ms = jnp.mean(kh * kh, axis=-1, keepdims=True)
        o_ref[:, h * MEM_HEAD_DIM:(h + 1) * MEM_HEAD_DIM] = kh * lax.rsqrt(ms + RMS_EPS) * kg_ref[...]
    o_ref[:, MEM_WIDTH:] = kv[:, MEM_WIDTH:]


def _mem_kv(mem, g, w_kv, kg, tm):
    n = mem.shape[0]
    return pl.pallas_call(
        _mem_kv_kernel,
        grid=(n // tm,),
        in_specs=[pl.BlockSpec((tm, D_MODEL), lambda i: (i, 0)),
                  pl.BlockSpec((1, D_MODEL), lambda i: (0, 0)),
                  pl.BlockSpec((D_MODEL, 2 * MEM_WIDTH), lambda i: (0, 0)),
                  pl.BlockSpec((1, MEM_HEAD_DIM), lambda i: (0, 0))],
        out_specs=pl.BlockSpec((tm, 2 * MEM_WIDTH), lambda i: (i, 0)),
        out_shape=jax.ShapeDtypeStruct((n, 2 * MEM_WIDTH), F32),
        compiler_params=_cparams(("parallel",)),
        name="mem_kv",
    )(mem, g, w_kv, kg)


def _cross_router_kernel(x_ref, gc_ref, wq_ref, qg_ref, mk_ref, mv_ref, wo_ref,
                         gf_ref, wr_ref, br_ref,
                         x2_ref, h_ref, idx_ref, gate_ref):
    x = x_ref[...]
    ms = jnp.mean(x * x, axis=-1, keepdims=True)
    xn = (x * lax.rsqrt(ms + RMS_EPS) * gc_ref[...]).astype(BF16)
    q = jnp.dot(xn, wq_ref[...], preferred_element_type=F32)
    outs = []
    for h in range(MEM_HEADS):
        sl = slice(h * MEM_HEAD_DIM, (h + 1) * MEM_HEAD_DIM)
        qh = q[:, sl]
        qms = jnp.mean(qh * qh, axis=-1, keepdims=True)
        qh = (qh * lax.rsqrt(qms + RMS_EPS) * qg_ref[...]).astype(BF16)
        s = lax.dot_general(qh, mk_ref[0, :, sl], (((1,), (1,)), ((), ())), preferred_element_type=F32)
        s = s * (MEM_HEAD_DIM ** -0.5)
        p = jnp.exp(s - jnp.max(s, axis=-1, keepdims=True))
        p = p / jnp.sum(p, axis=-1, keepdims=True)
        outs.append(jnp.dot(p.astype(BF16), mv_ref[0, :, sl], preferred_element_type=F32))
    o = jnp.concatenate(outs, axis=-1).astype(BF16)
    x2 = x + jnp.dot(o, wo_ref[...], preferred_element_type=F32)
    x2_ref[...] = x2

    ms2 = jnp.mean(x2 * x2, axis=-1, keepdims=True)
    hn = x2 * lax.rsqrt(ms2 + RMS_EPS) * gf_ref[...]
    h_ref[...] = hn.astype(BF16)
    logits = jnp.dot(hn, wr_ref[...], preferred_element_type=F32, precision=lax.Precision.HIGHEST)
    logits = logits + br_ref[...]
    lane = lax.broadcasted_iota(jnp.int32, logits.shape, 1)
    idx_acc = jnp.zeros(logits.shape, jnp.int32)
    val_acc = jnp.zeros(logits.shape, F32)
    top = None
    for kk in range(TOP_K):
        m = jnp.max(logits, axis=-1, keepdims=True)
        sel = jnp.min(jnp.where(logits == m, lane, LANES), axis=-1, keepdims=True)
        if kk == 0:
            top = m
        idx_acc = jnp.where(lane == kk, sel, idx_acc)
        val_acc = jnp.where(lane == kk, jnp.exp(m - top), val_acc)
        logits = jnp.where(lane == sel, -jnp.inf, logits)
    idx_ref[...] = idx_acc
    gate_ref[...] = val_acc / jnp.sum(val_acc, axis=-1, keepdims=True)


def _cross_router(x, mk, mv, pc, nb, tm):
    n = x.shape[0]
    per = n // nb // tm
    full = lambda a: pl.BlockSpec(a.shape, lambda i: (0,) * a.ndim)
    mem = pl.BlockSpec((1, N_MEM, MEM_WIDTH), lambda i: (i // per, 0, 0))
    row = lambda w: pl.BlockSpec((tm, w), lambda i: (i, 0))
    params1 = [pc["g_cross"], pc["w_xq"], pc["xq_gain"]]
    params2 = [pc["w_xo"], pc["g_ffn"], pc["w_router"], pc["b_router"]]
    return pl.pallas_call(
        _cross_router_kernel,
        grid=(n // tm,),
        in_specs=[row(D_MODEL)] + [full(a) for a in params1] + [mem, mem] + [full(a) for a in params2],
        out_specs=[row(D_MODEL), row(D_MODEL), row(LANES), row(LANES)],
        out_shape=[jax.ShapeDtypeStruct((n, D_MODEL), F32), jax.ShapeDtypeStruct((n, D_MODEL), BF16),
                   jax.ShapeDtypeStruct((n, LANES), jnp.int32), jax.ShapeDtypeStruct((n, LANES), F32)],
        compiler_params=_cparams(("parallel",)),
        name="cross_router",
    )(x, *params1, mk, mv, *params2)


MOE_TM = 768
MOE_TF = 256
MOE_TN = 512
MOE_NG = D_FF // (2 * MOE_TF)
MOE_ND = D_MODEL // MOE_TN
MOE_CALLS = 3


def _moe_kernel(te_ref, tv_ref, xs_ref, wgu_ref, bgu_ref, wd_ref, bd_ref, *refs):
    o_ref, act_ref = refs[-2:]
    _moe_tile_step(tv_ref, xs_ref, wgu_ref, bgu_ref, wd_ref, bd_ref, o_ref, act_ref)


def _moe_tile_step(tv_ref, xs_ref, wgu_ref, bgu_ref, wd_ref, bd_ref, o_ref, act_ref):
    s = pl.program_id(1)
    used = tv_ref[pl.program_id(0)] > 0
    half = 2 * MOE_TF

    @pl.when(jnp.logical_and(used, s < MOE_NG))
    def _():
        gu = jnp.dot(xs_ref[...], wgu_ref[0].astype(BF16), preferred_element_type=F32) + bgu_ref[0]
        glu = jnp.minimum(gu, SWIGLU_LIMIT)
        fglu = glu * _sigmoid(SWIGLU_ALPHA * glu)
        lin = jnp.clip(gu, -SWIGLU_LIMIT, SWIGLU_LIMIT) + 1.0
        act_a = pltpu.roll(fglu[:, :half], 1, 1) * lin[:, :half]
        act_b = fglu[:, half:] * pltpu.roll(lin[:, half:], half - 1, 1)
        lane = lax.broadcasted_iota(jnp.int32, act_a.shape, 1)
        merged = jnp.where(lane % 2 == 0, act_b, act_a).astype(BF16)
        for c in range(MOE_NG):
            @pl.when(s == c)
            def _(c=c):
                act_ref[:, c * half:(c + 1) * half] = merged

    @pl.when(jnp.logical_and(used, s >= MOE_NG))
    def _():
        top = lambda w: lax.bitcast_convert_type(w.astype(BF16).astype(F32), jnp.uint32)
        chunks = []
        for c in range(MOE_NG):
            a = wd_ref[0, c * half:c * half + MOE_TF, :]
            b = wd_ref[0, c * half + MOE_TF:(c + 1) * half, :]
            chunks.append(pltpu.bitcast((top(b) >> 16) | top(a), BF16))
        wd = jnp.concatenate(chunks, axis=0)
        o_ref[...] = jnp.dot(act_ref[...], wd, preferred_element_type=F32) + bd_ref[0]

    @pl.when(jnp.logical_and(jnp.logical_not(used), s >= MOE_NG))
    def _():
        o_ref[...] = jnp.zeros_like(o_ref)


def _moe_experts(tile_expert, tile_valid, xs, w_gu, b_gu, w_down, b_down, p_total, tile_off, out_prev=None):
    nt = xs.shape[0] // MOE_TM
    gi = lambda i, s, tv: jnp.where(tv[i] > 0, jnp.minimum(s, MOE_NG - 1), MOE_NG - 1)
    di = lambda i, s, tv: jnp.where(tv[i] > 0, jnp.maximum(s - MOE_NG, 0), MOE_ND - 1)
    in_specs = [pl.BlockSpec((MOE_TM, D_MODEL), lambda i, s, te, tv: (i, 0)),
                pl.BlockSpec((1, D_MODEL, 4 * MOE_TF), lambda i, s, te, tv: (te[i], 0, gi(i, s, tv))),
                pl.BlockSpec((1, 1, 4 * MOE_TF), lambda i, s, te, tv: (te[i], 0, gi(i, s, tv))),
                pl.BlockSpec((1, D_FF, MOE_TN), lambda i, s, te, tv: (te[i], 0, di(i, s, tv))),
                pl.BlockSpec((1, 1, MOE_TN), lambda i, s, te, tv: (te[i], 0, di(i, s, tv)))]
    args = [tile_expert, tile_valid, xs, w_gu, b_gu, w_down, b_down]
    aliases = {}
    if out_prev is not None:
        in_specs.append(pl.BlockSpec(memory_space=pl.ANY))
        aliases = {len(args): 0}
        args.append(out_prev)
    grid_spec = pltpu.PrefetchScalarGridSpec(
        num_scalar_prefetch=2,
        grid=(nt, MOE_NG + MOE_ND),
        in_specs=in_specs,
        out_specs=pl.BlockSpec((MOE_TM, MOE_TN),
                               lambda i, s, te, tv: (i + tile_off, jnp.maximum(s - MOE_NG, 0))),
        scratch_shapes=[pltpu.VMEM((MOE_TM, D_FF), BF16)],
    )
    return pl.pallas_call(
        _moe_kernel,
        grid_spec=grid_spec,
        out_shape=jax.ShapeDtypeStruct((p_total, D_MODEL), F32),
        input_output_aliases=aliases,
        compiler_params=_cparams(("arbitrary", "arbitrary")),
        name="moe_experts",
    )(*args)


def _route(top_i):
    n = top_i.shape[0]
    na = n * TOP_K
    flat_e = top_i.reshape(na)
    order = jnp.argsort(flat_e, stable=True).astype(jnp.int32)
    inv = jnp.argsort(order).astype(jnp.int32)
    counts = jnp.sum((flat_e[:, None] == jnp.arange(N_EXPERTS, dtype=jnp.int32)[None, :]).astype(jnp.int32),
                     axis=0)
    padded = ((counts + MOE_TM - 1) // MOE_TM) * MOE_TM
    pad_start = jnp.cumsum(padded) - padded
    start = jnp.cumsum(counts) - counts
    pos = (inv + (pad_start - start)[flat_e]).reshape(n, TOP_K)
    nt = na // MOE_TM + N_EXPERTS
    p_rows = nt * MOE_TM
    tile_start = jnp.arange(nt, dtype=jnp.int32) * MOE_TM
    pad_end = pad_start + padded
    n_used = (jnp.sum(padded) // MOE_TM).astype(jnp.int32)
    tile_expert = jnp.minimum(jnp.sum((tile_start[:, None] >= pad_end[None, :]).astype(jnp.int32), axis=1),
                              N_EXPERTS - 1)
    used = jnp.arange(nt) < n_used
    tile_expert = jnp.where(used, tile_expert, tile_expert[jnp.maximum(n_used - 1, 0)])
    tile_valid = jnp.where(used, jnp.clip(counts[tile_expert] - (tile_start - pad_start[tile_expert]), 0, MOE_TM), 0)
    row = jnp.arange(p_rows, dtype=jnp.int32)
    row_e = jnp.repeat(tile_expert, MOE_TM)
    rank = row - pad_start[row_e]
    src = jnp.clip(start[row_e] + rank, 0, na - 1)
    row_token = jnp.where(rank < counts[row_e], order[src] // TOP_K, 0)
    return row_token, pos, tile_expert.astype(jnp.int32), tile_valid.astype(jnp.int32)


def _pad_cols(a, width):
    return jnp.pad(a, ((0, 0), (0, width - a.shape[1])))


def _pack_cols(a):
    o_w = 3 * RWKV_WIDTH
    o_a = o_w + DECAY_LORA
    o_g = o_a + AAA_LORA
    fb = A_COLS
    return jnp.concatenate([
        a[:, 0:3 * RWKV_WIDTH],
        a[:, fb:fb + 4 * FOX_WIDTH],
        _pad_cols(a[:, o_g:A_COLS], 256),
        a[:, o_w:o_g],
        _pad_cols(a[:, fb + 4 * FOX_WIDTH:], LANES),
    ], axis=1)


def _unpack_a_cols(z):
    return jnp.concatenate([z[..., 0:3 * RWKV_WIDTH], z[..., C_WA:C_WA + 128], z[..., C_GL:C_GL + GATE_LORA]],
                           axis=-1)


def _prev_rows(z, tm, first):
    nseq, t, _ = z.shape
    if t > tm:
        inner = z[:, tm - 1:t - 1:tm]
        rows = jnp.concatenate([first, inner], axis=1)
    else:
        rows = first
    rows = rows.reshape(-1, 1, Z_COLS)
    return rows[..., 0:3072], rows[..., C_GL:C_GL + 256], rows[..., C_WA:C_WA + 128]


def kernel(x_prompt, x_sample, mem_prompt, cache_fox_k, cache_fox_v, cache_fox_logf, state_rwkv, state_shift,
           cache_mem_k, cache_mem_v, norm_mix, w_in, rwkv_mu, rwkv_w0, rwkv_w2, rwkv_a0, rwkv_a2, rwkv_g2,
           rwkv_k_k, rwkv_k_a, rwkv_r_k, rwkv_lnx_w, rwkv_lnx_b, fox_b_f, fox_q_norm, fox_k_norm, w_out,
           norm_cross, norm_mem, w_xq, w_xk, w_xv, w_xo, xq_norm, xk_norm, norm_ffn, w_router, b_router,
           w_gu, b_gu, w_down, b_down):
    bp, sp, _ = x_prompt.shape
    bs, ts, _ = x_sample.shape
    past = cache_fox_k.shape[2]
    n_p, n_s = bp * sp, bs * ts
    l = 0

    w_in_p = _pack_cols(w_in[l]).astype(BF16)
    mu_p = _pack_cols(jnp.pad(rwkv_mu[l][None, :], ((0, 0), (0, w_in.shape[2] - A_COLS))))
    zero_lora = jnp.zeros((DECAY_LORA, RWKV_WIDTH), F32)
    prep = {
        "mu_m": mu_p[:, 0:3072], "mu_g": mu_p[:, C_GL:C_GL + 256], "mu_w": mu_p[:, C_WA:C_WA + 128],
        "w0": rwkv_w0[l][None], "a0": rwkv_a0[l][None],
        "w2": jnp.concatenate([rwkv_w2[l], zero_lora], axis=0).astype(BF16),
        "a2": jnp.concatenate([zero_lora, rwkv_a2[l]], axis=0).astype(BF16),
        "g2": jnp.pad(rwkv_g2[l], ((0, 256 - GATE_LORA), (0, 0))).astype(BF16),
        "k_k": rwkv_k_k[l][None], "k_a": rwkv_k_a[l][None], "r_k": rwkv_r_k[l].reshape(1, RWKV_WIDTH),
    }
    lnw, lnb = rwkv_lnx_w[l][None], rwkv_lnx_b[l][None]
    qg = jnp.tile(fox_q_norm[l], FOX_HEADS)[None]
    kg = jnp.tile(fox_k_norm[l], FOX_HEADS)[None]
    bf = _pad_cols(fox_b_f[l][None], LANES)
    w_out_b = w_out[l].astype(BF16)
    w_kv = jnp.concatenate([w_xk[l], w_xv[l]], axis=1).astype(BF16)
    pc = {
        "g_cross": norm_cross[l][None], "w_xq": w_xq[l].astype(BF16), "xq_gain": xq_norm[l][None],
        "w_xo": w_xo[l].astype(BF16), "g_ffn": norm_ffn[l][None],
        "w_router": _pad_cols(w_router[l], LANES),
        "b_router": jnp.concatenate([b_router[l], jnp.full((LANES - N_EXPERTS,), NEG_BIG, F32)])[None],
    }

    def mixers(x, nseq, t, first_shift, s0, tm_mm, tm_prep, tb, dim_major):
        z = _norm_mm(x, norm_mix[l][None], w_in_p, tm_mm, 768)
        z3 = z.reshape(nseq, t, Z_COLS)
        pm, pg, pw = _prev_rows(z3, tm_prep, first_shift)
        *steps, g, bonus = _rwkv_prep(z, pm, pg, pw, prep, tm_prep)
        steps = [a.reshape(nseq, t, HEAD_PAIRS, LANES) for a in steps]
        ya, s_fin = _rwkv_scan(steps, g.reshape(nseq, t, RWKV_WIDTH), bonus.reshape(nseq, t, RWKV_WIDTH),
                               _state_to_pairs(s0), lnw, lnb, tb)
        fox = _fox_proj(z, qg, kg, bf, tm_prep, nseq if dim_major else None)
        return z, z3, ya.reshape(nseq * t, RWKV_WIDTH), _pairs_to_state(s_fin), fox

    zero_shift = jnp.zeros((bp, 1, Z_COLS), F32)
    zero_state = jnp.zeros((bp, RWKV_HEADS, HEAD_DIM, HEAD_DIM), F32)
    xp = x_prompt.reshape(n_p, D_MODEL)
    xs = x_sample.reshape(n_s, D_MODEL)
    zp, zp3, ya_p, st_p, (qb_p, kb_p, lf_p, kt_p, vt_p, vtb_p, lft_p) = mixers(
        xp, bp, sp, zero_shift, zero_state, 1024, 256, SCAN_BLOCK, True)
    shift_s = _pack_cols(jnp.pad(state_shift[l].reshape(bs, A_COLS), ((0, 0), (0, w_in.shape[2] - A_COLS))))
    zs, zs3, ya_s, st_s, (qb_s, kb_s, lf_s, kf_s) = mixers(
        xs, bs, ts, shift_s.reshape(bs, 1, Z_COLS), state_rwkv[l], 256, ts, ts, False)
    lf_s = lf_s[:, :FOX_HEADS]

    fp = jnp.cumsum(lft_p, axis=2) * LOG2E
    fq_p, fk_p = _prefix_terms(fp, bp, sp)
    yb_p = _fox_prompt(qb_p, kb_p, fq_p, fk_p, vtb_p.reshape(bp, FOX_HEADS, HEAD_DIM, sp), zp, bp, sp,
                       min(1024, sp))

    fs = jnp.cumsum(jnp.concatenate([cache_fox_logf[l].astype(F32), lf_s.reshape(bs, ts, FOX_HEADS)], axis=1),
                    axis=1) * LOG2E
    fn_s = fs[:, past:].transpose(0, 2, 1)
    kc_t = cache_fox_k[l].transpose(0, 2, 3, 1)
    vc_t = cache_fox_v[l].transpose(0, 2, 3, 1)
    yb_s = _fox_sample(qb_s, kc_t, vc_t, kb_s, zs, fn_s.reshape(bs, FOX_HEADS * ts, 1),
                       fs[:, :past].transpose(0, 2, 1), fn_s, bs, ts, min(512, past))

    x1_p = _out_proj(xp, ya_p, yb_p, w_out_b, 512)
    x1_s = _out_proj(xs, ya_s, yb_s, w_out_b, 256)

    kv_p = _mem_kv(mem_prompt.reshape(bp * N_MEM, D_MODEL), norm_mem[l][None], w_kv, xk_norm[l][None], 256)
    mk_p = kv_p[:, :MEM_WIDTH].reshape(bp, N_MEM, MEM_WIDTH)
    mv_p = kv_p[:, MEM_WIDTH:].reshape(bp, N_MEM, MEM_WIDTH)
    x2_p, h_p, ti_p, tg_p = _cross_router(x1_p, mk_p.astype(BF16), mv_p.astype(BF16), pc, bp, 256)
    mk_s = cache_mem_k[l].reshape(bs, N_MEM, MEM_WIDTH).astype(BF16)
    mv_s = cache_mem_v[l].reshape(bs, N_MEM, MEM_WIDTH).astype(BF16)
    x2_s, h_s, ti_s, tg_s = _cross_router(x1_s, mk_s, mv_s, pc, bs, ts)

    h_all = jnp.concatenate([h_p, h_s], axis=0)
    top_i = jnp.concatenate([ti_p[:, :TOP_K], ti_s[:, :TOP_K]], axis=0)
    gates = jnp.concatenate([tg_p[:, :TOP_K], tg_s[:, :TOP_K]], axis=0)
    row_token, pos, tile_expert, tile_valid = _route(top_i)
    nt = tile_expert.shape[0]
    bounds = [nt * c // MOE_CALLS for c in range(MOE_CALLS + 1)]
    out_rows = None
    for lo, hi in zip(bounds[:-1], bounds[1:]):
        xs_rows = h_all[row_token[lo * MOE_TM:hi * MOE_TM]]
        out_rows = _moe_experts(tile_expert[lo:hi], tile_valid[lo:hi], xs_rows, w_gu[l], b_gu[l][:, None, :],
                                w_down[l], b_down[l][:, None, :], nt * MOE_TM, lo, out_rows)
    def combine(x2, rows, gate):
        return x2 + sum(out_rows[rows[:, k]] * gate[:, k:k + 1] for k in range(TOP_K))

    y_p = combine(x2_p, pos[:n_p], gates[:n_p]).reshape(bp, sp, D_MODEL)
    y_s = combine(x2_s, pos[n_p:], gates[n_p:]).reshape(bs, ts, D_MODEL)

    to_cache = lambda a: a.reshape(1, bp, FOX_HEADS, HEAD_DIM, sp).transpose(0, 1, 4, 2, 3)
    pk, pv, plf = to_cache(kt_p), to_cache(vt_p), lft_p.transpose(0, 2, 1)[None]
    sk = kf_s.reshape(1, bs, ts, FOX_HEADS, HEAD_DIM)
    sv = zs3[..., C_FV:C_FV + FOX_WIDTH].reshape(1, bs, ts, FOX_HEADS, HEAD_DIM)
    slf = lf_s.reshape(1, bs, ts, FOX_HEADS)
    return (y_p, y_s, pk, pv, plf, st_p[None], _unpack_a_cols(zp3[:, -1:])[None],
            mk_p.reshape(1, bp, N_MEM, MEM_HEADS, MEM_HEAD_DIM), mv_p.reshape(1, bp, N_MEM, MEM_HEADS, MEM_HEAD_DIM),
            sk, sv, slf, st_s[None], _unpack_a_cols(zs3[:, -1:])[None])
```

```python
import functools

import jax
import jax.numpy as jnp
from jax import lax
from jax.experimental import pallas as pl
from jax.experimental.pallas import tpu as pltpu

F32 = jnp.float32
BF16 = jnp.bfloat16

D_MODEL = 2048
HEAD_DIM = 64
RWKV_WIDTH = 1024
FOX_WIDTH = 1024
RWKV_HEADS = 16
FOX_HEADS = 16
DECAY_LORA = 64
AAA_LORA = 64
GATE_LORA = 160
A_COLS = 3 * RWKV_WIDTH + DECAY_LORA + AAA_LORA + GATE_LORA
N_MEM = 256
MEM_HEADS = 4
MEM_HEAD_DIM = 128
MEM_WIDTH = 512
N_EXPERTS = 32
TOP_K = 4
D_FF = 2048
SWIGLU_LIMIT = 7.0
SWIGLU_ALPHA = 1.702
RMS_EPS = 1e-6
GN_EPS = 64e-5

LANES = 128
HEAD_PAIRS = 8

C_R, C_K, C_V = 0, 1024, 2048
C_FQ, C_FK, C_FV, C_FG = 3072, 4096, 5120, 6144
C_GL = 7168
C_WA = 7424
C_FL = 7552
Z_COLS = 7680

VMEM_LIMIT = 56 * 1024 * 1024
NEG_BIG = -1e30
LOG2E = 1.4426950408889634


def _cparams(sem):
    return pltpu.CompilerParams(dimension_semantics=sem, vmem_limit_bytes=VMEM_LIMIT)


def _block_ones():
    r = lax.broadcasted_iota(jnp.int32, (LANES, LANES), 0) // HEAD_DIM
    c = lax.broadcasted_iota(jnp.int32, (LANES, LANES), 1) // HEAD_DIM
    return (r == c).astype(BF16)


def _seg_sum64(x):
    ones = _block_ones()
    outs = []
    for s in range(x.shape[-1] // LANES):
        xs = x[:, s * LANES:(s + 1) * LANES]
        hi = xs.astype(BF16)
        lo = (xs - hi.astype(F32)).astype(BF16)
        outs.append(jnp.dot(hi, ones, preferred_element_type=F32)
                    + jnp.dot(lo, ones, preferred_element_type=F32))
    return outs[0] if len(outs) == 1 else jnp.concatenate(outs, axis=-1)


def _softplus(x):
    return jnp.maximum(x, 0.0) + jnp.log(1.0 + jnp.exp(-jnp.abs(x)))


def _sigmoid(x):
    return 1.0 / (1.0 + jnp.exp(-x))


def _norm_mm_kernel(x_ref, g_ref, w_ref, o_ref, xn_ref):
    @pl.when(pl.program_id(1) == 0)
    def _():
        x = x_ref[...]
        ms = jnp.mean(x * x, axis=-1, keepdims=True)
        xn_ref[...] = (x * lax.rsqrt(ms + RMS_EPS) * g_ref[...]).astype(BF16)

    o_ref[...] = jnp.dot(xn_ref[...], w_ref[...], preferred_element_type=F32)


def _norm_mm(x, g, w, tm, tn):
    m, k = x.shape
    n = w.shape[1]
    return pl.pallas_call(
        _norm_mm_kernel,
        grid=(m // tm, n // tn),
        in_specs=[pl.BlockSpec((tm, k), lambda i, j: (i, 0)),
                  pl.BlockSpec((1, k), lambda i, j: (0, 0)),
                  pl.BlockSpec((k, tn), lambda i, j: (0, j))],
        out_specs=pl.BlockSpec((tm, tn), lambda i, j: (i, j)),
        out_shape=jax.ShapeDtypeStruct((m, n), F32),
        scratch_shapes=[pltpu.VMEM((tm, k), BF16)],
        compiler_params=_cparams(("parallel", "arbitrary")),
        name="norm_mm",
    )(x, g, w)


def _rwkv_prep_kernel(zm_ref, zg_ref, zw_ref, pm_ref, pg_ref, pw_ref,
                      mum_ref, mug_ref, muw_ref, w0_ref, w2_ref, a0_ref, a2_ref, g2_ref,
                      kk_ref, ka_ref, rk_ref,
                      r_o, d_o, k_o, v_o, kk_o, b_o, g_o, bonus_o):
    def shifted(z_ref, p_ref, mu_ref):
        z = z_ref[...]
        rolled = pltpu.roll(z, 1, 0)
        row = lax.broadcasted_iota(jnp.int32, z.shape, 0)
        prev = jnp.where(row == 0, p_ref[0], rolled)
        return z + mu_ref[...] * (prev - z)

    zm = shifted(zm_ref, pm_ref, mum_ref)
    zg = shifted(zg_ref, pg_ref, mug_ref)
    zw = shifted(zw_ref, pw_ref, muw_ref)
    r = zm[:, C_R:C_R + RWKV_WIDTH]
    k = zm[:, C_K:C_K + RWKV_WIDTH]
    v = zm[:, C_V:C_V + RWKV_WIDTH]
    lw = jnp.dot(jnp.tanh(zw).astype(BF16), w2_ref[...], preferred_element_type=F32)
    la = jnp.dot(zw.astype(BF16), a2_ref[...], preferred_element_type=F32)
    g = jnp.dot(_sigmoid(zg).astype(BF16), g2_ref[...], preferred_element_type=F32)
    w_log = -_softplus(-(w0_ref[...] + lw)) - 0.5
    a = _sigmoid(a0_ref[...] + la)
    kk = k * kk_ref[...]
    kk = kk * lax.rsqrt(jnp.maximum(_seg_sum64(kk * kk), 1e-24))
    kh = k * (1.0 + (a - 1.0) * ka_ref[...])
    tiles = lambda x: x.reshape(x.shape[0], HEAD_PAIRS, LANES)
    r_o[...] = tiles(r)
    d_o[...] = tiles(jnp.exp(-jnp.exp(w_log)))
    k_o[...] = tiles(kh)
    v_o[...] = tiles(v)
    kk_o[...] = tiles(kk)
    b_o[...] = tiles(kk * a)
    g_o[...] = g
    bonus_o[...] = _seg_sum64(r * kh * rk_ref[...]) * v


def _rwkv_prep(z, prev_m, prev_g, prev_w, pr, tm):
    n = z.shape[0]
    row = lambda w, c: pl.BlockSpec((tm, w), lambda i, c=c: (i, c))
    prev = lambda w: pl.BlockSpec((1, 1, w), lambda i: (i, 0, 0))
    full = lambda a: pl.BlockSpec(a.shape, lambda i: (0,) * a.ndim)
    params = [pr["mu_m"], pr["mu_g"], pr["mu_w"], pr["w0"], pr["w2"], pr["a0"], pr["a2"], pr["g2"],
              pr["k_k"], pr["k_a"], pr["r_k"]]
    tiled = jax.ShapeDtypeStruct((n, HEAD_PAIRS, LANES), F32)
    flat = jax.ShapeDtypeStruct((n, RWKV_WIDTH), F32)
    return pl.pallas_call(
        _rwkv_prep_kernel,
        grid=(n // tm,),
        in_specs=[row(3072, 0), row(256, C_GL // 256), row(128, C_WA // 128),
                  prev(3072), prev(256), prev(128)] + [full(a) for a in params],
        out_specs=[pl.BlockSpec((tm, HEAD_PAIRS, LANES), lambda i: (i, 0, 0))] * 6
        + [pl.BlockSpec((tm, RWKV_WIDTH), lambda i: (i, 0))] * 2,
        out_shape=[tiled] * 6 + [flat] * 2,
        compiler_params=_cparams(("parallel",)),
        name="rwkv_prep",
    )(z, z, z, prev_m, prev_g, prev_w, *params)


SEQ_PER_STEP = 2
SCAN_BLOCK = 128


def _rwkv_scan_kernel(r_ref, d_ref, k_ref, v_ref, kk_ref, b_ref, g_ref, bonus_ref, s0_ref,
                      lw_ref, lb_ref,
                      y_ref, sf_ref,
                      s_ref, oa_ref, *, tb):
    tblk = pl.program_id(1)

    @pl.when(tblk == 0)
    def _():
        s_ref[...] = s0_ref[...]

    oa_ref[...] = jnp.zeros_like(oa_ref)
    r2 = lax.broadcasted_iota(jnp.int32, (2 * LANES, 2 * LANES), 0) // HEAD_DIM
    c2 = lax.broadcasted_iota(jnp.int32, (2 * LANES, 2 * LANES), 1) // HEAD_DIM
    ones = (r2 == c2).astype(BF16)
    lane = lax.broadcasted_iota(jnp.int32, (HEAD_DIM, LANES), 1)
    sub = lax.broadcasted_iota(jnp.int32, (HEAD_DIM, LANES), 0)
    diag = (lane % HEAD_DIM) == sub
    tiles = [(s, hp) for s in range(SEQ_PER_STEP) for hp in range(HEAD_PAIRS)]

    def bcast(ref, s, t, hp):
        return jnp.broadcast_to(ref[s, t, hp:hp + 1, :], (HEAD_DIM, LANES))

    def seg_sums(parts):
        lhs = jnp.concatenate([jnp.concatenate(parts[j:j + 2], axis=1) for j in range(0, len(parts), 2)], axis=0)
        out = jnp.dot(lhs, ones, preferred_element_type=F32)
        return [out[(j // 2) * HEAD_DIM:(j // 2 + 1) * HEAD_DIM, (j % 2) * LANES:(j % 2 + 1) * LANES]
                for j in range(len(parts))]

    def emit_outputs(t_out, valid):
        half = pl.multiple_of((t_out // HEAD_DIM) * HEAD_DIM, HEAD_DIM)
        hit = jnp.logical_and((lane % HEAD_DIM) == (t_out % HEAD_DIM), valid)
        ob = seg_sums([(s_ref[s, hp] * bcast(r_ref, s, t_out, hp)).astype(BF16) for s, hp in tiles])
        for j, (s, hp) in enumerate(tiles):
            cur = oa_ref[s, hp, pl.ds(half, HEAD_DIM), :]
            oa_ref[s, hp, pl.ds(half, HEAD_DIM), :] = jnp.where(hit, ob[j], cur)

    def step(t, carry):
        emit_outputs(jnp.maximum(t - 1, 0), t > 0)
        skk = seg_sums([(s_ref[s, hp] * bcast(kk_ref, s, t, hp)).astype(BF16) for s, hp in tiles])
        vb = seg_sums([jnp.where(diag, bcast(v_ref, s, t, hp), 0.0).astype(BF16) for s, hp in tiles])
        for j, (s, hp) in enumerate(tiles):
            s_ref[s, hp] = (s_ref[s, hp] * bcast(d_ref, s, t, hp)
                            - skk[j] * bcast(b_ref, s, t, hp)
                            + vb[j] * bcast(k_ref, s, t, hp))
        return carry

    lax.fori_loop(0, tb, step, 0)
    emit_outputs(tb - 1, True)

    lane_t = lax.broadcasted_iota(jnp.int32, (HEAD_DIM, LANES), 1)
    low = lane_t < HEAD_DIM
    for s in range(SEQ_PER_STEP):
        slabs = []
        for hp in range(HEAD_PAIRS):
            mt = oa_ref[s, hp].T
            ro = pltpu.roll(mt, HEAD_DIM, 1)
            top = jnp.where(low, mt[:HEAD_DIM], ro[HEAD_DIM:])
            bot = jnp.where(low, ro[:HEAD_DIM], mt[HEAD_DIM:])
            slabs.append(jnp.concatenate([top, bot], axis=0)[:tb])
        o = jnp.concatenate(slabs, axis=-1)
        mean = _seg_sum64(o) * (1.0 / HEAD_DIM)
        cen = o - mean
        var = _seg_sum64(cen * cen) * (1.0 / HEAD_DIM)
        on = cen * lax.rsqrt(var + GN_EPS) * lw_ref[...] + lb_ref[...]
        y_ref[s] = ((on + bonus_ref[s]) * g_ref[s]).astype(y_ref.dtype)

    @pl.when(tblk == pl.num_programs(1) - 1)
    def _():
        sf_ref[...] = s_ref[...]


def _rwkv_scan(steps, g, bonus, s0, lnw, lnb, tb):
    nseq, t = g.shape[:2]
    step_spec = pl.BlockSpec((SEQ_PER_STEP, tb, HEAD_PAIRS, LANES), lambda i, j: (i, j, 0, 0))
    seq_spec = pl.BlockSpec((SEQ_PER_STEP, tb, RWKV_WIDTH), lambda i, j: (i, j, 0))
    st_spec = pl.BlockSpec((SEQ_PER_STEP, HEAD_PAIRS, HEAD_DIM, LANES), lambda i, j: (i, 0, 0, 0))
    par_spec = pl.BlockSpec((1, RWKV_WIDTH), lambda i, j: (0, 0))
    return pl.pallas_call(
        functools.partial(_rwkv_scan_kernel, tb=tb),
        grid=(nseq // SEQ_PER_STEP, t // tb),
        in_specs=[step_spec] * 6 + [seq_spec] * 2 + [st_spec] + [par_spec] * 2,
        out_specs=[seq_spec, st_spec],
        out_shape=[jax.ShapeDtypeStruct((nseq, t, RWKV_WIDTH), BF16),
                   jax.ShapeDtypeStruct(s0.shape, F32)],
        scratch_shapes=[pltpu.VMEM((SEQ_PER_STEP, HEAD_PAIRS, HEAD_DIM, LANES), F32),
                        pltpu.VMEM((SEQ_PER_STEP, HEAD_PAIRS, LANES, LANES), F32)],
        compiler_params=_cparams(("parallel", "arbitrary")),
        name="rwkv_scan",
    )(*steps, g, bonus, s0, lnw, lnb)


def _state_to_pairs(s):
    b = s.shape[0]
    return s.reshape(b, HEAD_PAIRS, 2, HEAD_DIM, HEAD_DIM).transpose(0, 1, 3, 2, 4).reshape(
        b, HEAD_PAIRS, HEAD_DIM, LANES)


def _pairs_to_state(s):
    b = s.shape[0]
    return s.reshape(b, HEAD_PAIRS, HEAD_DIM, 2, HEAD_DIM).transpose(0, 1, 3, 2, 4).reshape(
        b, RWKV_HEADS, HEAD_DIM, HEAD_DIM)


def _fox_proj_kernel(q_ref, k_ref, v_ref, fl_ref, qg_ref, kg_ref, bf_ref, qb_o, kb_o, lf_o, *cache_o, dim_major):
    def headnorm(x, g):
        ms = _seg_sum64(x * x) * (1.0 / HEAD_DIM)
        return x * lax.rsqrt(ms + RMS_EPS) * g

    q = headnorm(q_ref[...], qg_ref[...])
    k = headnorm(k_ref[...], kg_ref[...])
    q = q * (HEAD_DIM ** -0.5 * LOG2E)
    lf = -_softplus(-(fl_ref[...] + bf_ref[...]))
    lf_o[...] = lf
    if dim_major:
        lane = lax.broadcasted_iota(jnp.int32, (q.shape[0], LANES), 1)
        for h in range(FOX_HEADS):
            slab = slice((h // 2) * LANES, (h // 2 + 1) * LANES)
            for x, o in ((q, qb_o), (k, kb_o)):
                xs = x[:, slab] if h % 2 == 0 else pltpu.roll(x[:, slab], HEAD_DIM, 1)
                o[0, h] = jnp.where(lane < HEAD_DIM, xs, 0.0).astype(BF16)
        kt_o, vt_o, vtb_o, lft_o = cache_o
        vt = v_ref[...].T
        kt_o[0] = k.T
        vt_o[0] = vt
        vtb_o[0] = vt.astype(BF16)
        lft_o[0] = lf.T[:FOX_HEADS]
    else:
        qb_o[...] = q.astype(BF16)
        kb_o[...] = k.astype(BF16)
        cache_o[0][...] = k


def _fox_proj(z, qg, kg, bf, tm, nseq=None):
    n = z.shape[0]
    col = lambda w, c: pl.BlockSpec((tm, w), lambda i, c=c: (i, c))
    par = lambda w: pl.BlockSpec((1, w), lambda i: (0, 0))
    wide = pl.BlockSpec((tm, FOX_WIDTH), lambda i: (i, 0))
    lf_spec = pl.BlockSpec((tm, LANES), lambda i: (i, 0))
    lf_shape = jax.ShapeDtypeStruct((n, LANES), F32)
    if nseq is None:
        out_specs = [wide, wide, lf_spec, wide]
        out_shape = [jax.ShapeDtypeStruct((n, FOX_WIDTH), BF16), jax.ShapeDtypeStruct((n, FOX_WIDTH), BF16),
                     lf_shape, jax.ShapeDtypeStruct((n, FOX_WIDTH), F32)]
    else:
        t = n // nseq
        per = t // tm
        heads = pl.BlockSpec((1, FOX_HEADS, tm, LANES), lambda i: (i // per, 0, i % per, 0))
        tall = pl.BlockSpec((1, FOX_WIDTH, tm), lambda i: (i // per, 0, i % per))
        padded = jax.ShapeDtypeStruct((nseq, FOX_HEADS, t, LANES), BF16)
        out_specs = [heads, heads, lf_spec, tall, tall, tall,
                     pl.BlockSpec((1, FOX_HEADS, tm), lambda i: (i // per, 0, i % per))]
        out_shape = [padded, padded, lf_shape,
                     jax.ShapeDtypeStruct((nseq, FOX_WIDTH, t), F32), jax.ShapeDtypeStruct((nseq, FOX_WIDTH, t), F32),
                     jax.ShapeDtypeStruct((nseq, FOX_WIDTH, t), BF16),
                     jax.ShapeDtypeStruct((nseq, FOX_HEADS, t), F32)]
    return pl.pallas_call(
        functools.partial(_fox_proj_kernel, dim_major=nseq is not None),
        grid=(n // tm,),
        in_specs=[col(1024, C_FQ // 1024), col(1024, C_FK // 1024), col(1024, C_FV // 1024), col(128, C_FL // 128),
                  par(1024), par(1024), par(128)],
        out_specs=out_specs,
        out_shape=out_shape,
        compiler_params=_cparams(("parallel",)),
        name="fox_proj",
    )(z, z, z, z, qg, kg, bf)


F_TERMS = 8


def _fox_prompt_kernel(qi_ref, kj_ref, q_ref, k_ref, fq_ref, fk_ref, vt_ref, gate_ref, y_ref,
                       qa_ref, m_ref, l_ref, acc_ref, *, tq):
    pair = pl.program_id(2)
    qi = qi_ref[pair]
    kj = kj_ref[pair]
    lane = lax.broadcasted_iota(jnp.int32, (tq, LANES), 1)
    zero = jnp.zeros((tq, LANES), BF16)

    def with_terms(x, f, h):
        lo = HEAD_DIM + F_TERMS * h
        return x + jnp.where(jnp.logical_and(lane >= lo, lane < lo + F_TERMS), f, zero)

    @pl.when(kj == 0)
    def _():
        m_ref[...] = jnp.full_like(m_ref, NEG_BIG)
        l_ref[...] = jnp.zeros_like(l_ref)
        acc_ref[...] = jnp.zeros_like(acc_ref)
        for h in range(2):
            qa_ref[h] = with_terms(q_ref[0, h], fq_ref[0, 0], h)

    def body(masked):
        for h in range(2):
            st = lax.dot_general(with_terms(k_ref[0, h], fk_ref[0, 0], h), qa_ref[h], (((1,), (1,)), ((), ())),
                                 preferred_element_type=F32)
            if masked:
                key = lax.broadcasted_iota(jnp.int32, st.shape, 0)
                qry = lax.broadcasted_iota(jnp.int32, st.shape, 1)
                st = jnp.where(key <= qry, st, NEG_BIG)
            m_prev = m_ref[h]
            m_new = jnp.maximum(m_prev, jnp.max(st, axis=0, keepdims=True))
            alpha = jnp.exp2(m_prev - m_new)
            p = jnp.exp2(st - m_new)
            l_ref[h] = alpha * l_ref[h] + jnp.sum(p, axis=0, keepdims=True)
            acc_ref[h] = alpha * acc_ref[h] + jnp.dot(vt_ref[0, h], p.astype(BF16), preferred_element_type=F32)
            m_ref[h] = m_new

    @pl.when(kj < qi)
    def _():
        body(False)

    @pl.when(kj == qi)
    def _():
        body(True)
        ot = jnp.concatenate([acc_ref[0] / l_ref[0], acc_ref[1] / l_ref[1]], axis=0)
        y_ref[...] = (ot.T * _sigmoid(gate_ref[...])).astype(y_ref.dtype)


def _fox_prompt(q_pad, k_pad, fq, fk, vt, z, nb, seq, tq):
    nq = seq // tq
    pairs = [(i, j) for i in range(nq) for j in range(i + 1)]
    qi_tab = jnp.asarray([p[0] for p in pairs], jnp.int32)
    kj_tab = jnp.asarray([p[1] for p in pairs], jnp.int32)
    grid_spec = pltpu.PrefetchScalarGridSpec(
        num_scalar_prefetch=2,
        grid=(nb, HEAD_PAIRS, len(pairs)),
        in_specs=[pl.BlockSpec((1, 2, tq, LANES), lambda b, hp, p, qi, kj: (b, hp, qi[p], 0)),
                  pl.BlockSpec((1, 2, tq, LANES), lambda b, hp, p, qi, kj: (b, hp, kj[p], 0)),
                  pl.BlockSpec((1, 1, tq, LANES), lambda b, hp, p, qi, kj: (b, hp, qi[p], 0)),
                  pl.BlockSpec((1, 1, tq, LANES), lambda b, hp, p, qi, kj: (b, hp, kj[p], 0)),
                  pl.BlockSpec((1, 2, HEAD_DIM, tq), lambda b, hp, p, qi, kj: (b, hp, 0, kj[p])),
                  pl.BlockSpec((tq, LANES), lambda b, hp, p, qi, kj: (b * nq + qi[p], C_FG // LANES + hp))],
        out_specs=pl.BlockSpec((tq, LANES), lambda b, hp, p, qi, kj: (b * nq + qi[p], hp)),
        scratch_shapes=[pltpu.VMEM((2, tq, LANES), BF16), pltpu.VMEM((2, 1, tq), F32), pltpu.VMEM((2, 1, tq), F32),
                        pltpu.VMEM((2, HEAD_DIM, tq), F32)],
    )
    return pl.pallas_call(
        functools.partial(_fox_prompt_kernel, tq=tq),
        grid_spec=grid_spec,
        out_shape=jax.ShapeDtypeStruct((nb * seq, FOX_WIDTH), BF16),
        compiler_params=_cparams(("parallel", "parallel", "arbitrary")),
        name="fox_prompt",
    )(qi_tab, kj_tab, q_pad, k_pad, fq, fk, vt, z)


def _split3(x):
    def top(v):
        bits = lax.bitcast_convert_type(v, jnp.uint32) & jnp.uint32(0xFFFF0000)
        return lax.bitcast_convert_type(bits, F32)

    hi = top(x)
    mid = top(x - hi)
    lo = top(x - hi - mid)
    return hi.astype(BF16), mid.astype(BF16), lo.astype(BF16)


def _prefix_terms(f2, nb, seq):
    one = jnp.ones_like(f2, BF16)
    none = jnp.zeros_like(f2, BF16)

    def lanes(terms):
        t = jnp.stack([*terms, none, none], axis=-1)
        t = t.reshape(nb, HEAD_PAIRS, 2, seq, F_TERMS).transpose(0, 1, 3, 2, 4).reshape(nb, HEAD_PAIRS, seq, 16)
        return jnp.pad(t, ((0, 0), (0, 0), (0, 0), (HEAD_DIM, LANES - HEAD_DIM - 16)))

    return lanes([*_split3(f2), one, one, one]), lanes([one, one, one, *_split3(-f2)])


def _fox_sample_kernel(q_ref, kc_ref, vc_ref, kn_ref, vn_ref, fq_ref, fkc_ref, fkn_ref, gate_ref, y_ref,
                       m_ref, l_ref, acc_ref, *, nk):
    kj = pl.program_id(1)

    @pl.when(kj == 0)
    def _():
        m_ref[...] = jnp.full_like(m_ref, NEG_BIG)
        l_ref[...] = jnp.zeros_like(l_ref)
        acc_ref[...] = jnp.zeros_like(acc_ref)

    def head_slice(ref, h):
        return ref[:, h * HEAD_DIM:(h + 1) * HEAD_DIM]

    t = q_ref.shape[0]

    nt_dims = (((1,), (1,)), ((), ()))

    def attend(ks, vs, fk_ref, masked, dim_major):
        if dim_major:
            qk = lambda h: jnp.dot(head_slice(q_ref, h), ks[h], preferred_element_type=F32)
            pv_of = lambda ph, h: lax.dot_general(ph, vs[h], nt_dims, preferred_element_type=F32)
        else:
            qk = lambda h: lax.dot_general(head_slice(q_ref, h), ks[h], nt_dims, preferred_element_type=F32)
            pv_of = lambda ph, h: jnp.dot(ph, vs[h], preferred_element_type=F32)
        s = jnp.concatenate([qk(h) for h in range(FOX_HEADS)], axis=0)
        fk = jnp.concatenate([jnp.broadcast_to(fk_ref[0, h:h + 1, :], (t, s.shape[1])) for h in range(FOX_HEADS)],
                             axis=0)
        s = s + fq_ref[0] - fk
        if masked:
            qry = lax.broadcasted_iota(jnp.int32, s.shape, 0) % t
            key = lax.broadcasted_iota(jnp.int32, s.shape, 1)
            s = jnp.where(key <= qry, s, NEG_BIG)
        m_prev = m_ref[...]
        m_new = jnp.maximum(m_prev, jnp.max(s, axis=-1, keepdims=True))
        alpha = jnp.exp2(m_prev - m_new)
        p = jnp.exp2(s - m_new)
        l_ref[...] = alpha * l_ref[...] + jnp.sum(p, axis=-1, keepdims=True)
        p = p.astype(BF16)
        pv = jnp.concatenate([pv_of(p[h * t:(h + 1) * t], h) for h in range(FOX_HEADS)], axis=0)
        acc_ref[...] = alpha * acc_ref[...] + pv
        m_ref[...] = m_new

    @pl.when(kj < nk)
    def _():
        attend([kc_ref[0, h].astype(BF16) for h in range(FOX_HEADS)],
               [vc_ref[0, h].astype(BF16) for h in range(FOX_HEADS)], fkc_ref, False, True)

    @pl.when(kj == nk)
    def _():
        attend([head_slice(kn_ref, h) for h in range(FOX_HEADS)],
               [head_slice(vn_ref, h).astype(BF16) for h in range(FOX_HEADS)], fkn_ref, True, False)
        o = acc_ref[...] / l_ref[...]
        for h in range(FOX_HEADS):
            gate = _sigmoid(head_slice(gate_ref, h))
            y_ref[:, h * HEAD_DIM:(h + 1) * HEAD_DIM] = (o[h * t:(h + 1) * t] * gate).astype(y_ref.dtype)


def _fox_sample(qb, kc, vc, kb, z, fq, fkc, fkn, nb, t, tk):
    past = kc.shape[3]
    nk = past // tk
    last = nk - 1
    wide = lambda c: pl.BlockSpec((t, FOX_WIDTH), lambda b, j, c=c: (b, c))
    cache = pl.BlockSpec((1, FOX_HEADS, HEAD_DIM, tk), lambda b, j: (b, 0, 0, jnp.minimum(j, last)))
    return pl.pallas_call(
        functools.partial(_fox_sample_kernel, nk=nk),
        grid=(nb, nk + 1),
        in_specs=[wide(0), cache, cache, wide(0), wide(C_FV // FOX_WIDTH),
                  pl.BlockSpec((1, FOX_HEADS * t, 1), lambda b, j: (b, 0, 0)),
                  pl.BlockSpec((1, FOX_HEADS, tk), lambda b, j: (b, 0, jnp.minimum(j, last))),
                  pl.BlockSpec((1, FOX_HEADS, t), lambda b, j: (b, 0, 0)),
                  wide(C_FG // FOX_WIDTH)],
        out_specs=wide(0),
        out_shape=jax.ShapeDtypeStruct((nb * t, FOX_WIDTH), BF16),
        scratch_shapes=[pltpu.VMEM((FOX_HEADS * t, 1), F32), pltpu.VMEM((FOX_HEADS * t, 1), F32),
                        pltpu.VMEM((FOX_HEADS * t, HEAD_DIM), F32)],
        compiler_params=_cparams(("parallel", "arbitrary")),
        name="fox_sample",
    )(qb, kc, vc, kb, z, fq, fkc, fkn, z)


def _out_proj_kernel(x_ref, ya_ref, yb_ref, wa_ref, wb_ref, o_ref):
    o_ref[...] = (x_ref[...]
                  + jnp.dot(ya_ref[...], wa_ref[...], preferred_element_type=F32)
                  + jnp.dot(yb_ref[...], wb_ref[...], preferred_element_type=F32))


def _out_proj(x, ya, yb, w, tm):
    n = x.shape[0]
    return pl.pallas_call(
        _out_proj_kernel,
        grid=(n // tm,),
        in_specs=[pl.BlockSpec((tm, D_MODEL), lambda i: (i, 0)),
                  pl.BlockSpec((tm, RWKV_WIDTH), lambda i: (i, 0)),
                  pl.BlockSpec((tm, FOX_WIDTH), lambda i: (i, 0)),
                  pl.BlockSpec((RWKV_WIDTH, D_MODEL), lambda i: (0, 0)),
                  pl.BlockSpec((FOX_WIDTH, D_MODEL), lambda i: (1, 0))],
        out_specs=pl.BlockSpec((tm, D_MODEL), lambda i: (i, 0)),
        out_shape=jax.ShapeDtypeStruct((n, D_MODEL), F32),
        compiler_params=_cparams(("parallel",)),
        name="out_proj",
    )(x, ya, yb, w, w)


def _mem_kv_kernel(x_ref, g_ref, w_ref, kg_ref, o_ref):
    x = x_ref[...]
    ms = jnp.mean(x * x, axis=-1, keepdims=True)
    xn = (x * lax.rsqrt(ms + RMS_EPS) * g_ref[...]).astype(BF16)
    kv = jnp.dot(xn, w_ref[...], preferred_element_type=F32)
    for h in range(MEM_HEADS):
        kh = kv[:, h * MEM_HEAD_DIM:(h + 1) * MEM_HEAD_DIM]
        ms = jnp.mean(kh * kh, axis=-1, keepdims=True)
        o_ref[:, h * MEM_HEAD_DIM:(h + 1) * MEM_HEAD_DIM] = kh * lax.rsqrt(ms + RMS_EPS) * kg_ref[...]
    o_ref[:, MEM_WIDTH:] = kv[:, MEM_WIDTH:]


def _mem_kv(mem, g, w_kv, kg, tm):
    n = mem.shape[0]
    return pl.pallas_call(
        _mem_kv_kernel,
        grid=(n // tm,),
        in_specs=[pl.BlockSpec((tm, D_MODEL), lambda i: (i, 0)),
                  pl.BlockSpec((1, D_MODEL), lambda i: (0, 0)),
                  pl.BlockSpec((D_MODEL, 2 * MEM_WIDTH), lambda i: (0, 0)),
                  pl.BlockSpec((1, MEM_HEAD_DIM), lambda i: (0, 0))],
        out_specs=pl.BlockSpec((tm, 2 * MEM_WIDTH), lambda i: (i, 0)),
        out_shape=jax.ShapeDtypeStruct((n, 2 * MEM_WIDTH), F32),
        compiler_params=_cparams(("parallel",)),
        name="mem_kv",
    )(mem, g, w_kv, kg)


def _cross_router_kernel(x_ref, gc_ref, wq_ref, qg_ref, mk_ref, mv_ref, wo_ref,
                         gf_ref, wr_ref, br_ref,
                         x2_ref, h_ref, idx_ref, gate_ref):
    x = x_ref[...]
    ms = jnp.mean(x * x, axis=-1, keepdims=True)
    xn = (x * lax.rsqrt(ms + RMS_EPS) * gc_ref[...]).astype(BF16)
    q = jnp.dot(xn, wq_ref[...], preferred_element_type=F32)
    outs = []
    for h in range(MEM_HEADS):
        sl = slice(h * MEM_HEAD_DIM, (h + 1) * MEM_HEAD_DIM)
        qh = q[:, sl]
        qms = jnp.mean(qh * qh, axis=-1, keepdims=True)
        qh = (qh * lax.rsqrt(qms + RMS_EPS) * qg_ref[...]).astype(BF16)
        s = lax.dot_general(qh, mk_ref[0, :, sl], (((1,), (1,)), ((), ())), preferred_element_type=F32)
        s = s * (MEM_HEAD_DIM ** -0.5)
        p = jnp.exp(s - jnp.max(s, axis=-1, keepdims=True))
        p = p / jnp.sum(p, axis=-1, keepdims=True)
        outs.append(jnp.dot(p.astype(BF16), mv_ref[0, :, sl], preferred_element_type=F32))
    o = jnp.concatenate(outs, axis=-1).astype(BF16)
    x2 = x + jnp.dot(o, wo_ref[...], preferred_element_type=F32)
    x2_ref[...] = x2

    ms2 = jnp.mean(x2 * x2, axis=-1, keepdims=True)
    hn = x2 * lax.rsqrt(ms2 + RMS_EPS) * gf_ref[...]
    h_ref[...] = hn.astype(BF16)
    logits = jnp.dot(hn, wr_ref[...], preferred_element_type=F32, precision=lax.Precision.HIGHEST)
    logits = logits + br_ref[...]
    lane = lax.broadcasted_iota(jnp.int32, logits.shape, 1)
    idx_acc = jnp.zeros(logits.shape, jnp.int32)
    val_acc = jnp.zeros(logits.shape, F32)
    top = None
    for kk in range(TOP_K):
        m = jnp.max(logits, axis=-1, keepdims=True)
        sel = jnp.min(jnp.where(logits == m, lane, LANES), axis=-1, keepdims=True)
        if kk == 0:
            top = m
        idx_acc = jnp.where(lane == kk, sel, idx_acc)
        val_acc = jnp.where(lane == kk, jnp.exp(m - top), val_acc)
        logits = jnp.where(lane == sel, -jnp.inf, logits)
    idx_ref[...] = idx_acc
    gate_ref[...] = val_acc / jnp.sum(val_acc, axis=-1, keepdims=True)


def _cross_router(x, mk, mv, pc, nb, tm):
    n = x.shape[0]
    per = n // nb // tm
    full = lambda a: pl.BlockSpec(a.shape, lambda i: (0,) * a.ndim)
    mem = pl.BlockSpec((1, N_MEM, MEM_WIDTH), lambda i: (i // per, 0, 0))
    row = lambda w: pl.BlockSpec((tm, w), lambda i: (i, 0))
    params1 = [pc["g_cross"], pc["w_xq"], pc["xq_gain"]]
    params2 = [pc["w_xo"], pc["g_ffn"], pc["w_router"], pc["b_router"]]
    return pl.pallas_call(
        _cross_router_kernel,
        grid=(n // tm,),
        in_specs=[row(D_MODEL)] + [full(a) for a in params1] + [mem, mem] + [full(a) for a in params2],
        out_specs=[row(D_MODEL), row(D_MODEL), row(LANES), row(LANES)],
        out_shape=[jax.ShapeDtypeStruct((n, D_MODEL), F32), jax.ShapeDtypeStruct((n, D_MODEL), BF16),
                   jax.ShapeDtypeStruct((n, LANES), jnp.int32), jax.ShapeDtypeStruct((n, LANES), F32)],
        compiler_params=_cparams(("parallel",)),
        name="cross_router",
    )(x, *params1, mk, mv, *params2)


MOE_TM = 1152
MOE_TF = 256
MOE_TN = 512
MOE_NG = D_FF // (2 * MOE_TF)
MOE_ND = D_MODEL // MOE_TN


def _moe_kernel(te_ref, tv_ref, xs_ref, wgu_ref, bgu_ref, wd_ref, bd_ref, o_ref, act_ref):
    s = pl.program_id(1)
    used = tv_ref[pl.program_id(0)] > 0
    half = 2 * MOE_TF

    @pl.when(jnp.logical_and(used, s < MOE_NG))
    def _():
        gu = jnp.dot(xs_ref[...], wgu_ref[0].astype(BF16), preferred_element_type=F32) + bgu_ref[0]
        glu = jnp.minimum(gu, SWIGLU_LIMIT)
        fglu = glu * _sigmoid(SWIGLU_ALPHA * glu)
        lin = jnp.clip(gu, -SWIGLU_LIMIT, SWIGLU_LIMIT) + 1.0
        act_a = pltpu.roll(fglu[:, :half], 1, 1) * lin[:, :half]
        act_b = fglu[:, half:] * pltpu.roll(lin[:, half:], half - 1, 1)
        lane = lax.broadcasted_iota(jnp.int32, act_a.shape, 1)
        merged = jnp.where(lane % 2 == 0, act_b, act_a).astype(BF16)
        for c in range(MOE_NG):
            @pl.when(s == c)
            def _(c=c):
                act_ref[:, c * half:(c + 1) * half] = merged

    @pl.when(jnp.logical_and(used, s >= MOE_NG))
    def _():
        top = lambda w: lax.bitcast_convert_type(w.astype(BF16).astype(F32), jnp.uint32)
        chunks = []
        for c in range(MOE_NG):
            a = wd_ref[0, c * half:c * half + MOE_TF, :]
            b = wd_ref[0, c * half + MOE_TF:(c + 1) * half, :]
            chunks.append(pltpu.bitcast((top(b) >> 16) | top(a), BF16))
        wd = jnp.concatenate(chunks, axis=0)
        o_ref[...] = jnp.dot(act_ref[...], wd, preferred_element_type=F32) + bd_ref[0]

    @pl.when(jnp.logical_and(jnp.logical_not(used), s >= MOE_NG))
    def _():
        o_ref[...] = jnp.zeros_like(o_ref)


def _moe_experts(tile_expert, tile_valid, xs, w_gu, b_gu, w_down, b_down):
    p = xs.shape[0]
    nt = p // MOE_TM
    gi = lambda i, s, tv: jnp.where(tv[i] > 0, jnp.minimum(s, MOE_NG - 1), MOE_NG - 1)
    di = lambda i, s, tv: jnp.where(tv[i] > 0, jnp.maximum(s - MOE_NG, 0), MOE_ND - 1)
    in_specs = [pl.BlockSpec((MOE_TM, D_MODEL), lambda i, s, te, tv: (i, 0)),
                pl.BlockSpec((1, D_MODEL, 4 * MOE_TF), lambda i, s, te, tv: (te[i], 0, gi(i, s, tv))),
                pl.BlockSpec((1, 1, 4 * MOE_TF), lambda i, s, te, tv: (te[i], 0, gi(i, s, tv))),
                pl.BlockSpec((1, D_FF, MOE_TN), lambda i, s, te, tv: (te[i], 0, di(i, s, tv))),
                pl.BlockSpec((1, 1, MOE_TN), lambda i, s, te, tv: (te[i], 0, di(i, s, tv)))]
    grid_spec = pltpu.PrefetchScalarGridSpec(
        num_scalar_prefetch=2,
        grid=(nt, MOE_NG + MOE_ND),
        in_specs=in_specs,
        out_specs=pl.BlockSpec((MOE_TM, MOE_TN), lambda i, s, te, tv: (i, jnp.maximum(s - MOE_NG, 0))),
        scratch_shapes=[pltpu.VMEM((MOE_TM, D_FF), BF16)],
    )
    return pl.pallas_call(
        _moe_kernel,
        grid_spec=grid_spec,
        out_shape=jax.ShapeDtypeStruct((p, D_MODEL), F32),
        compiler_params=_cparams(("arbitrary", "arbitrary")),
        name="moe_experts",
    )(tile_expert, tile_valid, xs, w_gu, b_gu, w_down, b_down)


def _route(top_i):
    n = top_i.shape[0]
    na = n * TOP_K
    flat_e = top_i.reshape(na)
    order = jnp.argsort(flat_e, stable=True).astype(jnp.int32)
    inv = jnp.argsort(order).astype(jnp.int32)
    counts = jnp.sum((flat_e[:, None] == jnp.arange(N_EXPERTS, dtype=jnp.int32)[None, :]).astype(jnp.int32),
                     axis=0)
    padded = ((counts + MOE_TM - 1) // MOE_TM) * MOE_TM
    pad_start = jnp.cumsum(padded) - padded
    start = jnp.cumsum(counts) - counts
    pos = (inv + (pad_start - start)[flat_e]).reshape(n, TOP_K)
    nt = na // MOE_TM + N_EXPERTS
    p_rows = nt * MOE_TM
    tile_start = jnp.arange(nt, dtype=jnp.int32) * MOE_TM
    pad_end = pad_start + padded
    n_used = (jnp.sum(padded) // MOE_TM).astype(jnp.int32)
    tile_expert = jnp.minimum(jnp.sum((tile_start[:, None] >= pad_end[None, :]).astype(jnp.int32), axis=1),
                              N_EXPERTS - 1)
    used = jnp.arange(nt) < n_used
    tile_expert = jnp.where(used, tile_expert, tile_expert[jnp.maximum(n_used - 1, 0)])
    tile_valid = jnp.where(used, jnp.clip(counts[tile_expert] - (tile_start - pad_start[tile_expert]), 0, MOE_TM), 0)
    row = jnp.arange(p_rows, dtype=jnp.int32)
    row_e = jnp.repeat(tile_expert, MOE_TM)
    rank = row - pad_start[row_e]
    src = jnp.clip(start[row_e] + rank, 0, na - 1)
    row_token = jnp.where(rank < counts[row_e], order[src] // TOP_K, 0)
    return row_token, pos, tile_expert.astype(jnp.int32), tile_valid.astype(jnp.int32)


def _pad_cols(a, width):
    return jnp.pad(a, ((0, 0), (0, width - a.shape[1])))


def _pack_cols(a):
    o_w = 3 * RWKV_WIDTH
    o_a = o_w + DECAY_LORA
    o_g = o_a + AAA_LORA
    fb = A_COLS
    return jnp.concatenate([
        a[:, 0:3 * RWKV_WIDTH],
        a[:, fb:fb + 4 * FOX_WIDTH],
        _pad_cols(a[:, o_g:A_COLS], 256),
        a[:, o_w:o_g],
        _pad_cols(a[:, fb + 4 * FOX_WIDTH:], LANES),
    ], axis=1)


def _unpack_a_cols(z):
    return jnp.concatenate([z[..., 0:3 * RWKV_WIDTH], z[..., C_WA:C_WA + 128], z[..., C_GL:C_GL + GATE_LORA]],
                           axis=-1)


def _prev_rows(z, tm, first):
    nseq, t, _ = z.shape
    if t > tm:
        inner = z[:, tm - 1:t - 1:tm]
        rows = jnp.concatenate([first, inner], axis=1)
    else:
        rows = first
    rows = rows.reshape(-1, 1, Z_COLS)
    return rows[..., 0:3072], rows[..., C_GL:C_GL + 256], rows[..., C_WA:C_WA + 128]


def kernel(x_prompt, x_sample, mem_prompt, cache_fox_k, cache_fox_v, cache_fox_logf, state_rwkv, state_shift,
           cache_mem_k, cache_mem_v, norm_mix, w_in, rwkv_mu, rwkv_w0, rwkv_w2, rwkv_a0, rwkv_a2, rwkv_g2,
           rwkv_k_k, rwkv_k_a, rwkv_r_k, rwkv_lnx_w, rwkv_lnx_b, fox_b_f, fox_q_norm, fox_k_norm, w_out,
           norm_cross, norm_mem, w_xq, w_xk, w_xv, w_xo, xq_norm, xk_norm, norm_ffn, w_router, b_router,
           w_gu, b_gu, w_down, b_down):
    bp, sp, _ = x_prompt.shape
    bs, ts, _ = x_sample.shape
    past = cache_fox_k.shape[2]
    n_p, n_s = bp * sp, bs * ts
    l = 0

    w_in_p = _pack_cols(w_in[l]).astype(BF16)
    mu_p = _pack_cols(jnp.pad(rwkv_mu[l][None, :], ((0, 0), (0, w_in.shape[2] - A_COLS))))
    zero_lora = jnp.zeros((DECAY_LORA, RWKV_WIDTH), F32)
    prep = {
        "mu_m": mu_p[:, 0:3072], "mu_g": mu_p[:, C_GL:C_GL + 256], "mu_w": mu_p[:, C_WA:C_WA + 128],
        "w0": rwkv_w0[l][None], "a0": rwkv_a0[l][None],
        "w2": jnp.concatenate([rwkv_w2[l], zero_lora], axis=0).astype(BF16),
        "a2": jnp.concatenate([zero_lora, rwkv_a2[l]], axis=0).astype(BF16),
        "g2": jnp.pad(rwkv_g2[l], ((0, 256 - GATE_LORA), (0, 0))).astype(BF16),
        "k_k": rwkv_k_k[l][None], "k_a": rwkv_k_a[l][None], "r_k": rwkv_r_k[l].reshape(1, RWKV_WIDTH),
    }
    lnw, lnb = rwkv_lnx_w[l][None], rwkv_lnx_b[l][None]
    qg = jnp.tile(fox_q_norm[l], FOX_HEADS)[None]
    kg = jnp.tile(fox_k_norm[l], FOX_HEADS)[None]
    bf = _pad_cols(fox_b_f[l][None], LANES)
    w_out_b = w_out[l].astype(BF16)
    w_kv = jnp.concatenate([w_xk[l], w_xv[l]], axis=1).astype(BF16)
    pc = {
        "g_cross": norm_cross[l][None], "w_xq": w_xq[l].astype(BF16), "xq_gain": xq_norm[l][None],
        "w_xo": w_xo[l].astype(BF16), "g_ffn": norm_ffn[l][None],
        "w_router": _pad_cols(w_router[l], LANES),
        "b_router": jnp.concatenate([b_router[l], jnp.full((LANES - N_EXPERTS,), NEG_BIG, F32)])[None],
    }

    def mixers(x, nseq, t, first_shift, s0, tm_mm, tm_prep, tb, dim_major):
        z = _norm_mm(x, norm_mix[l][None], w_in_p, tm_mm, 768)
        z3 = z.reshape(nseq, t, Z_COLS)
        pm, pg, pw = _prev_rows(z3, tm_prep, first_shift)
        *steps, g, bonus = _rwkv_prep(z, pm, pg, pw, prep, tm_prep)
        steps = [a.reshape(nseq, t, HEAD_PAIRS, LANES) for a in steps]
        ya, s_fin = _rwkv_scan(steps, g.reshape(nseq, t, RWKV_WIDTH), bonus.reshape(nseq, t, RWKV_WIDTH),
                               _state_to_pairs(s0), lnw, lnb, tb)
        fox = _fox_proj(z, qg, kg, bf, tm_prep, nseq if dim_major else None)
        return z, z3, ya.reshape(nseq * t, RWKV_WIDTH), _pairs_to_state(s_fin), fox

    zero_shift = jnp.zeros((bp, 1, Z_COLS), F32)
    zero_state = jnp.zeros((bp, RWKV_HEADS, HEAD_DIM, HEAD_DIM), F32)
    xp = x_prompt.reshape(n_p, D_MODEL)
    xs = x_sample.reshape(n_s, D_MODEL)
    zp, zp3, ya_p, st_p, (qb_p, kb_p, lf_p, kt_p, vt_p, vtb_p, lft_p) = mixers(
        xp, bp, sp, zero_shift, zero_state, 1024, 256, SCAN_BLOCK, True)
    shift_s = _pack_cols(jnp.pad(state_shift[l].reshape(bs, A_COLS), ((0, 0), (0, w_in.shape[2] - A_COLS))))
    zs, zs3, ya_s, st_s, (qb_s, kb_s, lf_s, kf_s) = mixers(
        xs, bs, ts, shift_s.reshape(bs, 1, Z_COLS), state_rwkv[l], 256, ts, ts, False)
    lf_s = lf_s[:, :FOX_HEADS]

    fp = jnp.cumsum(lft_p, axis=2) * LOG2E
    fq_p, fk_p = _prefix_terms(fp, bp, sp)
    yb_p = _fox_prompt(qb_p, kb_p, fq_p, fk_p, vtb_p.reshape(bp, FOX_HEADS, HEAD_DIM, sp), zp, bp, sp,
                       min(1024, sp))

    fs = jnp.cumsum(jnp.concatenate([cache_fox_logf[l].astype(F32), lf_s.reshape(bs, ts, FOX_HEADS)], axis=1),
                    axis=1) * LOG2E
    fn_s = fs[:, past:].transpose(0, 2, 1)
    kc_t = cache_fox_k[l].transpose(0, 2, 3, 1)
    vc_t = cache_fox_v[l].transpose(0, 2, 3, 1)
    yb_s = _fox_sample(qb_s, kc_t, vc_t, kb_s, zs, fn_s.reshape(bs, FOX_HEADS * ts, 1),
                       fs[:, :past].transpose(0, 2, 1), fn_s, bs, ts, min(512, past))

    x1_p = _out_proj(xp, ya_p, yb_p, w_out_b, 512)
    x1_s = _out_proj(xs, ya_s, yb_s, w_out_b, 256)

    kv_p = _mem_kv(mem_prompt.reshape(bp * N_MEM, D_MODEL), norm_mem[l][None], w_kv, xk_norm[l][None], 256)
    mk_p = kv_p[:, :MEM_WIDTH].reshape(bp, N_MEM, MEM_WIDTH)
    mv_p = kv_p[:, MEM_WIDTH:].reshape(bp, N_MEM, MEM_WIDTH)
    x2_p, h_p, ti_p, tg_p = _cross_router(x1_p, mk_p.astype(BF16), mv_p.astype(BF16), pc, bp, 256)
    mk_s = cache_mem_k[l].reshape(bs, N_MEM, MEM_WIDTH).astype(BF16)
    mv_s = cache_mem_v[l].reshape(bs, N_MEM, MEM_WIDTH).astype(BF16)
    x2_s, h_s, ti_s, tg_s = _cross_router(x1_s, mk_s, mv_s, pc, bs, ts)

    h_all = jnp.concatenate([h_p, h_s], axis=0)
    top_i = jnp.concatenate([ti_p[:, :TOP_K], ti_s[:, :TOP_K]], axis=0)
    gates = jnp.concatenate([tg_p[:, :TOP_K], tg_s[:, :TOP_K]], axis=0)
    row_token, pos, tile_expert, tile_valid = _route(top_i)
    out_rows = _moe_experts(tile_expert, tile_valid, h_all[row_token], w_gu[l], b_gu[l][:, None, :], w_down[l],
                            b_down[l][:, None, :])
    def combine(x2, rows, gate):
        return x2 + sum(out_rows[rows[:, k]] * gate[:, k:k + 1] for k in range(TOP_K))

    y_p = combine(x2_p, pos[:n_p], gates[:n_p]).reshape(bp, sp, D_MODEL)
    y_s = combine(x2_s, pos[n_p:], gates[n_p:]).reshape(bs, ts, D_MODEL)

    to_cache = lambda a: a.reshape(1, bp, FOX_HEADS, HEAD_DIM, sp).transpose(0, 1, 4, 2, 3)
    pk, pv, plf = to_cache(kt_p), to_cache(vt_p), lft_p.transpose(0, 2, 1)[None]
    sk = kf_s.reshape(1, bs, ts, FOX_HEADS, HEAD_DIM)
    sv = zs3[..., C_FV:C_FV + FOX_WIDTH].reshape(1, bs, ts, FOX_HEADS, HEAD_DIM)
    slf = lf_s.reshape(1, bs, ts, FOX_HEADS)
    return (y_p, y_s, pk, pv, plf, st_p[None], _unpack_a_cols(zp3[:, -1:])[None],
            mk_p.reshape(1, bp, N_MEM, MEM_HEADS, MEM_HEAD_DIM), mv_p.reshape(1, bp, N_MEM, MEM_HEADS, MEM_HEAD_DIM),
            sk, sv, slf, st_s[None], _unpack_a_cols(zs3[:, -1:])[None])
```

```python
import functools

import jax
import jax.numpy as jnp
from jax import lax
from jax.experimental import pallas as pl
from jax.experimental.pallas import tpu as pltpu

F32 = jnp.float32
BF16 = jnp.bfloat16

D_MODEL = 2048
HEAD_DIM = 64
RWKV_WIDTH = 1024
FOX_WIDTH = 1024
RWKV_HEADS = 16
FOX_HEADS = 16
DECAY_LORA = 64
AAA_LORA = 64
GATE_LORA = 160
A_COLS = 3 * RWKV_WIDTH + DECAY_LORA + AAA_LORA + GATE_LORA
N_MEM = 256
MEM_HEADS = 4
MEM_HEAD_DIM = 128
MEM_WIDTH = 512
N_EXPERTS = 32
TOP_K = 4
D_FF = 2048
SWIGLU_LIMIT = 7.0
SWIGLU_ALPHA = 1.702
RMS_EPS = 1e-6
GN_EPS = 64e-5

LANES = 128
HEAD_PAIRS = 8

C_R, C_K, C_V = 0, 1024, 2048
C_FQ, C_FK, C_FV, C_FG = 3072, 4096, 5120, 6144
C_GL = 7168
C_WA = 7424
C_FL = 7552
Z_COLS = 7680

VMEM_LIMIT = 56 * 1024 * 1024
NEG_BIG = -1e30
LOG2E = 1.4426950408889634


def _cparams(sem):
    return pltpu.CompilerParams(dimension_semantics=sem, vmem_limit_bytes=VMEM_LIMIT)


def _block_ones():
    r = lax.broadcasted_iota(jnp.int32, (LANES, LANES), 0) // HEAD_DIM
    c = lax.broadcasted_iota(jnp.int32, (LANES, LANES), 1) // HEAD_DIM
    return (r == c).astype(BF16)


def _seg_sum64(x):
    ones = _block_ones()
    outs = []
    for s in range(x.shape[-1] // LANES):
        xs = x[:, s * LANES:(s + 1) * LANES]
        hi = xs.astype(BF16)
        lo = (xs - hi.astype(F32)).astype(BF16)
        outs.append(jnp.dot(hi, ones, preferred_element_type=F32)
                    + jnp.dot(lo, ones, preferred_element_type=F32))
    return outs[0] if len(outs) == 1 else jnp.concatenate(outs, axis=-1)


def _softplus(x):
    return jnp.maximum(x, 0.0) + jnp.log(1.0 + jnp.exp(-jnp.abs(x)))


def _sigmoid(x):
    return 1.0 / (1.0 + jnp.exp(-x))


def _norm_mm_kernel(x_ref, g_ref, w_ref, o_ref, xn_ref):
    @pl.when(pl.program_id(1) == 0)
    def _():
        x = x_ref[...]
        ms = jnp.mean(x * x, axis=-1, keepdims=True)
        xn_ref[...] = (x * lax.rsqrt(ms + RMS_EPS) * g_ref[...]).astype(BF16)

    o_ref[...] = jnp.dot(xn_ref[...], w_ref[...], preferred_element_type=F32)


def _norm_mm(x, g, w, tm, tn):
    m, k = x.shape
    n = w.shape[1]
    return pl.pallas_call(
        _norm_mm_kernel,
        grid=(m // tm, n // tn),
        in_specs=[pl.BlockSpec((tm, k), lambda i, j: (i, 0)),
                  pl.BlockSpec((1, k), lambda i, j: (0, 0)),
                  pl.BlockSpec((k, tn), lambda i, j: (0, j))],
        out_specs=pl.BlockSpec((tm, tn), lambda i, j: (i, j)),
        out_shape=jax.ShapeDtypeStruct((m, n), F32),
        scratch_shapes=[pltpu.VMEM((tm, k), BF16)],
        compiler_params=_cparams(("parallel", "arbitrary")),
        name="norm_mm",
    )(x, g, w)


def _rwkv_prep_kernel(zm_ref, zg_ref, zw_ref, pm_ref, pg_ref, pw_ref, fm_ref, fg_ref, fw_ref,
                      mum_ref, mug_ref, muw_ref, w0_ref, w2_ref, a0_ref, a2_ref, g2_ref,
                      kk_ref, ka_ref, rk_ref,
                      r_o, d_o, k_o, v_o, kk_o, b_o, g_o, bonus_o, *, per):
    seq_start = pl.program_id(0) % per == 0

    def shifted(z_ref, p_ref, f_ref, mu_ref):
        z = z_ref[...]
        rolled = pltpu.roll(z, 1, 0)
        row = lax.broadcasted_iota(jnp.int32, z.shape, 0)
        before = jnp.where(seq_start, f_ref[0], p_ref[7:8, :])
        prev = jnp.where(row == 0, before, rolled)
        return z + mu_ref[...] * (prev - z)

    zm = shifted(zm_ref, pm_ref, fm_ref, mum_ref)
    zg = shifted(zg_ref, pg_ref, fg_ref, mug_ref)
    zw = shifted(zw_ref, pw_ref, fw_ref, muw_ref)
    r = zm[:, C_R:C_R + RWKV_WIDTH]
    k = zm[:, C_K:C_K + RWKV_WIDTH]
    v = zm[:, C_V:C_V + RWKV_WIDTH]
    lw = jnp.dot(jnp.tanh(zw).astype(BF16), w2_ref[...], preferred_element_type=F32)
    la = jnp.dot(zw.astype(BF16), a2_ref[...], preferred_element_type=F32)
    g = jnp.dot(_sigmoid(zg).astype(BF16), g2_ref[...], preferred_element_type=F32)
    w_log = -_softplus(-(w0_ref[...] + lw)) - 0.5
    a = _sigmoid(a0_ref[...] + la)
    kk = k * kk_ref[...]
    kk = kk * lax.rsqrt(jnp.maximum(_seg_sum64(kk * kk), 1e-24))
    kh = k * (1.0 + (a - 1.0) * ka_ref[...])
    tiles = lambda x: x.reshape(x.shape[0], HEAD_PAIRS, LANES)
    r_o[...] = tiles(r)
    d_o[...] = tiles(jnp.exp(-jnp.exp(w_log)))
    k_o[...] = tiles(kh)
    v_o[...] = tiles(v)
    kk_o[...] = tiles(kk)
    b_o[...] = tiles(kk * a)
    g_o[...] = g
    bonus_o[...] = _seg_sum64(r * kh * rk_ref[...]) * v


def _rwkv_prep(z, first, pr, tm):
    n = z.shape[0]
    per = n // first.shape[0] // tm
    sub = 8
    row = lambda w, c: pl.BlockSpec((tm, w), lambda i, c=c: (i, c))
    prev = lambda w, c: pl.BlockSpec((sub, w), lambda i, c=c: (jnp.maximum(i * (tm // sub) - 1, 0), c))
    head = lambda w: pl.BlockSpec((1, 1, w), lambda i: (i // per, 0, 0))
    full = lambda a: pl.BlockSpec(a.shape, lambda i: (0,) * a.ndim)
    firsts = [first[..., 0:3072], first[..., C_GL:C_GL + 256], first[..., C_WA:C_WA + 128]]
    params = [pr["mu_m"], pr["mu_g"], pr["mu_w"], pr["w0"], pr["w2"], pr["a0"], pr["a2"], pr["g2"],
              pr["k_k"], pr["k_a"], pr["r_k"]]
    tiled = jax.ShapeDtypeStruct((n, HEAD_PAIRS, LANES), F32)
    flat = jax.ShapeDtypeStruct((n, RWKV_WIDTH), F32)
    return pl.pallas_call(
        functools.partial(_rwkv_prep_kernel, per=per),
        grid=(n // tm,),
        in_specs=[row(3072, 0), row(256, C_GL // 256), row(128, C_WA // 128),
                  prev(3072, 0), prev(256, C_GL // 256), prev(128, C_WA // 128),
                  head(3072), head(256), head(128)] + [full(a) for a in params],
        out_specs=[pl.BlockSpec((tm, HEAD_PAIRS, LANES), lambda i: (i, 0, 0))] * 6
        + [pl.BlockSpec((tm, RWKV_WIDTH), lambda i: (i, 0))] * 2,
        out_shape=[tiled] * 6 + [flat] * 2,
        compiler_params=_cparams(("parallel",)),
        name="rwkv_prep",
    )(z, z, z, z, z, z, *firsts, *params)


SEQ_PER_STEP = 2
SCAN_BLOCK = 128


def _rwkv_scan_kernel(r_ref, d_ref, k_ref, v_ref, kk_ref, b_ref, g_ref, bonus_ref, s0_ref,
                      lw_ref, lb_ref,
                      y_ref, sf_ref,
                      s_ref, oa_ref, *, tb):
    tblk = pl.program_id(1)

    @pl.when(tblk == 0)
    def _():
        s_ref[...] = s0_ref[...]

    oa_ref[...] = jnp.zeros_like(oa_ref)
    r2 = lax.broadcasted_iota(jnp.int32, (2 * LANES, 2 * LANES), 0) // HEAD_DIM
    c2 = lax.broadcasted_iota(jnp.int32, (2 * LANES, 2 * LANES), 1) // HEAD_DIM
    ones = (r2 == c2).astype(BF16)
    lane = lax.broadcasted_iota(jnp.int32, (HEAD_DIM, LANES), 1)
    sub = lax.broadcasted_iota(jnp.int32, (HEAD_DIM, LANES), 0)
    diag = (lane % HEAD_DIM) == sub
    tiles = [(s, hp) for s in range(SEQ_PER_STEP) for hp in range(HEAD_PAIRS)]

    def bcast(ref, s, t, hp):
        return jnp.broadcast_to(ref[s, t, hp:hp + 1, :], (HEAD_DIM, LANES))

    def seg_sums(parts):
        lhs = jnp.concatenate([jnp.concatenate(parts[j:j + 2], axis=1) for j in range(0, len(parts), 2)], axis=0)
        out = jnp.dot(lhs, ones, preferred_element_type=F32)
        return [out[(j // 2) * HEAD_DIM:(j // 2 + 1) * HEAD_DIM, (j % 2) * LANES:(j % 2 + 1) * LANES]
                for j in range(len(parts))]

    def emit_outputs(t_out, valid):
        half = pl.multiple_of((t_out // HEAD_DIM) * HEAD_DIM, HEAD_DIM)
        hit = jnp.logical_and((lane % HEAD_DIM) == (t_out % HEAD_DIM), valid)
        ob = seg_sums([(s_ref[s, hp] * bcast(r_ref, s, t_out, hp)).astype(BF16) for s, hp in tiles])
        for j, (s, hp) in enumerate(tiles):
            cur = oa_ref[s, hp, pl.ds(half, HEAD_DIM), :]
            oa_ref[s, hp, pl.ds(half, HEAD_DIM), :] = jnp.where(hit, ob[j], cur)

    def step(t, carry):
        emit_outputs(jnp.maximum(t - 1, 0), t > 0)
        skk = seg_sums([(s_ref[s, hp] * bcast(kk_ref, s, t, hp)).astype(BF16) for s, hp in tiles])
        vb = seg_sums([jnp.where(diag, bcast(v_ref, s, t, hp), 0.0).astype(BF16) for s, hp in tiles])
        for j, (s, hp) in enumerate(tiles):
            s_ref[s, hp] = (s_ref[s, hp] * bcast(d_ref, s, t, hp)
                            - skk[j] * bcast(b_ref, s, t, hp)
                            + vb[j] * bcast(k_ref, s, t, hp))
        return carry

    lax.fori_loop(0, tb, step, 0)
    emit_outputs(tb - 1, True)

    lane_t = lax.broadcasted_iota(jnp.int32, (HEAD_DIM, LANES), 1)
    low = lane_t < HEAD_DIM
    for s in range(SEQ_PER_STEP):
        slabs = []
        for hp in range(HEAD_PAIRS):
            mt = oa_ref[s, hp].T
            ro = pltpu.roll(mt, HEAD_DIM, 1)
            top = jnp.where(low, mt[:HEAD_DIM], ro[HEAD_DIM:])
            bot = jnp.where(low, ro[:HEAD_DIM], mt[HEAD_DIM:])
            slabs.append(jnp.concatenate([top, bot], axis=0)[:tb])
        o = jnp.concatenate(slabs, axis=-1)
        mean = _seg_sum64(o) * (1.0 / HEAD_DIM)
        cen = o - mean
        var = _seg_sum64(cen * cen) * (1.0 / HEAD_DIM)
        on = cen * lax.rsqrt(var + GN_EPS) * lw_ref[...] + lb_ref[...]
        y_ref[s] = ((on + bonus_ref[s]) * g_ref[s]).astype(y_ref.dtype)

    @pl.when(tblk == pl.num_programs(1) - 1)
    def _():
        sf_ref[...] = s_ref[...]


def _rwkv_scan(steps, g, bonus, s0, lnw, lnb, tb):
    nseq, t = g.shape[:2]
    step_spec = pl.BlockSpec((SEQ_PER_STEP, tb, HEAD_PAIRS, LANES), lambda i, j: (i, j, 0, 0))
    seq_spec = pl.BlockSpec((SEQ_PER_STEP, tb, RWKV_WIDTH), lambda i, j: (i, j, 0))
    st_spec = pl.BlockSpec((SEQ_PER_STEP, HEAD_PAIRS, HEAD_DIM, LANES), lambda i, j: (i, 0, 0, 0))
    par_spec = pl.BlockSpec((1, RWKV_WIDTH), lambda i, j: (0, 0))
    return pl.pallas_call(
        functools.partial(_rwkv_scan_kernel, tb=tb),
        grid=(nseq // SEQ_PER_STEP, t // tb),
        in_specs=[step_spec] * 6 + [seq_spec] * 2 + [st_spec] + [par_spec] * 2,
        out_specs=[seq_spec, st_spec],
        out_shape=[jax.ShapeDtypeStruct((nseq, t, RWKV_WIDTH), BF16),
                   jax.ShapeDtypeStruct(s0.shape, F32)],
        scratch_shapes=[pltpu.VMEM((SEQ_PER_STEP, HEAD_PAIRS, HEAD_DIM, LANES), F32),
                        pltpu.VMEM((SEQ_PER_STEP, HEAD_PAIRS, LANES, LANES), F32)],
        compiler_params=_cparams(("parallel", "arbitrary")),
        name="rwkv_scan",
    )(*steps, g, bonus, s0, lnw, lnb)


def _state_to_pairs(s):
    b = s.shape[0]
    return s.reshape(b, HEAD_PAIRS, 2, HEAD_DIM, HEAD_DIM).transpose(0, 1, 3, 2, 4).reshape(
        b, HEAD_PAIRS, HEAD_DIM, LANES)


def _pairs_to_state(s):
    b = s.shape[0]
    return s.reshape(b, HEAD_PAIRS, HEAD_DIM, 2, HEAD_DIM).transpose(0, 1, 3, 2, 4).reshape(
        b, RWKV_HEADS, HEAD_DIM, HEAD_DIM)


def _fox_proj_kernel(q_ref, k_ref, v_ref, fl_ref, qg_ref, kg_ref, bf_ref, qb_o, kb_o, lf_o, *cache_o, dim_major):
    def headnorm(x, g):
        ms = _seg_sum64(x * x) * (1.0 / HEAD_DIM)
        return x * lax.rsqrt(ms + RMS_EPS) * g

    q = headnorm(q_ref[...], qg_ref[...])
    k = headnorm(k_ref[...], kg_ref[...])
    q = q * (HEAD_DIM ** -0.5 * LOG2E)
    lf = -_softplus(-(fl_ref[...] + bf_ref[...]))
    lf_o[...] = lf
    if dim_major:
        lane = lax.broadcasted_iota(jnp.int32, (q.shape[0], LANES), 1)
        for h in range(FOX_HEADS):
            slab = slice((h // 2) * LANES, (h // 2 + 1) * LANES)
            for x, o in ((q, qb_o), (k, kb_o)):
                xs = x[:, slab] if h % 2 == 0 else pltpu.roll(x[:, slab], HEAD_DIM, 1)
                o[0, h] = jnp.where(lane < HEAD_DIM, xs, 0.0).astype(BF16)
        kt_o, vt_o, vtb_o, lft_o = cache_o
        vt = v_ref[...].T
        kt_o[0] = k.T
        vt_o[0] = vt
        vtb_o[0] = vt.astype(BF16)
        lft_o[0] = lf.T[:FOX_HEADS]
    else:
        qb_o[...] = q.astype(BF16)
        kb_o[...] = k.astype(BF16)
        cache_o[0][...] = k


def _fox_proj(z, qg, kg, bf, tm, nseq=None):
    n = z.shape[0]
    col = lambda w, c: pl.BlockSpec((tm, w), lambda i, c=c: (i, c))
    par = lambda w: pl.BlockSpec((1, w), lambda i: (0, 0))
    wide = pl.BlockSpec((tm, FOX_WIDTH), lambda i: (i, 0))
    lf_spec = pl.BlockSpec((tm, LANES), lambda i: (i, 0))
    lf_shape = jax.ShapeDtypeStruct((n, LANES), F32)
    if nseq is None:
        out_specs = [wide, wide, lf_spec, wide]
        out_shape = [jax.ShapeDtypeStruct((n, FOX_WIDTH), BF16), jax.ShapeDtypeStruct((n, FOX_WIDTH), BF16),
                     lf_shape, jax.ShapeDtypeStruct((n, FOX_WIDTH), F32)]
    else:
        t = n // nseq
        per = t // tm
        heads = pl.BlockSpec((1, FOX_HEADS, tm, LANES), lambda i: (i // per, 0, i % per, 0))
        tall = pl.BlockSpec((1, FOX_WIDTH, tm), lambda i: (i // per, 0, i % per))
        padded = jax.ShapeDtypeStruct((nseq, FOX_HEADS, t, LANES), BF16)
        out_specs = [heads, heads, lf_spec, tall, tall, tall,
                     pl.BlockSpec((1, FOX_HEADS, tm), lambda i: (i // per, 0, i % per))]
        out_shape = [padded, padded, lf_shape,
                     jax.ShapeDtypeStruct((nseq, FOX_WIDTH, t), F32), jax.ShapeDtypeStruct((nseq, FOX_WIDTH, t), F32),
                     jax.ShapeDtypeStruct((nseq, FOX_WIDTH, t), BF16),
                     jax.ShapeDtypeStruct((nseq, FOX_HEADS, t), F32)]
    return pl.pallas_call(
        functools.partial(_fox_proj_kernel, dim_major=nseq is not None),
        grid=(n // tm,),
        in_specs=[col(1024, C_FQ // 1024), col(1024, C_FK // 1024), col(1024, C_FV // 1024), col(128, C_FL // 128),
                  par(1024), par(1024), par(128)],
        out_specs=out_specs,
        out_shape=out_shape,
        compiler_params=_cparams(("parallel",)),
        name="fox_proj",
    )(z, z, z, z, qg, kg, bf)


F_TERMS = 8


def _fox_prompt_kernel(qi_ref, kj_ref, q_ref, k_ref, fq_ref, fk_ref, vt_ref, gate_ref, y_ref,
                       qa_ref, m_ref, l_ref, acc_ref, *, tq):
    pair = pl.program_id(2)
    qi = qi_ref[pair]
    kj = kj_ref[pair]
    lane = lax.broadcasted_iota(jnp.int32, (tq, LANES), 1)
    zero = jnp.zeros((tq, LANES), BF16)

    def with_terms(x, f, h):
        lo = HEAD_DIM + F_TERMS * h
        return x + jnp.where(jnp.logical_and(lane >= lo, lane < lo + F_TERMS), f, zero)

    @pl.when(kj == 0)
    def _():
        m_ref[...] = jnp.full_like(m_ref, NEG_BIG)
        l_ref[...] = jnp.zeros_like(l_ref)
        acc_ref[...] = jnp.zeros_like(acc_ref)
        for h in range(2):
            qa_ref[h] = with_terms(q_ref[0, h], fq_ref[0, 0], h)

    def body(masked):
        for h in range(2):
            st = lax.dot_general(with_terms(k_ref[0, h], fk_ref[0, 0], h), qa_ref[h], (((1,), (1,)), ((), ())),
                                 preferred_element_type=F32)
            if masked:
                key = lax.broadcasted_iota(jnp.int32, st.shape, 0)
                qry = lax.broadcasted_iota(jnp.int32, st.shape, 1)
                st = jnp.where(key <= qry, st, NEG_BIG)
            m_prev = m_ref[h]
            m_new = jnp.maximum(m_prev, jnp.max(st, axis=0, keepdims=True))
            alpha = jnp.exp2(m_prev - m_new)
            p = jnp.exp2(st - m_new)
            l_ref[h] = alpha * l_ref[h] + jnp.sum(p, axis=0, keepdims=True)
            acc_ref[h] = alpha * acc_ref[h] + jnp.dot(vt_ref[0, h], p.astype(BF16), preferred_element_type=F32)
            m_ref[h] = m_new

    @pl.when(kj < qi)
    def _():
        body(False)

    @pl.when(kj == qi)
    def _():
        body(True)
        ot = jnp.concatenate([acc_ref[0] / l_ref[0], acc_ref[1] / l_ref[1]], axis=0)
        y_ref[...] = (ot.T * _sigmoid(gate_ref[...])).astype(y_ref.dtype)


def _fox_prompt(q_pad, k_pad, fq, fk, vt, z, nb, seq, tq):
    nq = seq // tq
    pairs = [(i, j) for i in range(nq) for j in range(i + 1)]
    qi_tab = jnp.asarray([p[0] for p in pairs], jnp.int32)
    kj_tab = jnp.asarray([p[1] for p in pairs], jnp.int32)
    grid_spec = pltpu.PrefetchScalarGridSpec(
        num_scalar_prefetch=2,
        grid=(nb, HEAD_PAIRS, len(pairs)),
        in_specs=[pl.BlockSpec((1, 2, tq, LANES), lambda b, hp, p, qi, kj: (b, hp, qi[p], 0)),
                  pl.BlockSpec((1, 2, tq, LANES), lambda b, hp, p, qi, kj: (b, hp, kj[p], 0)),
                  pl.BlockSpec((1, 1, tq, LANES), lambda b, hp, p, qi, kj: (b, hp, qi[p], 0)),
                  pl.BlockSpec((1, 1, tq, LANES), lambda b, hp, p, qi, kj: (b, hp, kj[p], 0)),
                  pl.BlockSpec((1, 2, HEAD_DIM, tq), lambda b, hp, p, qi, kj: (b, hp, 0, kj[p])),
                  pl.BlockSpec((tq, LANES), lambda b, hp, p, qi, kj: (b * nq + qi[p], C_FG // LANES + hp))],
        out_specs=pl.BlockSpec((tq, LANES), lambda b, hp, p, qi, kj: (b * nq + qi[p], hp)),
        scratch_shapes=[pltpu.VMEM((2, tq, LANES), BF16), pltpu.VMEM((2, 1, tq), F32), pltpu.VMEM((2, 1, tq), F32),
                        pltpu.VMEM((2, HEAD_DIM, tq), F32)],
    )
    return pl.pallas_call(
        functools.partial(_fox_prompt_kernel, tq=tq),
        grid_spec=grid_spec,
        out_shape=jax.ShapeDtypeStruct((nb * seq, FOX_WIDTH), BF16),
        compiler_params=_cparams(("parallel", "parallel", "arbitrary")),
        name="fox_prompt",
    )(qi_tab, kj_tab, q_pad, k_pad, fq, fk, vt, z)


def _split3(x):
    def top(v):
        bits = lax.bitcast_convert_type(v, jnp.uint32) & jnp.uint32(0xFFFF0000)
        return lax.bitcast_convert_type(bits, F32)

    hi = top(x)
    mid = top(x - hi)
    lo = top(x - hi - mid)
    return hi.astype(BF16), mid.astype(BF16), lo.astype(BF16)


def _prefix_terms(f2, nb, seq):
    one = jnp.ones_like(f2, BF16)
    none = jnp.zeros_like(f2, BF16)

    def lanes(terms):
        t = jnp.stack([*terms, none, none], axis=-1)
        t = t.reshape(nb, HEAD_PAIRS, 2, seq, F_TERMS).transpose(0, 1, 3, 2, 4).reshape(nb, HEAD_PAIRS, seq, 16)
        return jnp.pad(t, ((0, 0), (0, 0), (0, 0), (HEAD_DIM, LANES - HEAD_DIM - 16)))

    return lanes([*_split3(f2), one, one, one]), lanes([one, one, one, *_split3(-f2)])


def _fox_sample_kernel(q_ref, kc_ref, vc_ref, kn_ref, vn_ref, fq_ref, fkc_ref, fkn_ref, gate_ref, y_ref,
                       m_ref, l_ref, acc_ref, *, nk):
    kj = pl.program_id(1)

    @pl.when(kj == 0)
    def _():
        m_ref[...] = jnp.full_like(m_ref, NEG_BIG)
        l_ref[...] = jnp.zeros_like(l_ref)
        acc_ref[...] = jnp.zeros_like(acc_ref)

    def head_slice(ref, h):
        return ref[:, h * HEAD_DIM:(h + 1) * HEAD_DIM]

    t = q_ref.shape[0]

    nt_dims = (((1,), (1,)), ((), ()))

    def attend(ks, vs, fk_ref, masked, dim_major):
        if dim_major:
            qk = lambda h: jnp.dot(head_slice(q_ref, h), ks[h], preferred_element_type=F32)
            pv_of = lambda ph, h: lax.dot_general(ph, vs[h], nt_dims, preferred_element_type=F32)
        else:
            qk = lambda h: lax.dot_general(head_slice(q_ref, h), ks[h], nt_dims, preferred_element_type=F32)
            pv_of = lambda ph, h: jnp.dot(ph, vs[h], preferred_element_type=F32)
        s = jnp.concatenate([qk(h) for h in range(FOX_HEADS)], axis=0)
        fk = jnp.concatenate([jnp.broadcast_to(fk_ref[0, h:h + 1, :], (t, s.shape[1])) for h in range(FOX_HEADS)],
                             axis=0)
        s = s + fq_ref[0] - fk
        if masked:
            qry = lax.broadcasted_iota(jnp.int32, s.shape, 0) % t
            key = lax.broadcasted_iota(jnp.int32, s.shape, 1)
            s = jnp.where(key <= qry, s, NEG_BIG)
        m_prev = m_ref[...]
        m_new = jnp.maximum(m_prev, jnp.max(s, axis=-1, keepdims=True))
        alpha = jnp.exp2(m_prev - m_new)
        p = jnp.exp2(s - m_new)
        l_ref[...] = alpha * l_ref[...] + jnp.sum(p, axis=-1, keepdims=True)
        p = p.astype(BF16)
        pv = jnp.concatenate([pv_of(p[h * t:(h + 1) * t], h) for h in range(FOX_HEADS)], axis=0)
        acc_ref[...] = alpha * acc_ref[...] + pv
        m_ref[...] = m_new

    @pl.when(kj < nk)
    def _():
        attend([kc_ref[0, h].astype(BF16) for h in range(FOX_HEADS)],
               [vc_ref[0, h].astype(BF16) for h in range(FOX_HEADS)], fkc_ref, False, True)

    @pl.when(kj == nk)
    def _():
        attend([head_slice(kn_ref, h) for h in range(FOX_HEADS)],
               [head_slice(vn_ref, h).astype(BF16) for h in range(FOX_HEADS)], fkn_ref, True, False)
        o = acc_ref[...] / l_ref[...]
        for h in range(FOX_HEADS):
            gate = _sigmoid(head_slice(gate_ref, h))
            y_ref[:, h * HEAD_DIM:(h + 1) * HEAD_DIM] = (o[h * t:(h + 1) * t] * gate).astype(y_ref.dtype)


def _fox_sample(qb, kc, vc, kb, z, fq, fkc, fkn, nb, t, tk):
    past = kc.shape[3]
    nk = past // tk
    last = nk - 1
    wide = lambda c: pl.BlockSpec((t, FOX_WIDTH), lambda b, j, c=c: (b, c))
    cache = pl.BlockSpec((1, FOX_HEADS, HEAD_DIM, tk), lambda b, j: (b, 0, 0, jnp.minimum(j, last)))
    return pl.pallas_call(
        functools.partial(_fox_sample_kernel, nk=nk),
        grid=(nb, nk + 1),
        in_specs=[wide(0), cache, cache, wide(0), wide(C_FV // FOX_WIDTH),
                  pl.BlockSpec((1, FOX_HEADS * t, 1), lambda b, j: (b, 0, 0)),
                  pl.BlockSpec((1, FOX_HEADS, tk), lambda b, j: (b, 0, jnp.minimum(j, last))),
                  pl.BlockSpec((1, FOX_HEADS, t), lambda b, j: (b, 0, 0)),
                  wide(C_FG // FOX_WIDTH)],
        out_specs=wide(0),
        out_shape=jax.ShapeDtypeStruct((nb * t, FOX_WIDTH), BF16),
        scratch_shapes=[pltpu.VMEM((FOX_HEADS * t, 1), F32), pltpu.VMEM((FOX_HEADS * t, 1), F32),
                        pltpu.VMEM((FOX_HEADS * t, HEAD_DIM), F32)],
        compiler_params=_cparams(("parallel", "arbitrary")),
        name="fox_sample",
    )(qb, kc, vc, kb, z, fq, fkc, fkn, z)


def _out_proj_kernel(x_ref, ya_ref, yb_ref, wa_ref, wb_ref, o_ref):
    o_ref[...] = (x_ref[...]
                  + jnp.dot(ya_ref[...], wa_ref[...], preferred_element_type=F32)
                  + jnp.dot(yb_ref[...], wb_ref[...], preferred_element_type=F32))


def _out_proj(x, ya, yb, w, tm):
    n = x.shape[0]
    return pl.pallas_call(
        _out_proj_kernel,
        grid=(n // tm,),
        in_specs=[pl.BlockSpec((tm, D_MODEL), lambda i: (i, 0)),
                  pl.BlockSpec((tm, RWKV_WIDTH), lambda i: (i, 0)),
                  pl.BlockSpec((tm, FOX_WIDTH), lambda i: (i, 0)),
                  pl.BlockSpec((RWKV_WIDTH, D_MODEL), lambda i: (0, 0)),
                  pl.BlockSpec((FOX_WIDTH, D_MODEL), lambda i: (1, 0))],
        out_specs=pl.BlockSpec((tm, D_MODEL), lambda i: (i, 0)),
        out_shape=jax.ShapeDtypeStruct((n, D_MODEL), F32),
        compiler_params=_cparams(("parallel",)),
        name="out_proj",
    )(x, ya, yb, w, w)


def _mem_kv_kernel(x_ref, g_ref, w_ref, kg_ref, o_ref):
    x = x_ref[...]
    ms = jnp.mean(x * x, axis=-1, keepdims=True)
    xn = (x * lax.rsqrt(ms + RMS_EPS) * g_ref[...]).astype(BF16)
    kv = jnp.dot(xn, w_ref[...], preferred_element_type=F32)
    for h in range(MEM_HEADS):
        kh = kv[:, h * MEM_HEAD_DIM:(h + 1) * MEM_HEAD_DIM]
        ms = jnp.mean(kh * kh, axis=-1, keepdims=True)
        o_ref[:, h * MEM_HEAD_DIM:(h + 1) * MEM_HEAD_DIM] = kh * lax.rsqrt(ms + RMS_EPS) * kg_ref[...]
    o_ref[:, MEM_WIDTH:] = kv[:, MEM_WIDTH:]


def _mem_kv(mem, g, w_kv, kg, tm):
    n = mem.shape[0]
    return pl.pallas_call(
        _mem_kv_kernel,
        grid=(n // tm,),
        in_specs=[pl.BlockSpec((tm, D_MODEL), lambda i: (i, 0)),
                  pl.BlockSpec((1, D_MODEL), lambda i: (0, 0)),
                  pl.BlockSpec((D_MODEL, 2 * MEM_WIDTH), lambda i: (0, 0)),
                  pl.BlockSpec((1, MEM_HEAD_DIM), lambda i: (0, 0))],
        out_specs=pl.BlockSpec((tm, 2 * MEM_WIDTH), lambda i: (i, 0)),
        out_shape=jax.ShapeDtypeStruct((n, 2 * MEM_WIDTH), F32),
        compiler_params=_cparams(("parallel",)),
        name="mem_kv",
    )(mem, g, w_kv, kg)


def _cross_router_kernel(x_ref, gc_ref, wq_ref, qg_ref, mk_ref, mv_ref, wo_ref,
                         gf_ref, wr_ref, br_ref,
                         x2_ref, h_ref, idx_ref, gate_ref):
    x = x_ref[...]
    ms = jnp.mean(x * x, axis=-1, keepdims=True)
    xn = (x * lax.rsqrt(ms + RMS_EPS) * gc_ref[...]).astype(BF16)
    q = jnp.dot(xn, wq_ref[...], preferred_element_type=F32)
    outs = []
    for h in range(MEM_HEADS):
        sl = slice(h * MEM_HEAD_DIM, (h + 1) * MEM_HEAD_DIM)
        qh = q[:, sl]
        qms = jnp.mean(qh * qh, axis=-1, keepdims=True)
        qh = (qh * lax.rsqrt(qms + RMS_EPS) * qg_ref[...]).astype(BF16)
        s = lax.dot_general(qh, mk_ref[0, :, sl], (((1,), (1,)), ((), ())), preferred_element_type=F32)
        s = s * (MEM_HEAD_DIM ** -0.5)
        p = jnp.exp(s - jnp.max(s, axis=-1, keepdims=True))
        p = p / jnp.sum(p, axis=-1, keepdims=True)
        outs.append(jnp.dot(p.astype(BF16), mv_ref[0, :, sl], preferred_element_type=F32))
    o = jnp.concatenate(outs, axis=-1).astype(BF16)
    x2 = x + jnp.dot(o, wo_ref[...], preferred_element_type=F32)
    x2_ref[...] = x2

    ms2 = jnp.mean(x2 * x2, axis=-1, keepdims=True)
    hn = x2 * lax.rsqrt(ms2 + RMS_EPS) * gf_ref[...]
    h_ref[...] = hn.astype(BF16)
    logits = jnp.dot(hn, wr_ref[...], preferred_element_type=F32, precision=lax.Precision.HIGHEST)
    logits = logits + br_ref[...]
    lane = lax.broadcasted_iota(jnp.int32, logits.shape, 1)
    idx_acc = jnp.zeros(logits.shape, jnp.int32)
    val_acc = jnp.zeros(logits.shape, F32)
    top = None
    for kk in range(TOP_K):
        m = jnp.max(logits, axis=-1, keepdims=True)
        sel = jnp.min(jnp.where(logits == m, lane, LANES), axis=-1, keepdims=True)
        if kk == 0:
            top = m
        idx_acc = jnp.where(lane == kk, sel, idx_acc)
        val_acc = jnp.where(lane == kk, jnp.exp(m - top), val_acc)
        logits = jnp.where(lane == sel, -jnp.inf, logits)
    idx_ref[...] = idx_acc
    gate_ref[...] = val_acc / jnp.sum(val_acc, axis=-1, keepdims=True)


def _cross_router(x, mk, mv, pc, nb, tm):
    n = x.shape[0]
    per = n // nb // tm
    full = lambda a: pl.BlockSpec(a.shape, lambda i: (0,) * a.ndim)
    mem = pl.BlockSpec((1, N_MEM, MEM_WIDTH), lambda i: (i // per, 0, 0))
    row = lambda w: pl.BlockSpec((tm, w), lambda i: (i, 0))
    params1 = [pc["g_cross"], pc["w_xq"], pc["xq_gain"]]
    params2 = [pc["w_xo"], pc["g_ffn"], pc["w_router"], pc["b_router"]]
    return pl.pallas_call(
        _cross_router_kernel,
        grid=(n // tm,),
        in_specs=[row(D_MODEL)] + [full(a) for a in params1] + [mem, mem] + [full(a) for a in params2],
        out_specs=[row(D_MODEL), row(D_MODEL), row(LANES), row(LANES)],
        out_shape=[jax.ShapeDtypeStruct((n, D_MODEL), F32), jax.ShapeDtypeStruct((n, D_MODEL), BF16),
                   jax.ShapeDtypeStruct((n, LANES), jnp.int32), jax.ShapeDtypeStruct((n, LANES), F32)],
        compiler_params=_cparams(("parallel",)),
        name="cross_router",
    )(x, *params1, mk, mv, *params2)


MOE_TM = 768
MOE_TF = 256
MOE_TN = 1024
MOE_NG = D_FF // (2 * MOE_TF)
MOE_ND = D_MODEL // MOE_TN


def _moe_kernel(te_ref, tv_ref, xs_ref, wgu_ref, bgu_ref, wd_ref, bd_ref, o_ref, act_ref):
    s = pl.program_id(1)
    used = tv_ref[pl.program_id(0)] > 0
    half = 2 * MOE_TF

    @pl.when(jnp.logical_and(used, s < MOE_NG))
    def _():
        gu = jnp.dot(xs_ref[...], wgu_ref[0].astype(BF16), preferred_element_type=F32) + bgu_ref[0]
        glu = jnp.minimum(gu, SWIGLU_LIMIT)
        fglu = glu * _sigmoid(SWIGLU_ALPHA * glu)
        lin = jnp.clip(gu, -SWIGLU_LIMIT, SWIGLU_LIMIT) + 1.0
        act_a = pltpu.roll(fglu[:, :half], 1, 1) * lin[:, :half]
        act_b = fglu[:, half:] * pltpu.roll(lin[:, half:], half - 1, 1)
        lane = lax.broadcasted_iota(jnp.int32, act_a.shape, 1)
        merged = jnp.where(lane % 2 == 0, act_b, act_a).astype(BF16)
        for c in range(MOE_NG):
            @pl.when(s == c)
            def _(c=c):
                act_ref[:, c * half:(c + 1) * half] = merged

    @pl.when(jnp.logical_and(used, s >= MOE_NG))
    def _():
        top = lambda w: lax.bitcast_convert_type(w.astype(BF16).astype(F32), jnp.uint32)
        chunks = []
        for c in range(MOE_NG):
            a = wd_ref[0, c * half:c * half + MOE_TF, :]
            b = wd_ref[0, c * half + MOE_TF:(c + 1) * half, :]
            chunks.append(pltpu.bitcast((top(b) >> 16) | top(a), BF16))
        wd = jnp.concatenate(chunks, axis=0)
        o_ref[...] = jnp.dot(act_ref[...], wd, preferred_element_type=F32) + bd_ref[0]

    @pl.when(jnp.logical_and(jnp.logical_not(used), s >= MOE_NG))
    def _():
        o_ref[...] = jnp.zeros_like(o_ref)


def _moe_experts(tile_expert, tile_valid, xs, w_gu, b_gu, w_down, b_down):
    p = xs.shape[0]
    nt = p // MOE_TM
    gi = lambda i, s, tv: jnp.where(tv[i] > 0, jnp.minimum(s, MOE_NG - 1), MOE_NG - 1)
    di = lambda i, s, tv: jnp.where(tv[i] > 0, jnp.maximum(s - MOE_NG, 0), MOE_ND - 1)
    in_specs = [pl.BlockSpec((MOE_TM, D_MODEL), lambda i, s, te, tv: (i, 0)),
                pl.BlockSpec((1, D_MODEL, 4 * MOE_TF), lambda i, s, te, tv: (te[i], 0, gi(i, s, tv))),
                pl.BlockSpec((1, 1, 4 * MOE_TF), lambda i, s, te, tv: (te[i], 0, gi(i, s, tv))),
                pl.BlockSpec((1, D_FF, MOE_TN), lambda i, s, te, tv: (te[i], 0, di(i, s, tv))),
                pl.BlockSpec((1, 1, MOE_TN), lambda i, s, te, tv: (te[i], 0, di(i, s, tv)))]
    grid_spec = pltpu.PrefetchScalarGridSpec(
        num_scalar_prefetch=2,
        grid=(nt, MOE_NG + MOE_ND),
        in_specs=in_specs,
        out_specs=pl.BlockSpec((MOE_TM, MOE_TN), lambda i, s, te, tv: (i, jnp.maximum(s - MOE_NG, 0))),
        scratch_shapes=[pltpu.VMEM((MOE_TM, D_FF), BF16)],
    )
    return pl.pallas_call(
        _moe_kernel,
        grid_spec=grid_spec,
        out_shape=jax.ShapeDtypeStruct((p, D_MODEL), F32),
        compiler_params=_cparams(("arbitrary", "arbitrary")),
        name="moe_experts",
    )(tile_expert, tile_valid, xs, w_gu, b_gu, w_down, b_down)


def _route(top_i):
    n = top_i.shape[0]
    na = n * TOP_K
    flat_e = top_i.reshape(na)
    order = jnp.argsort(flat_e, stable=True).astype(jnp.int32)
    inv = jnp.argsort(order).astype(jnp.int32)
    counts = jnp.sum((flat_e[:, None] == jnp.arange(N_EXPERTS, dtype=jnp.int32)[None, :]).astype(jnp.int32),
                     axis=0)
    padded = ((counts + MOE_TM - 1) // MOE_TM) * MOE_TM
    pad_start = jnp.cumsum(padded) - padded
    start = jnp.cumsum(counts) - counts
    pos = (inv + (pad_start - start)[flat_e]).reshape(n, TOP_K)
    nt = na // MOE_TM + N_EXPERTS
    p_rows = nt * MOE_TM
    tile_start = jnp.arange(nt, dtype=jnp.int32) * MOE_TM
    pad_end = pad_start + padded
    n_used = (jnp.sum(padded) // MOE_TM).astype(jnp.int32)
    tile_expert = jnp.minimum(jnp.sum((tile_start[:, None] >= pad_end[None, :]).astype(jnp.int32), axis=1),
                              N_EXPERTS - 1)
    used = jnp.arange(nt) < n_used
    tile_expert = jnp.where(used, tile_expert, tile_expert[jnp.maximum(n_used - 1, 0)])
    tile_valid = jnp.where(used, jnp.clip(counts[tile_expert] - (tile_start - pad_start[tile_expert]), 0, MOE_TM), 0)
    row = jnp.arange(p_rows, dtype=jnp.int32)
    row_e = jnp.repeat(tile_expert, MOE_TM)
    rank = row - pad_start[row_e]
    src = jnp.clip(start[row_e] + rank, 0, na - 1)
    row_token = jnp.where(rank < counts[row_e], order[src] // TOP_K, 0)
    return row_token, pos, tile_expert.astype(jnp.int32), tile_valid.astype(jnp.int32)


def _pad_cols(a, width):
    return jnp.pad(a, ((0, 0), (0, width - a.shape[1])))


def _pack_cols(a):
    o_w = 3 * RWKV_WIDTH
    o_a = o_w + DECAY_LORA
    o_g = o_a + AAA_LORA
    fb = A_COLS
    return jnp.concatenate([
        a[:, 0:3 * RWKV_WIDTH],
        a[:, fb:fb + 4 * FOX_WIDTH],
        _pad_cols(a[:, o_g:A_COLS], 256),
        a[:, o_w:o_g],
        _pad_cols(a[:, fb + 4 * FOX_WIDTH:], LANES),
    ], axis=1)


def _unpack_a_cols(z):
    return jnp.concatenate([z[..., 0:3 * RWKV_WIDTH], z[..., C_WA:C_WA + 128], z[..., C_GL:C_GL + GATE_LORA]],
                           axis=-1)


def kernel(x_prompt, x_sample, mem_prompt, cache_fox_k, cache_fox_v, cache_fox_logf, state_rwkv, state_shift,
           cache_mem_k, cache_mem_v, norm_mix, w_in, rwkv_mu, rwkv_w0, rwkv_w2, rwkv_a0, rwkv_a2, rwkv_g2,
           rwkv_k_k, rwkv_k_a, rwkv_r_k, rwkv_lnx_w, rwkv_lnx_b, fox_b_f, fox_q_norm, fox_k_norm, w_out,
           norm_cross, norm_mem, w_xq, w_xk, w_xv, w_xo, xq_norm, xk_norm, norm_ffn, w_router, b_router,
           w_gu, b_gu, w_down, b_down):
    bp, sp, _ = x_prompt.shape
    bs, ts, _ = x_sample.shape
    past = cache_fox_k.shape[2]
    n_p, n_s = bp * sp, bs * ts
    l = 0

    w_in_p = _pack_cols(w_in[l]).astype(BF16)
    mu_p = _pack_cols(jnp.pad(rwkv_mu[l][None, :], ((0, 0), (0, w_in.shape[2] - A_COLS))))
    zero_lora = jnp.zeros((DECAY_LORA, RWKV_WIDTH), F32)
    prep = {
        "mu_m": mu_p[:, 0:3072], "mu_g": mu_p[:, C_GL:C_GL + 256], "mu_w": mu_p[:, C_WA:C_WA + 128],
        "w0": rwkv_w0[l][None], "a0": rwkv_a0[l][None],
        "w2": jnp.concatenate([rwkv_w2[l], zero_lora], axis=0).astype(BF16),
        "a2": jnp.concatenate([zero_lora, rwkv_a2[l]], axis=0).astype(BF16),
        "g2": jnp.pad(rwkv_g2[l], ((0, 256 - GATE_LORA), (0, 0))).astype(BF16),
        "k_k": rwkv_k_k[l][None], "k_a": rwkv_k_a[l][None], "r_k": rwkv_r_k[l].reshape(1, RWKV_WIDTH),
    }
    lnw, lnb = rwkv_lnx_w[l][None], rwkv_lnx_b[l][None]
    qg = jnp.tile(fox_q_norm[l], FOX_HEADS)[None]
    kg = jnp.tile(fox_k_norm[l], FOX_HEADS)[None]
    bf = _pad_cols(fox_b_f[l][None], LANES)
    w_out_b = w_out[l].astype(BF16)
    w_kv = jnp.concatenate([w_xk[l], w_xv[l]], axis=1).astype(BF16)
    pc = {
        "g_cross": norm_cross[l][None], "w_xq": w_xq[l].astype(BF16), "xq_gain": xq_norm[l][None],
        "w_xo": w_xo[l].astype(BF16), "g_ffn": norm_ffn[l][None],
        "w_router": _pad_cols(w_router[l], LANES),
        "b_router": jnp.concatenate([b_router[l], jnp.full((LANES - N_EXPERTS,), NEG_BIG, F32)])[None],
    }

    def mixers(x, nseq, t, first_shift, s0, tm_mm, tm_prep, tb, dim_major):
        z = _norm_mm(x, norm_mix[l][None], w_in_p, tm_mm, 768)
        z3 = z.reshape(nseq, t, Z_COLS)
        *steps, g, bonus = _rwkv_prep(z, first_shift, prep, tm_prep)
        steps = [a.reshape(nseq, t, HEAD_PAIRS, LANES) for a in steps]
        ya, s_fin = _rwkv_scan(steps, g.reshape(nseq, t, RWKV_WIDTH), bonus.reshape(nseq, t, RWKV_WIDTH),
                               _state_to_pairs(s0), lnw, lnb, tb)
        fox = _fox_proj(z, qg, kg, bf, tm_prep, nseq if dim_major else None)
        return z, z3, ya.reshape(nseq * t, RWKV_WIDTH), _pairs_to_state(s_fin), fox

    zero_shift = jnp.zeros((bp, 1, Z_COLS), F32)
    zero_state = jnp.zeros((bp, RWKV_HEADS, HEAD_DIM, HEAD_DIM), F32)
    xp = x_prompt.reshape(n_p, D_MODEL)
    xs = x_sample.reshape(n_s, D_MODEL)
    zp, zp3, ya_p, st_p, (qb_p, kb_p, lf_p, kt_p, vt_p, vtb_p, lft_p) = mixers(
        xp, bp, sp, zero_shift, zero_state, 1024, 256, SCAN_BLOCK, True)
    shift_s = _pack_cols(jnp.pad(state_shift[l].reshape(bs, A_COLS), ((0, 0), (0, w_in.shape[2] - A_COLS))))
    zs, zs3, ya_s, st_s, (qb_s, kb_s, lf_s, kf_s) = mixers(
        xs, bs, ts, shift_s.reshape(bs, 1, Z_COLS), state_rwkv[l], 256, ts, ts, False)
    lf_s = lf_s[:, :FOX_HEADS]

    fp = jnp.cumsum(lft_p, axis=2) * LOG2E
    fq_p, fk_p = _prefix_terms(fp, bp, sp)
    yb_p = _fox_prompt(qb_p, kb_p, fq_p, fk_p, vtb_p.reshape(bp, FOX_HEADS, HEAD_DIM, sp), zp, bp, sp,
                       min(1024, sp))

    fs = jnp.cumsum(jnp.concatenate([cache_fox_logf[l].astype(F32), lf_s.reshape(bs, ts, FOX_HEADS)], axis=1),
                    axis=1) * LOG2E
    fn_s = fs[:, past:].transpose(0, 2, 1)
    kc_t = cache_fox_k[l].transpose(0, 2, 3, 1)
    vc_t = cache_fox_v[l].transpose(0, 2, 3, 1)
    yb_s = _fox_sample(qb_s, kc_t, vc_t, kb_s, zs, fn_s.reshape(bs, FOX_HEADS * ts, 1),
                       fs[:, :past].transpose(0, 2, 1), fn_s, bs, ts, min(512, past))

    x1_p = _out_proj(xp, ya_p, yb_p, w_out_b, 512)
    x1_s = _out_proj(xs, ya_s, yb_s, w_out_b, 256)

    kv_p = _mem_kv(mem_prompt.reshape(bp * N_MEM, D_MODEL), norm_mem[l][None], w_kv, xk_norm[l][None], 256)
    mk_p = kv_p[:, :MEM_WIDTH].reshape(bp, N_MEM, MEM_WIDTH)
    mv_p = kv_p[:, MEM_WIDTH:].reshape(bp, N_MEM, MEM_WIDTH)
    x2_p, h_p, ti_p, tg_p = _cross_router(x1_p, mk_p.astype(BF16), mv_p.astype(BF16), pc, bp, 256)
    mk_s = cache_mem_k[l].reshape(bs, N_MEM, MEM_WIDTH).astype(BF16)
    mv_s = cache_mem_v[l].reshape(bs, N_MEM, MEM_WIDTH).astype(BF16)
    x2_s, h_s, ti_s, tg_s = _cross_router(x1_s, mk_s, mv_s, pc, bs, ts)

    h_all = jnp.concatenate([h_p, h_s], axis=0)
    top_i = jnp.concatenate([ti_p[:, :TOP_K], ti_s[:, :TOP_K]], axis=0)
    gates = jnp.concatenate([tg_p[:, :TOP_K], tg_s[:, :TOP_K]], axis=0)
    row_token, pos, tile_expert, tile_valid = _route(top_i)
    out_rows = _moe_experts(tile_expert, tile_valid, h_all[row_token], w_gu[l], b_gu[l][:, None, :], w_down[l],
                            b_down[l][:, None, :])
    def combine(x2, rows, gate):
        return x2 + sum(out_rows[rows[:, k]] * gate[:, k:k + 1] for k in range(TOP_K))

    y_p = combine(x2_p, pos[:n_p], gates[:n_p]).reshape(bp, sp, D_MODEL)
    y_s = combine(x2_s, pos[n_p:], gates[n_p:]).reshape(bs, ts, D_MODEL)

    to_cache = lambda a: a.reshape(1, bp, FOX_HEADS, HEAD_DIM, sp).transpose(0, 1, 4, 2, 3)
    pk, pv, plf = to_cache(kt_p), to_cache(vt_p), lft_p.transpose(0, 2, 1)[None]
    sk = kf_s.reshape(1, bs, ts, FOX_HEADS, HEAD_DIM)
    sv = zs3[..., C_FV:C_FV + FOX_WIDTH].reshape(1, bs, ts, FOX_HEADS, HEAD_DIM)
    slf = lf_s.reshape(1, bs, ts, FOX_HEADS)
    return (y_p, y_s, pk, pv, plf, st_p[None], _unpack_a_cols(zp3[:, -1:])[None],
            mk_p.reshape(1, bp, N_MEM, MEM_HEADS, MEM_HEAD_DIM), mv_p.reshape(1, bp, N_MEM, MEM_HEADS, MEM_HEAD_DIM),
            sk, sv, slf, st_s[None], _unpack_a_cols(zs3[:, -1:])[None])
```

```python
import functools

import jax
import jax.numpy as jnp
from jax import lax
from jax.experimental import pallas as pl
from jax.experimental.pallas import tpu as pltpu

F32 = jnp.float32
BF16 = jnp.bfloat16

D_MODEL = 2048
HEAD_DIM = 64
RWKV_WIDTH = 1024
FOX_WIDTH = 1024
RWKV_HEADS = 16
FOX_HEADS = 16
DECAY_LORA = 64
AAA_LORA = 64
GATE_LORA = 160
A_COLS = 3 * RWKV_WIDTH + DECAY_LORA + AAA_LORA + GATE_LORA
N_MEM = 256
MEM_HEADS = 4
MEM_HEAD_DIM = 128
MEM_WIDTH = 512
N_EXPERTS = 32
TOP_K = 4
D_FF = 2048
SWIGLU_LIMIT = 7.0
SWIGLU_ALPHA = 1.702
RMS_EPS = 1e-6
GN_EPS = 64e-5

LANES = 128
HEAD_PAIRS = 8

C_R, C_K, C_V = 0, 1024, 2048
C_FQ, C_FK, C_FV, C_FG = 3072, 4096, 5120, 6144
C_GL = 7168
C_WA = 7424
C_FL = 7552
Z_COLS = 7680

VMEM_LIMIT = 56 * 1024 * 1024
NEG_BIG = -1e30
LOG2E = 1.4426950408889634


def _cparams(sem):
    return pltpu.CompilerParams(dimension_semantics=sem, vmem_limit_bytes=VMEM_LIMIT)


def _block_ones():
    r = lax.broadcasted_iota(jnp.int32, (LANES, LANES), 0) // HEAD_DIM
    c = lax.broadcasted_iota(jnp.int32, (LANES, LANES), 1) // HEAD_DIM
    return (r == c).astype(BF16)


def _seg_sum64(x):
    ones = _block_ones()
    outs = []
    for s in range(x.shape[-1] // LANES):
        xs = x[:, s * LANES:(s + 1) * LANES]
        hi = xs.astype(BF16)
        lo = (xs - hi.astype(F32)).astype(BF16)
        outs.append(jnp.dot(hi, ones, preferred_element_type=F32)
                    + jnp.dot(lo, ones, preferred_element_type=F32))
    return outs[0] if len(outs) == 1 else jnp.concatenate(outs, axis=-1)


def _softplus(x):
    return jnp.maximum(x, 0.0) + jnp.log(1.0 + jnp.exp(-jnp.abs(x)))


def _sigmoid(x):
    return 1.0 / (1.0 + jnp.exp(-x))


def _norm_mm_kernel(x_ref, g_ref, w_ref, o_ref, xn_ref):
    @pl.when(pl.program_id(1) == 0)
    def _():
        x = x_ref[...]
        ms = jnp.mean(x * x, axis=-1, keepdims=True)
        xn_ref[...] = (x * lax.rsqrt(ms + RMS_EPS) * g_ref[...]).astype(BF16)

    o_ref[...] = jnp.dot(xn_ref[...], w_ref[...], preferred_element_type=F32)


def _norm_mm(x, g, w, tm, tn):
    m, k = x.shape
    n = w.shape[1]
    return pl.pallas_call(
        _norm_mm_kernel,
        grid=(m // tm, n // tn),
        in_specs=[pl.BlockSpec((tm, k), lambda i, j: (i, 0)),
                  pl.BlockSpec((1, k), lambda i, j: (0, 0)),
                  pl.BlockSpec((k, tn), lambda i, j: (0, j))],
        out_specs=pl.BlockSpec((tm, tn), lambda i, j: (i, j)),
        out_shape=jax.ShapeDtypeStruct((m, n), F32),
        scratch_shapes=[pltpu.VMEM((tm, k), BF16)],
        compiler_params=_cparams(("parallel", "arbitrary")),
        name="norm_mm",
    )(x, g, w)


def _rwkv_prep_kernel(zm_ref, zg_ref, zw_ref, pm_ref, pg_ref, pw_ref, fm_ref, fg_ref, fw_ref,
                      mum_ref, mug_ref, muw_ref, w0_ref, w2_ref, a0_ref, a2_ref, g2_ref,
                      kk_ref, ka_ref, rk_ref,
                      r_o, d_o, k_o, v_o, kk_o, b_o, g_o, bonus_o, *, per):
    seq_start = pl.program_id(0) % per == 0

    def shifted(z_ref, p_ref, f_ref, mu_ref):
        z = z_ref[...]
        rolled = pltpu.roll(z, 1, 0)
        row = lax.broadcasted_iota(jnp.int32, z.shape, 0)
        before = jnp.where(seq_start, f_ref[0], p_ref[7:8, :])
        prev = jnp.where(row == 0, before, rolled)
        return z + mu_ref[...] * (prev - z)

    zm = shifted(zm_ref, pm_ref, fm_ref, mum_ref)
    zg = shifted(zg_ref, pg_ref, fg_ref, mug_ref)
    zw = shifted(zw_ref, pw_ref, fw_ref, muw_ref)
    r = zm[:, C_R:C_R + RWKV_WIDTH]
    k = zm[:, C_K:C_K + RWKV_WIDTH]
    v = zm[:, C_V:C_V + RWKV_WIDTH]
    lw = jnp.dot(jnp.tanh(zw).astype(BF16), w2_ref[...], preferred_element_type=F32)
    la = jnp.dot(zw.astype(BF16), a2_ref[...], preferred_element_type=F32)
    g = jnp.dot(_sigmoid(zg).astype(BF16), g2_ref[...], preferred_element_type=F32)
    w_log = -_softplus(-(w0_ref[...] + lw)) - 0.5
    a = _sigmoid(a0_ref[...] + la)
    kk = k * kk_ref[...]
    kk = kk * lax.rsqrt(jnp.maximum(_seg_sum64(kk * kk), 1e-24))
    kh = k * (1.0 + (a - 1.0) * ka_ref[...])
    tiles = lambda x: x.reshape(x.shape[0], HEAD_PAIRS, LANES)
    r_o[...] = tiles(r)
    d_o[...] = tiles(jnp.exp(-jnp.exp(w_log)))
    k_o[...] = tiles(kh)
    v_o[...] = tiles(v)
    kk_o[...] = tiles(kk)
    b_o[...] = tiles(kk * a)
    g_o[...] = g
    bonus_o[...] = _seg_sum64(r * kh * rk_ref[...]) * v


def _rwkv_prep(z, first, pr, tm):
    n = z.shape[0]
    per = n // first.shape[0] // tm
    sub = 8
    row = lambda w, c: pl.BlockSpec((tm, w), lambda i, c=c: (i, c))
    prev = lambda w, c: pl.BlockSpec((sub, w), lambda i, c=c: (jnp.maximum(i * (tm // sub) - 1, 0), c))
    head = lambda w: pl.BlockSpec((1, 1, w), lambda i: (i // per, 0, 0))
    full = lambda a: pl.BlockSpec(a.shape, lambda i: (0,) * a.ndim)
    firsts = [first[..., 0:3072], first[..., C_GL:C_GL + 256], first[..., C_WA:C_WA + 128]]
    params = [pr["mu_m"], pr["mu_g"], pr["mu_w"], pr["w0"], pr["w2"], pr["a0"], pr["a2"], pr["g2"],
              pr["k_k"], pr["k_a"], pr["r_k"]]
    tiled = jax.ShapeDtypeStruct((n, HEAD_PAIRS, LANES), F32)
    flat = jax.ShapeDtypeStruct((n, RWKV_WIDTH), F32)
    return pl.pallas_call(
        functools.partial(_rwkv_prep_kernel, per=per),
        grid=(n // tm,),
        in_specs=[row(3072, 0), row(256, C_GL // 256), row(128, C_WA // 128),
                  prev(3072, 0), prev(256, C_GL // 256), prev(128, C_WA // 128),
                  head(3072), head(256), head(128)] + [full(a) for a in params],
        out_specs=[pl.BlockSpec((tm, HEAD_PAIRS, LANES), lambda i: (i, 0, 0))] * 6
        + [pl.BlockSpec((tm, RWKV_WIDTH), lambda i: (i, 0))] * 2,
        out_shape=[tiled] * 6 + [flat] * 2,
        compiler_params=_cparams(("parallel",)),
        name="rwkv_prep",
    )(z, z, z, z, z, z, *firsts, *params)


SEQ_PER_STEP = 2
SCAN_BLOCK = 128


def _rwkv_scan_kernel(r_ref, d_ref, k_ref, v_ref, kk_ref, b_ref, g_ref, bonus_ref, s0_ref,
                      lw_ref, lb_ref,
                      y_ref, sf_ref,
                      s_ref, oa_ref, *, tb):
    tblk = pl.program_id(1)

    @pl.when(tblk == 0)
    def _():
        s_ref[...] = s0_ref[...]

    oa_ref[...] = jnp.zeros_like(oa_ref)
    r2 = lax.broadcasted_iota(jnp.int32, (2 * LANES, 2 * LANES), 0) // HEAD_DIM
    c2 = lax.broadcasted_iota(jnp.int32, (2 * LANES, 2 * LANES), 1) // HEAD_DIM
    ones = (r2 == c2).astype(BF16)
    lane = lax.broadcasted_iota(jnp.int32, (HEAD_DIM, LANES), 1)
    sub = lax.broadcasted_iota(jnp.int32, (HEAD_DIM, LANES), 0)
    diag = (lane % HEAD_DIM) == sub
    tiles = [(s, hp) for s in range(SEQ_PER_STEP) for hp in range(HEAD_PAIRS)]

    def bcast(ref, s, t, hp):
        return jnp.broadcast_to(ref[s, t, hp:hp + 1, :], (HEAD_DIM, LANES))

    def seg_sums(parts):
        lhs = jnp.concatenate([jnp.concatenate(parts[j:j + 2], axis=1) for j in range(0, len(parts), 2)], axis=0)
        out = jnp.dot(lhs, ones, preferred_element_type=F32)
        return [out[(j // 2) * HEAD_DIM:(j // 2 + 1) * HEAD_DIM, (j % 2) * LANES:(j % 2 + 1) * LANES]
                for j in range(len(parts))]

    def emit_outputs(t_out, valid):
        half = pl.multiple_of((t_out // HEAD_DIM) * HEAD_DIM, HEAD_DIM)
        hit = jnp.logical_and((lane % HEAD_DIM) == (t_out % HEAD_DIM), valid)
        ob = seg_sums([(s_ref[s, hp] * bcast(r_ref, s, t_out, hp)).astype(BF16) for s, hp in tiles])
        for j, (s, hp) in enumerate(tiles):
            cur = oa_ref[s, hp, pl.ds(half, HEAD_DIM), :]
            oa_ref[s, hp, pl.ds(half, HEAD_DIM), :] = jnp.where(hit, ob[j], cur)

    def step(t, carry):
        emit_outputs(jnp.maximum(t - 1, 0), t > 0)
        skk = seg_sums([(s_ref[s, hp] * bcast(kk_ref, s, t, hp)).astype(BF16) for s, hp in tiles])
        vb = seg_sums([jnp.where(diag, bcast(v_ref, s, t, hp), 0.0).astype(BF16) for s, hp in tiles])
        for j, (s, hp) in enumerate(tiles):
            s_ref[s, hp] = (s_ref[s, hp] * bcast(d_ref, s, t, hp)
                            - skk[j] * bcast(b_ref, s, t, hp)
                            + vb[j] * bcast(k_ref, s, t, hp))
        return carry

    lax.fori_loop(0, tb, step, 0)
    emit_outputs(tb - 1, True)

    lane_t = lax.broadcasted_iota(jnp.int32, (HEAD_DIM, LANES), 1)
    low = lane_t < HEAD_DIM
    for s in range(SEQ_PER_STEP):
        slabs = []
        for hp in range(HEAD_PAIRS):
            mt = oa_ref[s, hp].T
            ro = pltpu.roll(mt, HEAD_DIM, 1)
            top = jnp.where(low, mt[:HEAD_DIM], ro[HEAD_DIM:])
            bot = jnp.where(low, ro[:HEAD_DIM], mt[HEAD_DIM:])
            slabs.append(jnp.concatenate([top, bot], axis=0)[:tb])
        o = jnp.concatenate(slabs, axis=-1)
        mean = _seg_sum64(o) * (1.0 / HEAD_DIM)
        cen = o - mean
        var = _seg_sum64(cen * cen) * (1.0 / HEAD_DIM)
        on = cen * lax.rsqrt(var + GN_EPS) * lw_ref[...] + lb_ref[...]
        y_ref[s] = ((on + bonus_ref[s]) * g_ref[s]).astype(y_ref.dtype)

    @pl.when(tblk == pl.num_programs(1) - 1)
    def _():
        sf_ref[...] = s_ref[...]


def _rwkv_scan(steps, g, bonus, s0, lnw, lnb, tb):
    nseq, t = g.shape[:2]
    step_spec = pl.BlockSpec((SEQ_PER_STEP, tb, HEAD_PAIRS, LANES), lambda i, j: (i, j, 0, 0))
    seq_spec = pl.BlockSpec((SEQ_PER_STEP, tb, RWKV_WIDTH), lambda i, j: (i, j, 0))
    st_spec = pl.BlockSpec((SEQ_PER_STEP, HEAD_PAIRS, HEAD_DIM, LANES), lambda i, j: (i, 0, 0, 0))
    par_spec = pl.BlockSpec((1, RWKV_WIDTH), lambda i, j: (0, 0))
    return pl.pallas_call(
        functools.partial(_rwkv_scan_kernel, tb=tb),
        grid=(nseq // SEQ_PER_STEP, t // tb),
        in_specs=[step_spec] * 6 + [seq_spec] * 2 + [st_spec] + [par_spec] * 2,
        out_specs=[seq_spec, st_spec],
        out_shape=[jax.ShapeDtypeStruct((nseq, t, RWKV_WIDTH), BF16),
                   jax.ShapeDtypeStruct(s0.shape, F32)],
        scratch_shapes=[pltpu.VMEM((SEQ_PER_STEP, HEAD_PAIRS, HEAD_DIM, LANES), F32),
                        pltpu.VMEM((SEQ_PER_STEP, HEAD_PAIRS, LANES, LANES), F32)],
        compiler_params=_cparams(("parallel", "arbitrary")),
        name="rwkv_scan",
    )(*steps, g, bonus, s0, lnw, lnb)


def _state_to_pairs(s):
    b = s.shape[0]
    return s.reshape(b, HEAD_PAIRS, 2, HEAD_DIM, HEAD_DIM).transpose(0, 1, 3, 2, 4).reshape(
        b, HEAD_PAIRS, HEAD_DIM, LANES)


def _pairs_to_state(s):
    b = s.shape[0]
    return s.reshape(b, HEAD_PAIRS, HEAD_DIM, 2, HEAD_DIM).transpose(0, 1, 3, 2, 4).reshape(
        b, RWKV_HEADS, HEAD_DIM, HEAD_DIM)


def _fox_proj_kernel(q_ref, k_ref, v_ref, fl_ref, qg_ref, kg_ref, bf_ref, qb_o, kb_o, lf_o, *cache_o, dim_major):
    def headnorm(x, g):
        ms = _seg_sum64(x * x) * (1.0 / HEAD_DIM)
        return x * lax.rsqrt(ms + RMS_EPS) * g

    q = headnorm(q_ref[...], qg_ref[...])
    k = headnorm(k_ref[...], kg_ref[...])
    q = q * (HEAD_DIM ** -0.5 * LOG2E)
    lf = -_softplus(-(fl_ref[...] + bf_ref[...]))
    lf_o[...] = lf
    if dim_major:
        lane = lax.broadcasted_iota(jnp.int32, (q.shape[0], LANES), 1)
        for h in range(FOX_HEADS):
            slab = slice((h // 2) * LANES, (h // 2 + 1) * LANES)
            for x, o in ((q, qb_o), (k, kb_o)):
                xs = x[:, slab] if h % 2 == 0 else pltpu.roll(x[:, slab], HEAD_DIM, 1)
                o[0, h] = jnp.where(lane < HEAD_DIM, xs, 0.0).astype(BF16)
        kt_o, vt_o, vtb_o, lft_o = cache_o
        vt = v_ref[...].T
        kt_o[0] = k.T
        vt_o[0] = vt
        vtb_o[0] = vt.astype(BF16)
        lft_o[0] = lf.T[:FOX_HEADS]
    else:
        qb_o[...] = q.astype(BF16)
        kb_o[...] = k.astype(BF16)
        cache_o[0][...] = k


def _fox_proj(z, qg, kg, bf, tm, nseq=None):
    n = z.shape[0]
    col = lambda w, c: pl.BlockSpec((tm, w), lambda i, c=c: (i, c))
    par = lambda w: pl.BlockSpec((1, w), lambda i: (0, 0))
    wide = pl.BlockSpec((tm, FOX_WIDTH), lambda i: (i, 0))
    lf_spec = pl.BlockSpec((tm, LANES), lambda i: (i, 0))
    lf_shape = jax.ShapeDtypeStruct((n, LANES), F32)
    if nseq is None:
        out_specs = [wide, wide, lf_spec, wide]
        out_shape = [jax.ShapeDtypeStruct((n, FOX_WIDTH), BF16), jax.ShapeDtypeStruct((n, FOX_WIDTH), BF16),
                     lf_shape, jax.ShapeDtypeStruct((n, FOX_WIDTH), F32)]
    else:
        t = n // nseq
        per = t // tm
        heads = pl.BlockSpec((1, FOX_HEADS, tm, LANES), lambda i: (i // per, 0, i % per, 0))
        tall = pl.BlockSpec((1, FOX_WIDTH, tm), lambda i: (i // per, 0, i % per))
        padded = jax.ShapeDtypeStruct((nseq, FOX_HEADS, t, LANES), BF16)
        out_specs = [heads, heads, lf_spec, tall, tall, tall,
                     pl.BlockSpec((1, FOX_HEADS, tm), lambda i: (i // per, 0, i % per))]
        out_shape = [padded, padded, lf_shape,
                     jax.ShapeDtypeStruct((nseq, FOX_WIDTH, t), F32), jax.ShapeDtypeStruct((nseq, FOX_WIDTH, t), F32),
                     jax.ShapeDtypeStruct((nseq, FOX_WIDTH, t), BF16),
                     jax.ShapeDtypeStruct((nseq, FOX_HEADS, t), F32)]
    return pl.pallas_call(
        functools.partial(_fox_proj_kernel, dim_major=nseq is not None),
        grid=(n // tm,),
        in_specs=[col(1024, C_FQ // 1024), col(1024, C_FK // 1024), col(1024, C_FV // 1024), col(128, C_FL // 128),
                  par(1024), par(1024), par(128)],
        out_specs=out_specs,
        out_shape=out_shape,
        compiler_params=_cparams(("parallel",)),
        name="fox_proj",
    )(z, z, z, z, qg, kg, bf)


F_TERMS = 8


def _fox_prompt_kernel(qi_ref, kj_ref, q_ref, k_ref, fq_ref, fk_ref, vt_ref, gate_ref, y_ref,
                       qa_ref, m_ref, l_ref, acc_ref, *, tq):
    pair = pl.program_id(2)
    qi = qi_ref[pair]
    kj = kj_ref[pair]
    lane = lax.broadcasted_iota(jnp.int32, (tq, LANES), 1)
    zero = jnp.zeros((tq, LANES), BF16)

    def with_terms(x, f, h):
        lo = HEAD_DIM + F_TERMS * h
        return x + jnp.where(jnp.logical_and(lane >= lo, lane < lo + F_TERMS), f, zero)

    @pl.when(kj == 0)
    def _():
        m_ref[...] = jnp.full_like(m_ref, NEG_BIG)
        l_ref[...] = jnp.zeros_like(l_ref)
        acc_ref[...] = jnp.zeros_like(acc_ref)
        for h in range(2):
            qa_ref[h] = with_terms(q_ref[0, h], fq_ref[0, 0], h)

    def body(masked):
        for h in range(2):
            st = lax.dot_general(with_terms(k_ref[0, h], fk_ref[0, 0], h), qa_ref[h], (((1,), (1,)), ((), ())),
                                 preferred_element_type=F32)
            if masked:
                key = lax.broadcasted_iota(jnp.int32, st.shape, 0)
                qry = lax.broadcasted_iota(jnp.int32, st.shape, 1)
                st = jnp.where(key <= qry, st, NEG_BIG)
            m_prev = m_ref[h]
            m_new = jnp.maximum(m_prev, jnp.max(st, axis=0, keepdims=True))
            alpha = jnp.exp2(m_prev - m_new)
            p = jnp.exp2(st - m_new)
            l_ref[h] = alpha * l_ref[h] + jnp.sum(p, axis=0, keepdims=True)
            acc_ref[h] = alpha * acc_ref[h] + jnp.dot(vt_ref[0, h], p.astype(BF16), preferred_element_type=F32)
            m_ref[h] = m_new

    @pl.when(kj < qi)
    def _():
        body(False)

    @pl.when(kj == qi)
    def _():
        body(True)
        ot = jnp.concatenate([acc_ref[0] / l_ref[0], acc_ref[1] / l_ref[1]], axis=0)
        y_ref[...] = (ot.T * _sigmoid(gate_ref[...])).astype(y_ref.dtype)


def _fox_prompt(q_pad, k_pad, fq, fk, vt, z, nb, seq, tq):
    nq = seq // tq
    pairs = [(i, j) for i in range(nq) for j in range(i + 1)]
    qi_tab = jnp.asarray([p[0] for p in pairs], jnp.int32)
    kj_tab = jnp.asarray([p[1] for p in pairs], jnp.int32)
    grid_spec = pltpu.PrefetchScalarGridSpec(
        num_scalar_prefetch=2,
        grid=(nb, HEAD_PAIRS, len(pairs)),
        in_specs=[pl.BlockSpec((1, 2, tq, LANES), lambda b, hp, p, qi, kj: (b, hp, qi[p], 0)),
                  pl.BlockSpec((1, 2, tq, LANES), lambda b, hp, p, qi, kj: (b, hp, kj[p], 0)),
                  pl.BlockSpec((1, 1, tq, LANES), lambda b, hp, p, qi, kj: (b, hp, qi[p], 0)),
                  pl.BlockSpec((1, 1, tq, LANES), lambda b, hp, p, qi, kj: (b, hp, kj[p], 0)),
                  pl.BlockSpec((1, 2, HEAD_DIM, tq), lambda b, hp, p, qi, kj: (b, hp, 0, kj[p])),
                  pl.BlockSpec((tq, LANES), lambda b, hp, p, qi, kj: (b * nq + qi[p], C_FG // LANES + hp))],
        out_specs=pl.BlockSpec((tq, LANES), lambda b, hp, p, qi, kj: (b * nq + qi[p], hp)),
        scratch_shapes=[pltpu.VMEM((2, tq, LANES), BF16), pltpu.VMEM((2, 1, tq), F32), pltpu.VMEM((2, 1, tq), F32),
                        pltpu.VMEM((2, HEAD_DIM, tq), F32)],
    )
    return pl.pallas_call(
        functools.partial(_fox_prompt_kernel, tq=tq),
        grid_spec=grid_spec,
        out_shape=jax.ShapeDtypeStruct((nb * seq, FOX_WIDTH), BF16),
        compiler_params=_cparams(("parallel", "parallel", "arbitrary")),
        name="fox_prompt",
    )(qi_tab, kj_tab, q_pad, k_pad, fq, fk, vt, z)


def _split3(x):
    def top(v):
        bits = lax.bitcast_convert_type(v, jnp.uint32) & jnp.uint32(0xFFFF0000)
        return lax.bitcast_convert_type(bits, F32)

    hi = top(x)
    mid = top(x - hi)
    lo = top(x - hi - mid)
    return hi.astype(BF16), mid.astype(BF16), lo.astype(BF16)


def _prefix_terms(f2, nb, seq):
    one = jnp.ones_like(f2, BF16)
    none = jnp.zeros_like(f2, BF16)

    def lanes(terms):
        t = jnp.stack([*terms, none, none], axis=-1)
        t = t.reshape(nb, HEAD_PAIRS, 2, seq, F_TERMS).transpose(0, 1, 3, 2, 4).reshape(nb, HEAD_PAIRS, seq, 16)
        return jnp.pad(t, ((0, 0), (0, 0), (0, 0), (HEAD_DIM, LANES - HEAD_DIM - 16)))

    return lanes([*_split3(f2), one, one, one]), lanes([one, one, one, *_split3(-f2)])


def _fox_sample_kernel(q_ref, kc_ref, vc_ref, kn_ref, vn_ref, fq_ref, fkc_ref, fkn_ref, gate_ref, y_ref,
                       m_ref, l_ref, acc_ref, *, nk):
    kj = pl.program_id(1)

    @pl.when(kj == 0)
    def _():
        m_ref[...] = jnp.full_like(m_ref, NEG_BIG)
        l_ref[...] = jnp.zeros_like(l_ref)
        acc_ref[...] = jnp.zeros_like(acc_ref)

    def head_slice(ref, h):
        return ref[:, h * HEAD_DIM:(h + 1) * HEAD_DIM]

    t = q_ref.shape[0]

    nt_dims = (((1,), (1,)), ((), ()))

    def attend(ks, vs, fk_ref, masked, dim_major):
        if dim_major:
            qk = lambda h: jnp.dot(head_slice(q_ref, h), ks[h], preferred_element_type=F32)
            pv_of = lambda ph, h: lax.dot_general(ph, vs[h], nt_dims, preferred_element_type=F32)
        else:
            qk = lambda h: lax.dot_general(head_slice(q_ref, h), ks[h], nt_dims, preferred_element_type=F32)
            pv_of = lambda ph, h: jnp.dot(ph, vs[h], preferred_element_type=F32)
        s = jnp.concatenate([qk(h) for h in range(FOX_HEADS)], axis=0)
        fk = jnp.concatenate([jnp.broadcast_to(fk_ref[0, h:h + 1, :], (t, s.shape[1])) for h in range(FOX_HEADS)],
                             axis=0)
        s = s + fq_ref[0] - fk
        if masked:
            qry = lax.broadcasted_iota(jnp.int32, s.shape, 0) % t
            key = lax.broadcasted_iota(jnp.int32, s.shape, 1)
            s = jnp.where(key <= qry, s, NEG_BIG)
        m_prev = m_ref[...]
        m_new = jnp.maximum(m_prev, jnp.max(s, axis=-1, keepdims=True))
        alpha = jnp.exp2(m_prev - m_new)
        p = jnp.exp2(s - m_new)
        l_ref[...] = alpha * l_ref[...] + jnp.sum(p, axis=-1, keepdims=True)
        p = p.astype(BF16)
        pv = jnp.concatenate([pv_of(p[h * t:(h + 1) * t], h) for h in range(FOX_HEADS)], axis=0)
        acc_ref[...] = alpha * acc_ref[...] + pv
        m_ref[...] = m_new

    @pl.when(kj < nk)
    def _():
        attend([kc_ref[0, h].astype(BF16) for h in range(FOX_HEADS)],
               [vc_ref[0, h].astype(BF16) for h in range(FOX_HEADS)], fkc_ref, False, True)

    @pl.when(kj == nk)
    def _():
        attend([head_slice(kn_ref, h) for h in range(FOX_HEADS)],
               [head_slice(vn_ref, h).astype(BF16) for h in range(FOX_HEADS)], fkn_ref, True, False)
        o = acc_ref[...] / l_ref[...]
        for h in range(FOX_HEADS):
            gate = _sigmoid(head_slice(gate_ref, h))
            y_ref[:, h * HEAD_DIM:(h + 1) * HEAD_DIM] = (o[h * t:(h + 1) * t] * gate).astype(y_ref.dtype)


def _fox_sample(qb, kc, vc, kb, z, fq, fkc, fkn, nb, t, tk):
    past = kc.shape[3]
    nk = past // tk
    last = nk - 1
    wide = lambda c: pl.BlockSpec((t, FOX_WIDTH), lambda b, j, c=c: (b, c))
    cache = pl.BlockSpec((1, FOX_HEADS, HEAD_DIM, tk), lambda b, j: (b, 0, 0, jnp.minimum(j, last)))
    return pl.pallas_call(
        functools.partial(_fox_sample_kernel, nk=nk),
        grid=(nb, nk + 1),
        in_specs=[wide(0), cache, cache, wide(0), wide(C_FV // FOX_WIDTH),
                  pl.BlockSpec((1, FOX_HEADS * t, 1), lambda b, j: (b, 0, 0)),
                  pl.BlockSpec((1, FOX_HEADS, tk), lambda b, j: (b, 0, jnp.minimum(j, last))),
                  pl.BlockSpec((1, FOX_HEADS, t), lambda b, j: (b, 0, 0)),
                  wide(C_FG // FOX_WIDTH)],
        out_specs=wide(0),
        out_shape=jax.ShapeDtypeStruct((nb * t, FOX_WIDTH), BF16),
        scratch_shapes=[pltpu.VMEM((FOX_HEADS * t, 1), F32), pltpu.VMEM((FOX_HEADS * t, 1), F32),
                        pltpu.VMEM((FOX_HEADS * t, HEAD_DIM), F32)],
        compiler_params=_cparams(("parallel", "arbitrary")),
        name="fox_sample",
    )(qb, kc, vc, kb, z, fq, fkc, fkn, z)


def _mem_kv_kernel(x_ref, g_ref, w_ref, kg_ref, o_ref):
    x = x_ref[...]
    ms = jnp.mean(x * x, axis=-1, keepdims=True)
    xn = (x * lax.rsqrt(ms + RMS_EPS) * g_ref[...]).astype(BF16)
    kv = jnp.dot(xn, w_ref[...], preferred_element_type=F32)
    for h in range(MEM_HEADS):
        kh = kv[:, h * MEM_HEAD_DIM:(h + 1) * MEM_HEAD_DIM]
        ms = jnp.mean(kh * kh, axis=-1, keepdims=True)
        o_ref[:, h * MEM_HEAD_DIM:(h + 1) * MEM_HEAD_DIM] = kh * lax.rsqrt(ms + RMS_EPS) * kg_ref[...]
    o_ref[:, MEM_WIDTH:] = kv[:, MEM_WIDTH:]


def _mem_kv(mem, g, w_kv, kg, tm):
    n = mem.shape[0]
    return pl.pallas_call(
        _mem_kv_kernel,
        grid=(n // tm,),
        in_specs=[pl.BlockSpec((tm, D_MODEL), lambda i: (i, 0)),
                  pl.BlockSpec((1, D_MODEL), lambda i: (0, 0)),
                  pl.BlockSpec((D_MODEL, 2 * MEM_WIDTH), lambda i: (0, 0)),
                  pl.BlockSpec((1, MEM_HEAD_DIM), lambda i: (0, 0))],
        out_specs=pl.BlockSpec((tm, 2 * MEM_WIDTH), lambda i: (i, 0)),
        out_shape=jax.ShapeDtypeStruct((n, 2 * MEM_WIDTH), F32),
        compiler_params=_cparams(("parallel",)),
        name="mem_kv",
    )(mem, g, w_kv, kg)


def _cross_router_kernel(x_ref, ya_ref, yb_ref, wa_ref, wb_ref, gc_ref, wq_ref, qg_ref, mk_ref, mv_ref, wo_ref,
                         gf_ref, wr_ref, br_ref,
                         x2_ref, h_ref, idx_ref, gate_ref):
    x = (x_ref[...]
         + jnp.dot(ya_ref[...], wa_ref[...], preferred_element_type=F32)
         + jnp.dot(yb_ref[...], wb_ref[...], preferred_element_type=F32))
    ms = jnp.mean(x * x, axis=-1, keepdims=True)
    xn = (x * lax.rsqrt(ms + RMS_EPS) * gc_ref[...]).astype(BF16)
    q = jnp.dot(xn, wq_ref[...], preferred_element_type=F32)
    outs = []
    for h in range(MEM_HEADS):
        sl = slice(h * MEM_HEAD_DIM, (h + 1) * MEM_HEAD_DIM)
        qh = q[:, sl]
        qms = jnp.mean(qh * qh, axis=-1, keepdims=True)
        qh = (qh * lax.rsqrt(qms + RMS_EPS) * qg_ref[...]).astype(BF16)
        s = lax.dot_general(qh, mk_ref[0, :, sl], (((1,), (1,)), ((), ())), preferred_element_type=F32)
        s = s * (MEM_HEAD_DIM ** -0.5)
        p = jnp.exp(s - jnp.max(s, axis=-1, keepdims=True))
        p = p / jnp.sum(p, axis=-1, keepdims=True)
        outs.append(jnp.dot(p.astype(BF16), mv_ref[0, :, sl], preferred_element_type=F32))
    o = jnp.concatenate(outs, axis=-1).astype(BF16)
    x2 = x + jnp.dot(o, wo_ref[...], preferred_element_type=F32)
    x2_ref[...] = x2

    ms2 = jnp.mean(x2 * x2, axis=-1, keepdims=True)
    hn = x2 * lax.rsqrt(ms2 + RMS_EPS) * gf_ref[...]
    h_ref[...] = hn.astype(BF16)
    logits = jnp.dot(hn, wr_ref[...], preferred_element_type=F32, precision=lax.Precision.HIGHEST)
    logits = logits + br_ref[...]
    lane = lax.broadcasted_iota(jnp.int32, logits.shape, 1)
    idx_acc = jnp.zeros(logits.shape, jnp.int32)
    val_acc = jnp.zeros(logits.shape, F32)
    top = None
    for kk in range(TOP_K):
        m = jnp.max(logits, axis=-1, keepdims=True)
        sel = jnp.min(jnp.where(logits == m, lane, LANES), axis=-1, keepdims=True)
        if kk == 0:
            top = m
        idx_acc = jnp.where(lane == kk, sel, idx_acc)
        val_acc = jnp.where(lane == kk, jnp.exp(m - top), val_acc)
        logits = jnp.where(lane == sel, -jnp.inf, logits)
    idx_ref[...] = idx_acc
    gate_ref[...] = val_acc / jnp.sum(val_acc, axis=-1, keepdims=True)


def _cross_router(x, ya, yb, w_out, mk, mv, pc, nb, tm):
    n = x.shape[0]
    half = lambda r: pl.BlockSpec((RWKV_WIDTH, D_MODEL), lambda i, r=r: (r, 0))
    per = n // nb // tm
    full = lambda a: pl.BlockSpec(a.shape, lambda i: (0,) * a.ndim)
    mem = pl.BlockSpec((1, N_MEM, MEM_WIDTH), lambda i: (i // per, 0, 0))
    row = lambda w: pl.BlockSpec((tm, w), lambda i: (i, 0))
    params1 = [pc["g_cross"], pc["w_xq"], pc["xq_gain"]]
    params2 = [pc["w_xo"], pc["g_ffn"], pc["w_router"], pc["b_router"]]
    return pl.pallas_call(
        _cross_router_kernel,
        grid=(n // tm,),
        in_specs=[row(D_MODEL), row(RWKV_WIDTH), row(FOX_WIDTH), half(0), half(1)]
        + [full(a) for a in params1] + [mem, mem] + [full(a) for a in params2],
        out_specs=[row(D_MODEL), row(D_MODEL), row(LANES), row(LANES)],
        out_shape=[jax.ShapeDtypeStruct((n, D_MODEL), F32), jax.ShapeDtypeStruct((n, D_MODEL), BF16),
                   jax.ShapeDtypeStruct((n, LANES), jnp.int32), jax.ShapeDtypeStruct((n, LANES), F32)],
        compiler_params=_cparams(("parallel",)),
        name="cross_router",
    )(x, ya, yb, w_out, w_out, *params1, mk, mv, *params2)


MOE_TM = 768
MOE_TF = 256
MOE_TN = 1024
MOE_NG = D_FF // (2 * MOE_TF)
MOE_ND = D_MODEL // MOE_TN


def _moe_kernel(te_ref, tv_ref, xs_ref, wgu_ref, bgu_ref, wd_ref, bd_ref, o_ref, act_ref):
    s = pl.program_id(1)
    used = tv_ref[pl.program_id(0)] > 0
    half = 2 * MOE_TF

    @pl.when(jnp.logical_and(used, s < MOE_NG))
    def _():
        gu = jnp.dot(xs_ref[...], wgu_ref[0].astype(BF16), preferred_element_type=F32) + bgu_ref[0]
        glu = jnp.minimum(gu, SWIGLU_LIMIT)
        fglu = glu * _sigmoid(SWIGLU_ALPHA * glu)
        lin = jnp.clip(gu, -SWIGLU_LIMIT, SWIGLU_LIMIT) + 1.0
        act_a = pltpu.roll(fglu[:, :half], 1, 1) * lin[:, :half]
        act_b = fglu[:, half:] * pltpu.roll(lin[:, half:], half - 1, 1)
        lane = lax.broadcasted_iota(jnp.int32, act_a.shape, 1)
        merged = jnp.where(lane % 2 == 0, act_b, act_a).astype(BF16)
        for c in range(MOE_NG):
            @pl.when(s == c)
            def _(c=c):
                act_ref[:, c * half:(c + 1) * half] = merged

    @pl.when(jnp.logical_and(used, s >= MOE_NG))
    def _():
        top = lambda w: lax.bitcast_convert_type(w.astype(BF16).astype(F32), jnp.uint32)
        chunks = []
        for c in range(MOE_NG):
            a = wd_ref[0, c * half:c * half + MOE_TF, :]
            b = wd_ref[0, c * half + MOE_TF:(c + 1) * half, :]
            chunks.append(pltpu.bitcast((top(b) >> 16) | top(a), BF16))
        wd = jnp.concatenate(chunks, axis=0)
        o_ref[...] = jnp.dot(act_ref[...], wd, preferred_element_type=F32) + bd_ref[0]

    @pl.when(jnp.logical_and(jnp.logical_not(used), s >= MOE_NG))
    def _():
        o_ref[...] = jnp.zeros_like(o_ref)


def _moe_experts(tile_expert, tile_valid, xs, w_gu, b_gu, w_down, b_down):
    p = xs.shape[0]
    nt = p // MOE_TM
    gi = lambda i, s, tv: jnp.where(tv[i] > 0, jnp.minimum(s, MOE_NG - 1), MOE_NG - 1)
    di = lambda i, s, tv: jnp.where(tv[i] > 0, jnp.maximum(s - MOE_NG, 0), MOE_ND - 1)
    in_specs = [pl.BlockSpec((MOE_TM, D_MODEL), lambda i, s, te, tv: (i, 0)),
                pl.BlockSpec((1, D_MODEL, 4 * MOE_TF), lambda i, s, te, tv: (te[i], 0, gi(i, s, tv))),
                pl.BlockSpec((1, 1, 4 * MOE_TF), lambda i, s, te, tv: (te[i], 0, gi(i, s, tv))),
                pl.BlockSpec((1, D_FF, MOE_TN), lambda i, s, te, tv: (te[i], 0, di(i, s, tv))),
                pl.BlockSpec((1, 1, MOE_TN), lambda i, s, te, tv: (te[i], 0, di(i, s, tv)))]
    grid_spec = pltpu.PrefetchScalarGridSpec(
        num_scalar_prefetch=2,
        grid=(nt, MOE_NG + MOE_ND),
        in_specs=in_specs,
        out_specs=pl.BlockSpec((MOE_TM, MOE_TN), lambda i, s, te, tv: (i, jnp.maximum(s - MOE_NG, 0))),
        scratch_shapes=[pltpu.VMEM((MOE_TM, D_FF), BF16)],
    )
    return pl.pallas_call(
        _moe_kernel,
        grid_spec=grid_spec,
        out_shape=jax.ShapeDtypeStruct((p, D_MODEL), F32),
        compiler_params=_cparams(("arbitrary", "arbitrary")),
        name="moe_experts",
    )(tile_expert, tile_valid, xs, w_gu, b_gu, w_down, b_down)


def _route(top_i):
    n = top_i.shape[0]
    na = n * TOP_K
    flat_e = top_i.reshape(na)
    order = jnp.argsort(flat_e, stable=True).astype(jnp.int32)
    inv = jnp.argsort(order).astype(jnp.int32)
    counts = jnp.sum((flat_e[:, None] == jnp.arange(N_EXPERTS, dtype=jnp.int32)[None, :]).astype(jnp.int32),
                     axis=0)
    padded = ((counts + MOE_TM - 1) // MOE_TM) * MOE_TM
    pad_start = jnp.cumsum(padded) - padded
    start = jnp.cumsum(counts) - counts
    pos = (inv + (pad_start - start)[flat_e]).reshape(n, TOP_K)
    nt = na // MOE_TM + N_EXPERTS
    p_rows = nt * MOE_TM
    tile_start = jnp.arange(nt, dtype=jnp.int32) * MOE_TM
    pad_end = pad_start + padded
    n_used = (jnp.sum(padded) // MOE_TM).astype(jnp.int32)
    tile_expert = jnp.minimum(jnp.sum((tile_start[:, None] >= pad_end[None, :]).astype(jnp.int32), axis=1),
                              N_EXPERTS - 1)
    used = jnp.arange(nt) < n_used
    tile_expert = jnp.where(used, tile_expert, tile_expert[jnp.maximum(n_used - 1, 0)])
    tile_valid = jnp.where(used, jnp.clip(counts[tile_expert] - (tile_start - pad_start[tile_expert]), 0, MOE_TM), 0)
    row = jnp.arange(p_rows, dtype=jnp.int32)
    row_e = jnp.repeat(tile_expert, MOE_TM)
    rank = row - pad_start[row_e]
    src = jnp.clip(start[row_e] + rank, 0, na - 1)
    row_token = jnp.where(rank < counts[row_e], order[src] // TOP_K, 0)
    return row_token, pos, tile_expert.astype(jnp.int32), tile_valid.astype(jnp.int32)


def _pad_cols(a, width):
    return jnp.pad(a, ((0, 0), (0, width - a.shape[1])))


def _pack_cols(a):
    o_w = 3 * RWKV_WIDTH
    o_a = o_w + DECAY_LORA
    o_g = o_a + AAA_LORA
    fb = A_COLS
    return jnp.concatenate([
        a[:, 0:3 * RWKV_WIDTH],
        a[:, fb:fb + 4 * FOX_WIDTH],
        _pad_cols(a[:, o_g:A_COLS], 256),
        a[:, o_w:o_g],
        _pad_cols(a[:, fb + 4 * FOX_WIDTH:], LANES),
    ], axis=1)


def _unpack_a_cols(z):
    return jnp.concatenate([z[..., 0:3 * RWKV_WIDTH], z[..., C_WA:C_WA + 128], z[..., C_GL:C_GL + GATE_LORA]],
                           axis=-1)


def kernel(x_prompt, x_sample, mem_prompt, cache_fox_k, cache_fox_v, cache_fox_logf, state_rwkv, state_shift,
           cache_mem_k, cache_mem_v, norm_mix, w_in, rwkv_mu, rwkv_w0, rwkv_w2, rwkv_a0, rwkv_a2, rwkv_g2,
           rwkv_k_k, rwkv_k_a, rwkv_r_k, rwkv_lnx_w, rwkv_lnx_b, fox_b_f, fox_q_norm, fox_k_norm, w_out,
           norm_cross, norm_mem, w_xq, w_xk, w_xv, w_xo, xq_norm, xk_norm, norm_ffn, w_router, b_router,
           w_gu, b_gu, w_down, b_down):
    bp, sp, _ = x_prompt.shape
    bs, ts, _ = x_sample.shape
    past = cache_fox_k.shape[2]
    n_p, n_s = bp * sp, bs * ts
    l = 0

    w_in_p = _pack_cols(w_in[l]).astype(BF16)
    mu_p = _pack_cols(jnp.pad(rwkv_mu[l][None, :], ((0, 0), (0, w_in.shape[2] - A_COLS))))
    zero_lora = jnp.zeros((DECAY_LORA, RWKV_WIDTH), F32)
    prep = {
        "mu_m": mu_p[:, 0:3072], "mu_g": mu_p[:, C_GL:C_GL + 256], "mu_w": mu_p[:, C_WA:C_WA + 128],
        "w0": rwkv_w0[l][None], "a0": rwkv_a0[l][None],
        "w2": jnp.concatenate([rwkv_w2[l], zero_lora], axis=0).astype(BF16),
        "a2": jnp.concatenate([zero_lora, rwkv_a2[l]], axis=0).astype(BF16),
        "g2": jnp.pad(rwkv_g2[l], ((0, 256 - GATE_LORA), (0, 0))).astype(BF16),
        "k_k": rwkv_k_k[l][None], "k_a": rwkv_k_a[l][None], "r_k": rwkv_r_k[l].reshape(1, RWKV_WIDTH),
    }
    lnw, lnb = rwkv_lnx_w[l][None], rwkv_lnx_b[l][None]
    qg = jnp.tile(fox_q_norm[l], FOX_HEADS)[None]
    kg = jnp.tile(fox_k_norm[l], FOX_HEADS)[None]
    bf = _pad_cols(fox_b_f[l][None], LANES)
    w_out_b = w_out[l].astype(BF16)
    w_kv = jnp.concatenate([w_xk[l], w_xv[l]], axis=1).astype(BF16)
    pc = {
        "g_cross": norm_cross[l][None], "w_xq": w_xq[l].astype(BF16), "xq_gain": xq_norm[l][None],
        "w_xo": w_xo[l].astype(BF16), "g_ffn": norm_ffn[l][None],
        "w_router": _pad_cols(w_router[l], LANES),
        "b_router": jnp.concatenate([b_router[l], jnp.full((LANES - N_EXPERTS,), NEG_BIG, F32)])[None],
    }

    def mixers(x, nseq, t, first_shift, s0, tm_mm, tm_prep, tb, dim_major):
        z = _norm_mm(x, norm_mix[l][None], w_in_p, tm_mm, 768)
        z3 = z.reshape(nseq, t, Z_COLS)
        *steps, g, bonus = _rwkv_prep(z, first_shift, prep, tm_prep)
        steps = [a.reshape(nseq, t, HEAD_PAIRS, LANES) for a in steps]
        ya, s_fin = _rwkv_scan(steps, g.reshape(nseq, t, RWKV_WIDTH), bonus.reshape(nseq, t, RWKV_WIDTH),
                               _state_to_pairs(s0), lnw, lnb, tb)
        fox = _fox_proj(z, qg, kg, bf, tm_prep, nseq if dim_major else None)
        return z, z3, ya.reshape(nseq * t, RWKV_WIDTH), _pairs_to_state(s_fin), fox

    zero_shift = jnp.zeros((bp, 1, Z_COLS), F32)
    zero_state = jnp.zeros((bp, RWKV_HEADS, HEAD_DIM, HEAD_DIM), F32)
    xp = x_prompt.reshape(n_p, D_MODEL)
    xs = x_sample.reshape(n_s, D_MODEL)
    zp, zp3, ya_p, st_p, (qb_p, kb_p, lf_p, kt_p, vt_p, vtb_p, lft_p) = mixers(
        xp, bp, sp, zero_shift, zero_state, 1024, 256, SCAN_BLOCK, True)
    shift_s = _pack_cols(jnp.pad(state_shift[l].reshape(bs, A_COLS), ((0, 0), (0, w_in.shape[2] - A_COLS))))
    zs, zs3, ya_s, st_s, (qb_s, kb_s, lf_s, kf_s) = mixers(
        xs, bs, ts, shift_s.reshape(bs, 1, Z_COLS), state_rwkv[l], 256, ts, ts, False)
    lf_s = lf_s[:, :FOX_HEADS]

    fp = jnp.cumsum(lft_p, axis=2) * LOG2E
    fq_p, fk_p = _prefix_terms(fp, bp, sp)
    yb_p = _fox_prompt(qb_p, kb_p, fq_p, fk_p, vtb_p.reshape(bp, FOX_HEADS, HEAD_DIM, sp), zp, bp, sp,
                       min(1024, sp))

    fs = jnp.cumsum(jnp.concatenate([cache_fox_logf[l].astype(F32), lf_s.reshape(bs, ts, FOX_HEADS)], axis=1),
                    axis=1) * LOG2E
    fn_s = fs[:, past:].transpose(0, 2, 1)
    kc_t = cache_fox_k[l].transpose(0, 2, 3, 1)
    vc_t = cache_fox_v[l].transpose(0, 2, 3, 1)
    yb_s = _fox_sample(qb_s, kc_t, vc_t, kb_s, zs, fn_s.reshape(bs, FOX_HEADS * ts, 1),
                       fs[:, :past].transpose(0, 2, 1), fn_s, bs, ts, min(512, past))


    kv_p = _mem_kv(mem_prompt.reshape(bp * N_MEM, D_MODEL), norm_mem[l][None], w_kv, xk_norm[l][None], 256)
    mk_p = kv_p[:, :MEM_WIDTH].reshape(bp, N_MEM, MEM_WIDTH)
    mv_p = kv_p[:, MEM_WIDTH:].reshape(bp, N_MEM, MEM_WIDTH)
    x2_p, h_p, ti_p, tg_p = _cross_router(xp, ya_p, yb_p, w_out_b, mk_p.astype(BF16), mv_p.astype(BF16), pc, bp,
                                          256)
    mk_s = cache_mem_k[l].reshape(bs, N_MEM, MEM_WIDTH).astype(BF16)
    mv_s = cache_mem_v[l].reshape(bs, N_MEM, MEM_WIDTH).astype(BF16)
    x2_s, h_s, ti_s, tg_s = _cross_router(xs, ya_s, yb_s, w_out_b, mk_s, mv_s, pc, bs, ts)

    h_all = jnp.concatenate([h_p, h_s], axis=0)
    top_i = jnp.concatenate([ti_p[:, :TOP_K], ti_s[:, :TOP_K]], axis=0)
    gates = jnp.concatenate([tg_p[:, :TOP_K], tg_s[:, :TOP_K]], axis=0)
    row_token, pos, tile_expert, tile_valid = _route(top_i)
    out_rows = _moe_experts(tile_expert, tile_valid, h_all[row_token], w_gu[l], b_gu[l][:, None, :], w_down[l],
                            b_down[l][:, None, :])
    def combine(x2, rows, gate):
        return x2 + sum(out_rows[rows[:, k]] * gate[:, k:k + 1] for k in range(TOP_K))

    y_p = combine(x2_p, pos[:n_p], gates[:n_p]).reshape(bp, sp, D_MODEL)
    y_s = combine(x2_s, pos[n_p:], gates[n_p:]).reshape(bs, ts, D_MODEL)

    to_cache = lambda a: a.reshape(1, bp, FOX_HEADS, HEAD_DIM, sp).transpose(0, 1, 4, 2, 3)
    pk, pv, plf = to_cache(kt_p), to_cache(vt_p), lft_p.transpose(0, 2, 1)[None]
    sk = kf_s.reshape(1, bs, ts, FOX_HEADS, HEAD_DIM)
    sv = zs3[..., C_FV:C_FV + FOX_WIDTH].reshape(1, bs, ts, FOX_HEADS, HEAD_DIM)
    slf = lf_s.reshape(1, bs, ts, FOX_HEADS)
    return (y_p, y_s, pk, pv, plf, st_p[None], _unpack_a_cols(zp3[:, -1:])[None],
            mk_p.reshape(1, bp, N_MEM, MEM_HEADS, MEM_HEAD_DIM), mv_p.reshape(1, bp, N_MEM, MEM_HEADS, MEM_HEAD_DIM),
            sk, sv, slf, st_s[None], _unpack_a_cols(zs3[:, -1:])[None])
```
